```python
import math
import jax
import jax.numpy as jnp
from jax import lax
import numpy as np

D_MODEL = 4096
BATCH = 2
SEQ = 8192
DEPTH = 4
DEC_BATCH = 1
DEC_SEQ = 16384
PAST_LEN = 128

HEAD_DIM = 128
GRID_W = 64
H_A = 16
DIL_PATTERNS = ((128, 1), (512, 4), (2048, 16))
H_B = 16
RET_CHUNK = 128
H_C = 16
KV_C = 4
H_D = 16
Q_LORA = 768
KV_LORA = 256
QK_NOPE = 128
QK_ROPE = 64
V_D = 128
N_BUCKETS = 32
MAX_DIST = 1024
ROPE_THETA = 10000.0
D_FF = 11008
CONV_W = 3
Q_BLOCK = 128
EPS = 1e-6
NEG = -1e30

N_EVEN = (DEPTH + 1) // 2
N_ODD = DEPTH // 2
W_A = H_A * HEAD_DIM
W_B = H_B * HEAD_DIM
W_CQ = H_C * HEAD_DIM
W_CKV = KV_C * HEAD_DIM
IN_EVEN = 3 * W_A + 4 * W_B
IN_ODD = W_CQ + 2 * W_CKV + Q_LORA + KV_LORA + QK_ROPE
MIX_EVEN = W_A + W_B
MIX_ODD = W_CQ + H_D * V_D

kernel_name = 'hybrid_bidir_encoder'


def rms_norm(x, w):
    xf = x.astype(jnp.float32)
    y = xf * lax.rsqrt(jnp.mean(xf * xf, axis=-1, keepdims=True) + EPS)
    return (y * w.astype(jnp.float32)).astype(x.dtype)


def rope_angles(pos, dim):
    inv = ROPE_THETA ** (-jnp.arange(0, dim, 2, dtype=jnp.float32) / dim)
    ang = pos.astype(jnp.float32)[:, None] * inv[None, :]
    return jnp.cos(ang), jnp.sin(ang)


def apply_rope(x, cos, sin):
    xf = x.astype(jnp.float32)
    x1, x2 = jnp.split(xf, 2, axis=-1)
    c = cos[None, :, None, :]
    s = sin[None, :, None, :]
    return jnp.concatenate([x1 * c - x2 * s, x1 * s + x2 * c], axis=-1).astype(x.dtype)


def t5_bucket(rel):
    nb = N_BUCKETS // 2
    max_exact = nb // 2
    n = jnp.abs(rel)
    log_ratio = jnp.log(jnp.maximum(n, 1).astype(jnp.float32) / max_exact) / math.log(MAX_DIST / max_exact)
    large = jnp.minimum(max_exact + (log_ratio * (nb - max_exact)).astype(jnp.int32), nb - 1)
    return jnp.where(rel > 0, nb, 0) + jnp.where(n < max_exact, n, large)


def dilated_window_attention(q, k, v, rel_bias, window, dil):
    Bn, T, H, dh = q.shape
    half = window // (2 * dil)
    L = T // dil
    nblk = -(-L // half)
    Lp = nblk * half
    N = Bn * dil

    def to_strided(a):
        a = a.reshape(Bn, L, dil, H, dh).transpose(0, 2, 1, 3, 4).reshape(N, L, H, dh)
        return jnp.pad(a, ((0, 0), (0, Lp - L), (0, 0), (0, 0)))

    def windows(a):
        ap = jnp.pad(a, ((0, 0), (half, half), (0, 0), (0, 0))).reshape(N, nblk + 2, half, H, dh)
        return jnp.concatenate([ap[:, :-2], ap[:, 1:-1], ap[:, 2:]], axis=2)

    qb = to_strided(q).reshape(N, nblk, half, H, dh)
    kw = windows(to_strided(k))
    vw = windows(to_strided(v))
    kj = jnp.arange(3 * half)
    rel = kj[None, :] - half - jnp.arange(half)[:, None]
    key_idx = jnp.arange(nblk)[:, None, None] * half + kj[None, None, :] - half
    valid = (jnp.abs(rel) <= half)[None] & (key_idx >= 0) & (key_idx < L)
    bias = rel_bias[t5_bucket(rel * dil)].transpose(2, 0, 1).astype(jnp.float32)
    s = jnp.einsum('nbqhd,nbkhd->nbhqk', qb, kw, preferred_element_type=jnp.float32) * (dh ** -0.5)
    s = jnp.where(valid[None, :, None], s + bias[None, None], NEG)
    m = jnp.max(s, axis=-1, keepdims=True)
    p = jnp.exp(s - m)
    den = jnp.sum(p, axis=-1, keepdims=True)
    o = jnp.einsum('nbhqk,nbkhd->nbqhd', (p / den).astype(v.dtype), vw)
    lse = (m + jnp.log(den))[..., 0]
    o = o.reshape(N, Lp, H, dh)[:, :L].reshape(Bn, dil, L, H, dh).transpose(0, 2, 1, 3, 4).reshape(Bn, T, H, dh)
    lse = lse.transpose(0, 1, 3, 2).reshape(N, Lp, H)[:, :L].reshape(Bn, dil, L, H).transpose(0, 2, 1, 3).reshape(Bn, T, H)
    return o, lse


def retention_bidir(q, k, v, log_1m_gamma):
    Bn, T, H, dh = q.shape
    C = RET_CHUNK
    nc = T // C
    lg = jnp.log1p(-jnp.exp(log_1m_gamma.astype(jnp.float32)))
    qf = q.astype(jnp.float32).reshape(Bn, nc, C, H, dh) * (dh ** -0.5)
    kf = k.astype(jnp.float32).reshape(Bn, nc, C, H, dh)
    vf = v.astype(jnp.float32).reshape(Bn, nc, C, H, dh)
    n = jnp.arange(C, dtype=jnp.float32)
    rel = n[:, None] - n[None, :]
    d_fwd = jnp.where(rel >= 0, jnp.exp(lg[0][:, None, None] * jnp.maximum(rel, 0.0)), 0.0)
    d_bwd = jnp.where(rel < 0, jnp.exp(lg[1][:, None, None] * jnp.maximum(-rel, 0.0)), 0.0)
    qk = jnp.einsum('bcnhd,bcmhd->bchnm', qf, kf)
    intra = jnp.einsum('bchnm,bcmhd->bcnhd', qk * (d_fwd + d_bwd)[None, None], vf)

    def scan_states(u, decay, reverse):
        def step(r, u_c):
            return decay * r + u_c, r
        _, prev = lax.scan(step, jnp.zeros_like(u[0]), u, reverse=reverse)
        return prev

    wk_f = jnp.exp(lg[0][None, :] * (C - 1 - n)[:, None])
    wq_f = jnp.exp(lg[0][None, :] * (n + 1)[:, None])
    u_f = jnp.einsum('bcmhd,bcmhe,mh->cbhde', kf, vf, wk_f)
    r_f = scan_states(u_f, jnp.exp(lg[0] * C)[None, :, None, None], False)
    cross_f = jnp.einsum('bcnhd,cbhde,nh->bcnhe', qf, r_f, wq_f)
    wk_b = jnp.exp(lg[1][None, :] * n[:, None])
    wq_b = jnp.exp(lg[1][None, :] * (C - n)[:, None])
    u_b = jnp.einsum('bcmhd,bcmhe,mh->cbhde', kf, vf, wk_b)
    r_b = scan_states(u_b, jnp.exp(lg[1] * C)[None, :, None, None], True)
    cross_b = jnp.einsum('bcnhd,cbhde,nh->bcnhe', qf, r_b, wq_b)

    o = intra + cross_f + cross_b
    mu = jnp.mean(o, axis=-1, keepdims=True)
    var = jnp.mean(jnp.square(o - mu), axis=-1, keepdims=True)
    o = (o - mu) * lax.rsqrt(var + EPS)
    return o.reshape(Bn, T, H * dh)


def block_attention(q, k, v):
    Bn, T, Hq, dq = q.shape
    Hk = k.shape[2]
    G = Hq // Hk
    nq = T // Q_BLOCK
    qb = q.reshape(Bn, nq, Q_BLOCK, Hk, G, dq).transpose(1, 0, 2, 3, 4, 5)
    scale = dq ** -0.5

    def one_block(q_blk):
        s = jnp.einsum('bqhgd,bkhd->bhgqk', q_blk, k, preferred_element_type=jnp.float32) * scale
        p = jax.nn.softmax(s, axis=-1)
        return jnp.einsum('bhgqk,bkhd->bqhgd', p.astype(v.dtype), v)

    o = lax.map(one_block, qb)
    return o.transpose(1, 0, 2, 3, 4, 5).reshape(Bn, T, Hq * v.shape[-1])


def mix_even(h, w_in, log_1m_gamma, w_out, rel_bias):
    Bn, T, _ = h.shape
    proj = h @ w_in
    qa, ka, va, qb, kb, vb, gb = jnp.split(
        proj, [W_A, 2 * W_A, 3 * W_A, 3 * W_A + W_B, 3 * W_A + 2 * W_B, 3 * W_A + 3 * W_B], axis=-1)
    qa = qa.reshape(Bn, T, H_A, HEAD_DIM)
    ka = ka.reshape(Bn, T, H_A, HEAD_DIM)
    va = va.reshape(Bn, T, H_A, HEAD_DIM)
    outs = []
    lses = []
    for window, dil in DIL_PATTERNS:
        o_p, lse_p = dilated_window_attention(qa, ka, va, rel_bias, window, dil)
        outs.append(o_p)
        lses.append(lse_p)
    wts = jax.nn.softmax(jnp.stack(lses, axis=-1), axis=-1)
    oa = jnp.einsum('bthp,pbthd->bthd', wts, jnp.stack(outs, axis=0).astype(jnp.float32))
    oa = oa.reshape(Bn, T, W_A).astype(h.dtype)

    cos, sin = rope_angles(jnp.arange(T), HEAD_DIM)
    qb = apply_rope(qb.reshape(Bn, T, H_B, HEAD_DIM), cos, sin)
    kb = apply_rope(kb.reshape(Bn, T, H_B, HEAD_DIM), cos, sin)
    ob = retention_bidir(qb, kb, vb.reshape(Bn, T, H_B, HEAD_DIM), log_1m_gamma).astype(h.dtype)
    ob = jax.nn.silu(gb) * ob
    return jnp.concatenate([oa, ob], axis=-1) @ w_out


def mix_odd(h, w_in, qk_norm, cq_norm, ckv_norm, w_uq, w_ukv, w_out):
    Bn, T, _ = h.shape
    proj = h @ w_in
    o1 = W_CQ
    o2 = o1 + W_CKV
    o3 = o2 + W_CKV
    o4 = o3 + Q_LORA
    o5 = o4 + KV_LORA
    qc, kc, vc, cq, ckv, k_pe = jnp.split(proj, [o1, o2, o3, o4, o5], axis=-1)

    rows = T // GRID_W
    row_pos = jnp.repeat(jnp.arange(rows), GRID_W)
    col_pos = jnp.tile(jnp.arange(GRID_W), rows)
    half_d = HEAD_DIM // 2
    cos_r, sin_r = rope_angles(row_pos, half_d)
    cos_c, sin_c = rope_angles(col_pos, half_d)

    def axial_rope(x):
        return jnp.concatenate([apply_rope(x[..., :half_d], cos_r, sin_r),
                                apply_rope(x[..., half_d:], cos_c, sin_c)], axis=-1)

    qc = axial_rope(rms_norm(qc.reshape(Bn, T, H_C, HEAD_DIM), qk_norm[0]))
    kc = axial_rope(rms_norm(kc.reshape(Bn, T, KV_C, HEAD_DIM), qk_norm[1]))
    oc = block_attention(qc, kc, vc.reshape(Bn, T, KV_C, HEAD_DIM))

    cos1, sin1 = rope_angles(jnp.arange(T), QK_ROPE)
    qd = (rms_norm(cq, cq_norm) @ w_uq).reshape(Bn, T, H_D, QK_NOPE + QK_ROPE)
    kvd = (rms_norm(ckv, ckv_norm) @ w_ukv).reshape(Bn, T, H_D, QK_NOPE + V_D)
    q_d = jnp.concatenate([qd[..., :QK_NOPE], apply_rope(qd[..., QK_NOPE:], cos1, sin1)], axis=-1)
    k_rope = apply_rope(k_pe[:, :, None, :], cos1, sin1)
    k_d = jnp.concatenate([kvd[..., :QK_NOPE], jnp.broadcast_to(k_rope, (Bn, T, H_D, QK_ROPE))], axis=-1)
    od = block_attention(q_d, k_d, kvd[..., QK_NOPE:])
    return jnp.concatenate([oc, od], axis=-1) @ w_out


def conv_ffn(h, w_up, conv_w, w_down):
    u = h @ w_up
    up = jnp.pad(u, ((0, 0), (1, 1), (0, 0)))
    u = up[:, :-2] * conv_w[0] + up[:, 1:-1] * conv_w[1] + up[:, 2:] * conv_w[2]
    g, val = jnp.split(u, 2, axis=-1)
    return (jax.nn.silu(g) * val) @ w_down


def encoder_trunk(x, c, rel_bias, norm_w, w_mod, b_mod, w_in_even, ret_log_1m_gamma, w_out_even,
                  w_in_odd, c_qk_norm, mla_cq_norm, mla_ckv_norm, w_uq, w_ukv, w_out_odd,
                  w_ff_up, conv_ff, w_ff_down, final_norm_w):
    for layer in range(DEPTH):
        mod = (jax.nn.silu(c) @ w_mod[layer] + b_mod[layer]).reshape(c.shape[0], 6, 1, D_MODEL)
        h = rms_norm(x, norm_w[layer, 0]) * (1.0 + mod[:, 1]) + mod[:, 0]
        i = layer // 2
        if layer % 2 == 0:
            y = mix_even(h, w_in_even[i], ret_log_1m_gamma[i], w_out_even[i], rel_bias)
        else:
            y = mix_odd(h, w_in_odd[i], c_qk_norm[i], mla_cq_norm[i], mla_ckv_norm[i],
                        w_uq[i], w_ukv[i], w_out_odd[i])
        x = x + mod[:, 2] * y
        h = rms_norm(x, norm_w[layer, 1]) * (1.0 + mod[:, 4]) + mod[:, 3]
        x = x + mod[:, 5] * conv_ffn(h, w_ff_up[layer], conv_ff[layer], w_ff_down[layer])
    return rms_norm(x, final_norm_w)


def setup_inputs(seed: int = 0) -> dict:
    key = jax.random.key(seed)
    ks = jax.random.split(key, 24)

    def nrm(k, shape, s):
        return jax.random.normal(k, shape, jnp.float32) * s

    ret_base = (-5.0 - jnp.arange(H_B, dtype=jnp.float32)) * math.log(2.0)
    return {
        'x_prompt': nrm(ks[0], (BATCH, SEQ, D_MODEL), 1.0),
        'x_sample': nrm(ks[1], (DEC_BATCH, DEC_SEQ, D_MODEL), 1.0),
        'c_prompt': nrm(ks[2], (BATCH, D_MODEL), 1.0),
        'c_sample': nrm(ks[3], (DEC_BATCH, D_MODEL), 1.0),
        'rel_bias': nrm(ks[4], (N_BUCKETS, H_A), 0.1),
        'norm_w': 1.0 + nrm(ks[5], (DEPTH, 2, D_MODEL), 0.02),
        'w_mod': nrm(ks[6], (DEPTH, D_MODEL, 6 * D_MODEL), 0.5 * D_MODEL ** -0.5),
        'b_mod': nrm(ks[7], (DEPTH, 6 * D_MODEL), 0.02),
        'w_in_even': nrm(ks[8], (N_EVEN, D_MODEL, IN_EVEN), D_MODEL ** -0.5),
        'ret_log_1m_gamma': ret_base + nrm(ks[9], (N_EVEN, 2, H_B), 0.1),
        'w_out_even': nrm(ks[10], (N_EVEN, MIX_EVEN, D_MODEL), MIX_EVEN ** -0.5),
        'w_in_odd': nrm(ks[11], (N_ODD, D_MODEL, IN_ODD), D_MODEL ** -0.5),
        'c_qk_norm': 1.0 + nrm(ks[12], (N_ODD, 2, HEAD_DIM), 0.02),
        'mla_cq_norm': 1.0 + nrm(ks[13], (N_ODD, Q_LORA), 0.02),
        'mla_ckv_norm': 1.0 + nrm(ks[14], (N_ODD, KV_LORA), 0.02),
        'w_uq': nrm(ks[15], (N_ODD, Q_LORA, H_D * (QK_NOPE + QK_ROPE)), Q_LORA ** -0.5),
        'w_ukv': nrm(ks[16], (N_ODD, KV_LORA, H_D * (QK_NOPE + V_D)), KV_LORA ** -0.5),
        'w_out_odd': nrm(ks[17], (N_ODD, MIX_ODD, D_MODEL), MIX_ODD ** -0.5),
        'w_ff_up': nrm(ks[18], (DEPTH, D_MODEL, 2 * D_FF), D_MODEL ** -0.5),
        'conv_ff': nrm(ks[19], (DEPTH, CONV_W, 2 * D_FF), CONV_W ** -0.5),
        'w_ff_down': nrm(ks[20], (DEPTH, D_FF, D_MODEL), D_FF ** -0.5),
        'final_norm_w': 1.0 + nrm(ks[21], (D_MODEL,), 0.02),
    }


def reference(x_prompt, x_sample, c_prompt, c_sample, rel_bias, norm_w, w_mod, b_mod, w_in_even,
              ret_log_1m_gamma, w_out_even, w_in_odd, c_qk_norm, mla_cq_norm, mla_ckv_norm, w_uq,
              w_ukv, w_out_odd, w_ff_up, conv_ff, w_ff_down, final_norm_w):
    y_prompt = encoder_trunk(x_prompt, c_prompt, rel_bias, norm_w, w_mod, b_mod, w_in_even,
                             ret_log_1m_gamma, w_out_even, w_in_odd, c_qk_norm, mla_cq_norm,
                             mla_ckv_norm, w_uq, w_ukv, w_out_odd, w_ff_up, conv_ff, w_ff_down,
                             final_norm_w)
    y_sample = encoder_trunk(x_sample, c_sample, rel_bias, norm_w, w_mod, b_mod, w_in_even,
                             ret_log_1m_gamma, w_out_even, w_in_odd, c_qk_norm, mla_cq_norm,
                             mla_ckv_norm, w_uq, w_ukv, w_out_odd, w_ff_up, conv_ff, w_ff_down,
                             final_norm_w)
    return (y_prompt, y_sample)
```

```python
import functools
import math

import numpy as np
import jax
import jax.numpy as jnp
from jax import lax
from jax.experimental import pallas as pl
from jax.experimental.pallas import tpu as pltpu

F32 = jnp.float32
BF16 = jnp.bfloat16

HEAD_DIM = 128
GRID_W = 64
DIL_PATTERNS = ((128, 1), (512, 4), (2048, 16))
RET_CHUNK = 128
KV_C = 4
H_D = 16
Q_LORA = 768
KV_LORA = 256
QK_NOPE = 128
QK_ROPE = 64
V_D = 128
N_BUCKETS = 32
MAX_DIST = 1024
ROPE_THETA = 10000.0
EPS = 1e-6
NEG = -1e30

WIN_HALF = 64
WIN_TQ = 128
VMEM_LIMIT = 56 * 1024 * 1024


def _cparams(sem):
    return pltpu.CompilerParams(dimension_semantics=sem, vmem_limit_bytes=VMEM_LIMIT)


def _seq_id(row, seq_starts):
    s = 0
    for st in seq_starts[1:]:
        s = s + (row >= st).astype(jnp.int32)
    return s


def _is_any(row, marks):
    r = row == marks[0]
    for m in marks[1:]:
        r = jnp.logical_or(r, row == m)
    return r


def _mm_body(*refs, nk, mode):
    a_ref, b_ref = refs[0], refs[1]
    if mode == "plain":
        extra, (o_ref, acc_ref) = (), refs[2:]
    elif mode == "bias":
        extra, (o_ref, acc_ref) = refs[2:3], refs[3:]
    else:
        extra, (o_ref, acc_ref) = refs[2:4], refs[4:]

    def epilogue(r):
        if mode == "bias":
            r = r + extra[0][...]
        elif mode == "resid":
            r = extra[0][...] + extra[1][0] * r
        o_ref[...] = r.astype(o_ref.dtype)

    prod = jnp.dot(a_ref[...].astype(BF16), b_ref[...].astype(BF16), preferred_element_type=F32)
    if nk == 1:
        epilogue(prod)
        return
    k = pl.program_id(2)

    @pl.when(k == 0)
    def _():
        acc_ref[...] = prod

    @pl.when(jnp.logical_and(k > 0, k < nk - 1))
    def _():
        acc_ref[...] += prod

    @pl.when(k == nk - 1)
    def _():
        epilogue(acc_ref[...] + prod)


def matmul(a, b, *, out_dtype, tm=1024, tn=1024, tk=1024, bias=None, resid=None, gate=None,
           seq_starts=None, name="mm"):
    m, kd = a.shape
    _, n = b.shape
    tm, tn, tk = min(tm, m), _largest_tile(n, tn), _largest_tile(kd, tk)
    assert m % tm == 0 and n % tn == 0 and kd % tk == 0, (a.shape, b.shape, tm, tn, tk)
    nk = kd // tk
    in_specs = [pl.BlockSpec((tm, tk), lambda i, j, k: (i, k)),
                pl.BlockSpec((tk, tn), lambda i, j, k: (k, j))]
    args = [a, b]
    if bias is not None:
        mode = "bias"
        in_specs.append(pl.BlockSpec((1, tn), lambda i, j, k: (0, j)))
        args.append(bias)
    elif resid is not None:
        mode = "resid"
        in_specs.append(pl.BlockSpec((tm, tn), lambda i, j, k: (i, j)))
        in_specs.append(pl.BlockSpec((1, 1, tn), lambda i, j, k: (_seq_id(i * tm, seq_starts), 0, j)))
        args += [resid, gate]
    else:
        mode = "plain"
    return pl.pallas_call(
        functools.partial(_mm_body, nk=nk, mode=mode),
        out_shape=jax.ShapeDtypeStruct((m, n), out_dtype),
        grid=(m // tm, n // tn, nk),
        in_specs=in_specs,
        out_specs=pl.BlockSpec((tm, tn), lambda i, j, k: (i, j)),
        scratch_shapes=[pltpu.VMEM((tm, tn) if nk > 1 else (8, 128), F32)],
        compiler_params=_cparams(("parallel", "parallel", "arbitrary")),
        name=name,
    )(*args)


def _normmod_body(x_ref, w_ref, sc_ref, sh_ref, o_ref):
    x = x_ref[...]
    y = x * lax.rsqrt(jnp.mean(x * x, axis=-1, keepdims=True) + EPS) * w_ref[...]
    o_ref[...] = (y * (1.0 + sc_ref[0]) + sh_ref[0]).astype(o_ref.dtype)


def norm_mod(x, w, mod6, scale_idx, shift_idx, seq_starts, tm=256):
    n, d = x.shape
    tm = min(tm, n)
    return pl.pallas_call(
        _normmod_body,
        out_shape=jax.ShapeDtypeStruct((n, d), BF16),
        grid=(n // tm,),
        in_specs=[pl.BlockSpec((tm, d), lambda i: (i, 0)),
                  pl.BlockSpec((1, d), lambda i: (0, 0)),
                  pl.BlockSpec((1, 1, d), lambda i: (_seq_id(i * tm, seq_starts) * 6 + scale_idx, 0, 0)),
                  pl.BlockSpec((1, 1, d), lambda i: (_seq_id(i * tm, seq_starts) * 6 + shift_idx, 0, 0))],
        out_specs=pl.BlockSpec((tm, d), lambda i: (i, 0)),
        compiler_params=_cparams(("parallel",)),
        name="norm_mod",
    )(x, w.reshape(1, d), mod6, mod6)


def _final_norm_body(x_ref, w_ref, o_ref):
    x = x_ref[...]
    o_ref[...] = x * lax.rsqrt(jnp.mean(x * x, axis=-1, keepdims=True) + EPS) * w_ref[...]


def final_norm(x, w, tm=256):
    n, d = x.shape
    tm = min(tm, n)
    return pl.pallas_call(
        _final_norm_body,
        out_shape=jax.ShapeDtypeStruct((n, d), F32),
        grid=(n // tm,),
        in_specs=[pl.BlockSpec((tm, d), lambda i: (i, 0)), pl.BlockSpec((1, d), lambda i: (0, 0))],
        out_specs=pl.BlockSpec((tm, d), lambda i: (i, 0)),
        compiler_params=_cparams(("parallel",)),
        name="final_norm",
    )(x, w.reshape(1, d))


def _convgate_body(ug_ref, uv_ref, pg_ref, pv_ref, ng_ref, nv_ref, cwg_ref, cwv_ref, o_ref, *,
                   tm, seq_starts, seq_ends):
    i = pl.program_id(0)
    first = _is_any(i * tm, seq_starts)
    last = _is_any((i + 1) * tm, seq_ends)
    row = lax.broadcasted_iota(jnp.int32, ug_ref.shape, 0)

    def conv(u_ref, p_ref, n_ref, cw_ref):
        u = u_ref[...]
        prev_row = jnp.where(first, 0.0, p_ref[7:8, :])
        next_row = jnp.where(last, 0.0, n_ref[0:1, :])
        u_prev = jnp.where(row == 0, prev_row, pltpu.roll(u, 1, axis=0))
        u_next = jnp.where(row == tm - 1, next_row, pltpu.roll(u, tm - 1, axis=0))
        cw = cw_ref[...]
        return u_prev * cw[0:1] + u * cw[1:2] + u_next * cw[2:3]

    g = conv(ug_ref, pg_ref, ng_ref, cwg_ref)
    val = conv(uv_ref, pv_ref, nv_ref, cwv_ref)
    o_ref[...] = (g * jax.nn.sigmoid(g) * val).astype(o_ref.dtype)


def conv_gate(u, conv_w, dff, seq_starts, seq_ends, tm=256, tn=1024):
    n = u.shape[0]
    tm, tn = min(tm, n), min(tn, dff)
    nj = dff // tn
    r8 = tm // 8
    nb8 = n // 8
    prev = lambda i: jnp.maximum(i * r8 - 1, 0)
    nxt = lambda i: jnp.minimum((i + 1) * r8, nb8 - 1)
    return pl.pallas_call(
        functools.partial(_convgate_body, tm=tm, seq_starts=seq_starts, seq_ends=seq_ends),
        out_shape=jax.ShapeDtypeStruct((n, dff), BF16),
        grid=(n // tm, nj),
        in_specs=[pl.BlockSpec((tm, tn), lambda i, j: (i, j)),
                  pl.BlockSpec((tm, tn), lambda i, j: (i, j + nj)),
                  pl.BlockSpec((8, tn), lambda i, j: (prev(i), j)),
                  pl.BlockSpec((8, tn), lambda i, j: (prev(i), j + nj)),
                  pl.BlockSpec((8, tn), lambda i, j: (nxt(i), j)),
                  pl.BlockSpec((8, tn), lambda i, j: (nxt(i), j + nj)),
                  pl.BlockSpec((3, tn), lambda i, j: (0, j)),
                  pl.BlockSpec((3, tn), lambda i, j: (0, j + nj))],
        out_specs=pl.BlockSpec((tm, tn), lambda i, j: (i, j)),
        compiler_params=_cparams(("parallel", "parallel")),
        name="conv_gate",
    )(u, u, u, u, u, u, conv_w, conv_w)


def _flash_body(q_ref, k_ref, v_ref, o_ref, m_ref, l_ref, acc_ref, *, g_heads, dqk, dv, nkv):
    j = pl.program_id(3)

    @pl.when(j == 0)
    def _():
        m_ref[...] = jnp.full(m_ref.shape, -jnp.inf, F32)
        l_ref[...] = jnp.zeros(l_ref.shape, F32)
        acc_ref[...] = jnp.zeros(acc_ref.shape, F32)

    k = k_ref[...]
    v = v_ref[...]
    for g in range(g_heads):
        q = q_ref[:, g * dqk:(g + 1) * dqk]
        s = lax.dot_general(q, k, (((1,), (1,)), ((), ())), preferred_element_type=F32)
        m_prev = m_ref[g]
        m_new = jnp.maximum(m_prev, jnp.max(s, axis=-1, keepdims=True))
        alpha = jnp.exp(m_prev - m_new)
        p = jnp.exp(s - m_new)
        l_ref[g] = alpha * l_ref[g] + jnp.sum(p, axis=-1, keepdims=True)
        acc_ref[g] = alpha * acc_ref[g] + jnp.dot(p.astype(BF16), v, preferred_element_type=F32)
        m_ref[g] = m_new

    @pl.when(j == nkv - 1)
    def _():
        for g in range(g_heads):
            o_ref[:, g * dv:(g + 1) * dv] = (acc_ref[g] / l_ref[g]).astype(o_ref.dtype)


def flash_attention(q, k, v, *, n_kv_heads, g_heads, dqk, dv, row0, batch, seq_len, tq, tk, name):
    tq, tk = min(tq, seq_len), min(tk, seq_len)
    nq, nkv = seq_len // tq, seq_len // tk
    qb0, kb0 = row0 // tq, row0 // tk
    return pl.pallas_call(
        functools.partial(_flash_body, g_heads=g_heads, dqk=dqk, dv=dv, nkv=nkv),
        out_shape=jax.ShapeDtypeStruct((batch * seq_len, n_kv_heads * g_heads * dv), BF16),
        grid=(batch, n_kv_heads, nq, nkv),
        in_specs=[pl.BlockSpec((tq, g_heads * dqk), lambda b, h, i, j: (qb0 + b * nq + i, h)),
                  pl.BlockSpec((tk, dqk), lambda b, h, i, j: (kb0 + b * nkv + j, h)),
                  pl.BlockSpec((tk, dv), lambda b, h, i, j: (kb0 + b * nkv + j, h))],
        out_specs=pl.BlockSpec((tq, g_heads * dv), lambda b, h, i, j: (b * nq + i, h)),
        scratch_shapes=[pltpu.VMEM((g_heads, tq, 1), F32), pltpu.VMEM((g_heads, tq, 1), F32),
                        pltpu.VMEM((g_heads, tq, dv), F32)],
        compiler_params=_cparams(("parallel", "parallel", "parallel", "arbitrary")),
        name=name,
    )(q, k, v)


def _win_body(q_ref, kp_ref, kc_ref, kn_ref, vp_ref, vc_ref, vn_ref, bias_ref, o_ref, lse_ref, *,
              n_heads, tq, dil, seq_starts, seq_ends):
    i = pl.program_id(1)
    first = _is_any(i * (tq * dil), seq_starts)
    last = _is_any((i + 1) * (tq * dil), seq_ends)
    nkeys = tq + 2 * WIN_HALF
    col = lax.broadcasted_iota(jnp.int32, (tq, nkeys), 1)
    edge_bad = jnp.logical_or(jnp.logical_and(col < WIN_HALF, first),
                              jnp.logical_and(col >= WIN_HALF + tq, last))
    scale = HEAD_DIM ** -0.5
    for h in range(n_heads):
        sl = slice(h * HEAD_DIM, (h + 1) * HEAD_DIM)
        q = (q_ref[:, sl] * scale).astype(BF16)
        k = jnp.concatenate([kp_ref[:, sl], kc_ref[:, sl], kn_ref[:, sl]], axis=0).astype(BF16)
        v = jnp.concatenate([vp_ref[:, sl], vc_ref[:, sl], vn_ref[:, sl]], axis=0).astype(BF16)
        s = lax.dot_general(q, k, (((1,), (1,)), ((), ())), preferred_element_type=F32) + bias_ref[h]
        s = jnp.where(edge_bad, NEG, s)
        m = jnp.max(s, axis=-1, keepdims=True)
        p = jnp.exp(s - m)
        den = jnp.sum(p, axis=-1, keepdims=True)
        o = jnp.dot(p.astype(BF16), v, preferred_element_type=F32) / den
        o_ref[:, sl] = o
        lse_ref[:, sl] = jnp.broadcast_to(m + jnp.log(den), (tq, HEAD_DIM))


def _t5_bucket(rel):
    nb = N_BUCKETS // 2
    max_exact = nb // 2
    n = jnp.abs(rel)
    log_ratio = jnp.log(jnp.maximum(n, 1).astype(F32) / max_exact) / math.log(MAX_DIST / max_exact)
    large = jnp.minimum(max_exact + (log_ratio * (nb - max_exact)).astype(jnp.int32), nb - 1)
    return jnp.where(rel > 0, nb, 0) + jnp.where(n < max_exact, n, large)


def _window_bias(rel_bias, dil, tq):
    kj = jnp.arange(tq + 2 * WIN_HALF)
    rel = kj[None, :] - WIN_HALF - jnp.arange(tq)[:, None]
    bias = rel_bias[_t5_bucket(rel * dil)].astype(F32)
    bias = jnp.where((jnp.abs(rel) <= WIN_HALF)[:, :, None], bias, NEG)
    return bias.transpose(2, 0, 1)


def window_attention(proj, rel_bias, dil, n_heads, seq_starts, seq_ends):
    n, c = proj.shape
    w = n_heads * HEAD_DIM
    assert c % w == 0
    cb = c // w
    tq = WIN_TQ
    rows = n // dil
    view = proj.reshape(rows, dil * c)
    bias = _window_bias(rel_bias, dil, tq)
    hb = tq // WIN_HALF
    nhb = rows // WIN_HALF
    prev = lambda i: jnp.maximum(i * hb - 1, 0)
    nxt = lambda i: jnp.minimum((i + 1) * hb, nhb - 1)
    cur_spec = lambda off: pl.BlockSpec((tq, w), lambda r, i: (i, r * cb + off))
    prev_spec = lambda off: pl.BlockSpec((WIN_HALF, w), lambda r, i: (prev(i), r * cb + off))
    next_spec = lambda off: pl.BlockSpec((WIN_HALF, w), lambda r, i: (nxt(i), r * cb + off))
    out_spec = pl.BlockSpec((tq, w), lambda r, i: (i, r))
    o, lse = pl.pallas_call(
        functools.partial(_win_body, n_heads=n_heads, tq=tq, dil=dil, seq_starts=seq_starts, seq_ends=seq_ends),
        out_shape=[jax.ShapeDtypeStruct((rows, dil * w), F32)] * 2,
        grid=(dil, rows // tq),
        in_specs=[cur_spec(0), prev_spec(1), cur_spec(1), next_spec(1),
                  prev_spec(2), cur_spec(2), next_spec(2),
                  pl.BlockSpec((n_heads, tq, tq + 2 * WIN_HALF), lambda r, i: (0, 0, 0))],
        out_specs=[out_spec, out_spec],
        compiler_params=_cparams(("parallel", "parallel")),
        name=f"win_attn_d{dil}",
    )(view, view, view, view, view, view, view, bias)
    return o.reshape(n, w), lse.reshape(n, w)


def _wincomb_body(o1, o2, o3, l1, l2, l3, out_ref):
    a, b, c = l1[...], l2[...], l3[...]
    m = jnp.maximum(jnp.maximum(a, b), c)
    wa, wb, wc = jnp.exp(a - m), jnp.exp(b - m), jnp.exp(c - m)
    out_ref[...] = ((wa * o1[...] + wb * o2[...] + wc * o3[...]) / (wa + wb + wc)).astype(out_ref.dtype)


def window_combine(outs, lses, tm=256):
    n, w = outs[0].shape
    tm = min(tm, n)
    spec = pl.BlockSpec((tm, w), lambda i: (i, 0))
    return pl.pallas_call(
        _wincomb_body,
        out_shape=jax.ShapeDtypeStruct((n, w), BF16),
        grid=(n // tm,),
        in_specs=[spec] * 6,
        out_specs=spec,
        compiler_params=_cparams(("parallel",)),
        name="win_combine",
    )(*outs, *lses)


def _rope_half(x, cos, sin_signed):
    return x * cos + pltpu.roll(x, HEAD_DIM // 2, axis=1) * sin_signed


def _ret_bwd_body(lg_ref, q_ref, k_ref, v_ref, cos_ref, sin_ref, o_ref, r_ref, *,
                  n_heads, n_chunks, seq_ends):
    c = RET_CHUNK
    g = n_chunks - 1 - pl.program_id(0)

    @pl.when(_is_any((g + 1) * c, seq_ends))
    def _():
        r_ref[...] = jnp.zeros(r_ref.shape, F32)

    cos, sin = cos_ref[...], sin_ref[...]
    n = lax.broadcasted_iota(jnp.int32, (c, HEAD_DIM), 0).astype(F32)
    scale = HEAD_DIM ** -0.5
    for h in range(n_heads):
        sl = slice(h * HEAD_DIM, (h + 1) * HEAD_DIM)
        lg = lg_ref[1, h]
        q = _rope_half(q_ref[:, sl], cos, sin) * scale
        k = _rope_half(k_ref[:, sl], cos, sin)
        v = v_ref[:, sl]
        r = r_ref[h]
        qw = (q * jnp.exp(lg * (c - n))).astype(BF16)
        o_ref[:, sl] = jnp.dot(qw, r.astype(BF16), preferred_element_type=F32)
        kw_t = (k * jnp.exp(lg * n)).T.astype(BF16)
        chunk_decay = jnp.exp(jnp.full(r.shape, lg * c, F32))
        r_ref[h] = chunk_decay * r + jnp.dot(kw_t, v.astype(BF16), preferred_element_type=F32)


def _ret_fwd_body(lg_ref, q_ref, k_ref, v_ref, gate_ref, xb_ref, cos_ref, sin_ref, o_ref, r_ref, *,
                  n_heads, seq_starts):
    c = RET_CHUNK
    g = pl.program_id(0)

    @pl.when(_is_any(g * c, seq_starts))
    def _():
        r_ref[...] = jnp.zeros(r_ref.shape, F32)

    cos, sin = cos_ref[...], sin_ref[...]
    n = lax.broadcasted_iota(jnp.int32, (c, HEAD_DIM), 0).astype(F32)
    rel = (lax.broadcasted_iota(jnp.int32, (c, c), 0) - lax.broadcasted_iota(jnp.int32, (c, c), 1)).astype(F32)
    scale = HEAD_DIM ** -0.5
    for h in range(n_heads):
        sl = slice(h * HEAD_DIM, (h + 1) * HEAD_DIM)
        lg_f, lg_b = lg_ref[0, h], lg_ref[1, h]
        q = _rope_half(q_ref[:, sl], cos, sin) * scale
        k = _rope_half(k_ref[:, sl], cos, sin)
        v = v_ref[:, sl].astype(BF16)
        r = r_ref[h]
        decay = jnp.where(rel >= 0, jnp.exp(lg_f * jnp.maximum(rel, 0.0)), jnp.exp(lg_b * jnp.maximum(-rel, 0.0)))
        qk = lax.dot_general(q.astype(BF16), k.astype(BF16), (((1,), (1,)), ((), ())), preferred_element_type=F32)
        o = jnp.dot((qk * decay).astype(BF16), v, preferred_element_type=F32)
        qw = (q * jnp.exp(lg_f * (n + 1.0))).astype(BF16)
        o = o + jnp.dot(qw, r.astype(BF16), preferred_element_type=F32) + xb_ref[:, sl]
        mu = jnp.mean(o, axis=-1, keepdims=True)
        d = o - mu
        var = jnp.mean(d * d, axis=-1, keepdims=True)
        o = d * lax.rsqrt(var + EPS)
        gate = gate_ref[:, sl]
        o_ref[:, sl] = (gate * jax.nn.sigmoid(gate) * o).astype(o_ref.dtype)
        kw_t = (k * jnp.exp(lg_f * (c - 1.0 - n))).T.astype(BF16)
        chunk_decay = jnp.exp(jnp.full(r.shape, lg_f * c, F32))
        r_ref[h] = chunk_decay * r + jnp.dot(kw_t, v, preferred_element_type=F32)


def retention(proj, col0, log_gamma, cos, sin, n_heads, seq_starts, seq_ends):
    n, _ = proj.shape
    w = n_heads * HEAD_DIM
    cb0 = col0 // w
    c = RET_CHUNK
    n_chunks = n // c
    smem = pl.BlockSpec(memory_space=pltpu.SMEM)
    rev = lambda off: pl.BlockSpec((c, w), lambda s: (n_chunks - 1 - s, cb0 + off))
    rev_tab = pl.BlockSpec((c, HEAD_DIM), lambda s: (n_chunks - 1 - s, 0))
    cross_b = pl.pallas_call(
        functools.partial(_ret_bwd_body, n_heads=n_heads, n_chunks=n_chunks, seq_ends=seq_ends),
        out_shape=jax.ShapeDtypeStruct((n, w), F32),
        grid=(n_chunks,),
        in_specs=[smem, rev(0), rev(1), rev(2), rev_tab, rev_tab],
        out_specs=pl.BlockSpec((c, w), lambda s: (n_chunks - 1 - s, 0)),
        scratch_shapes=[pltpu.VMEM((n_heads, HEAD_DIM, HEAD_DIM), F32)],
        compiler_params=_cparams(("arbitrary",)),
        name="ret_bwd",
    )(log_gamma, proj, proj, proj, cos, sin)
    fwd = lambda off: pl.BlockSpec((c, w), lambda s: (s, cb0 + off))
    tab = pl.BlockSpec((c, HEAD_DIM), lambda s: (s, 0))
    return pl.pallas_call(
        functools.partial(_ret_fwd_body, n_heads=n_heads, seq_starts=seq_starts),
        out_shape=jax.ShapeDtypeStruct((n, w), BF16),
        grid=(n_chunks,),
        in_specs=[smem, fwd(0), fwd(1), fwd(2), fwd(3), pl.BlockSpec((c, w), lambda s: (s, 0)), tab, tab],
        out_specs=pl.BlockSpec((c, w), lambda s: (s, 0)),
        scratch_shapes=[pltpu.VMEM((n_heads, HEAD_DIM, HEAD_DIM), F32)],
        compiler_params=_cparams(("arbitrary",)),
        name="ret_fwd",
    )(log_gamma, proj, proj, proj, proj, cross_b, cos, sin)


def _rope_quarter(x, cos, sin_signed, low):
    partner = jnp.where(low, pltpu.roll(x, HEAD_DIM - 32, axis=1), pltpu.roll(x, 32, axis=1))
    return x * cos + partner * sin_signed


def _gqa_prep_body(q_ref, k_ref, v_ref, nw_ref, cos_ref, sin_ref, qo_ref, ko_ref, vo_ref, *, hq, hk):
    cos, sin = cos_ref[...], sin_ref[...]
    low = (lax.broadcasted_iota(jnp.int32, cos.shape, 1) % 64) < 32
    scale = HEAD_DIM ** -0.5

    def norm_rope(x, w):
        y = x * lax.rsqrt(jnp.mean(x * x, axis=-1, keepdims=True) + EPS) * w
        return _rope_quarter(y, cos, sin, low)

    for h in range(hq):
        sl = slice(h * HEAD_DIM, (h + 1) * HEAD_DIM)
        qo_ref[:, sl] = (norm_rope(q_ref[:, sl], nw_ref[0:1, :]) * scale).astype(qo_ref.dtype)
    for h in range(hk):
        sl = slice(h * HEAD_DIM, (h + 1) * HEAD_DIM)
        ko_ref[:, sl] = norm_rope(k_ref[:, sl], nw_ref[1:2, :]).astype(ko_ref.dtype)
    vo_ref[...] = v_ref[...].astype(vo_ref.dtype)


def gqa_prep(proj, qk_norm, cos, sin, hq, hk, tm=256):
    n = proj.shape[0]
    tm = min(tm, n)
    wq, wk = hq * HEAD_DIM, hk * HEAD_DIM
    assert wq % wk == 0
    row = lambda width, blk: pl.BlockSpec((tm, width), lambda i: (i, blk))
    return pl.pallas_call(
        functools.partial(_gqa_prep_body, hq=hq, hk=hk),
        out_shape=[jax.ShapeDtypeStruct((n, wq), BF16), jax.ShapeDtypeStruct((n, wk), BF16),
                   jax.ShapeDtypeStruct((n, wk), BF16)],
        grid=(n // tm,),
        in_specs=[row(wq, 0), row(wk, wq // wk), row(wk, wq // wk + 1),
                  pl.BlockSpec((2, HEAD_DIM), lambda i: (0, 0)), row(HEAD_DIM, 0), row(HEAD_DIM, 0)],
        out_specs=[row(wq, 0), row(wk, 0), row(wk, 0)],
        compiler_params=_cparams(("parallel",)),
        name="gqa_prep",
    )(proj, proj, proj, qk_norm, cos, sin)


def _latent_norm_body(cq_ref, ckv_ref, wq_ref, wkv_ref, qo_ref, kvo_ref):
    def rms(x, w):
        return x * lax.rsqrt(jnp.mean(x * x, axis=-1, keepdims=True) + EPS) * w
    qo_ref[...] = rms(cq_ref[...], wq_ref[...]).astype(qo_ref.dtype)
    kvo_ref[...] = rms(ckv_ref[...], wkv_ref[...]).astype(kvo_ref.dtype)


def latent_norm(proj, col_cq, cq_norm, ckv_norm, tm=512):
    n = proj.shape[0]
    tm = min(tm, n)
    assert col_cq % Q_LORA == 0 and (col_cq + Q_LORA) % KV_LORA == 0
    return pl.pallas_call(
        _latent_norm_body,
        out_shape=[jax.ShapeDtypeStruct((n, Q_LORA), BF16), jax.ShapeDtypeStruct((n, KV_LORA), BF16)],
        grid=(n // tm,),
        in_specs=[pl.BlockSpec((tm, Q_LORA), lambda i: (i, col_cq // Q_LORA)),
                  pl.BlockSpec((tm, KV_LORA), lambda i: (i, (col_cq + Q_LORA) // KV_LORA)),
                  pl.BlockSpec((1, Q_LORA), lambda i: (0, 0)), pl.BlockSpec((1, KV_LORA), lambda i: (0, 0))],
        out_specs=[pl.BlockSpec((tm, Q_LORA), lambda i: (i, 0)), pl.BlockSpec((tm, KV_LORA), lambda i: (i, 0))],
        compiler_params=_cparams(("parallel",)),
        name="latent_norm",
    )(proj, proj, cq_norm.reshape(1, Q_LORA), ckv_norm.reshape(1, KV_LORA))


def _mla_prep_body(qd_ref, kvd_ref, kpe_ref, cos_ref, sin_ref, qo_ref, ko_ref, vo_ref, *, n_heads):
    cos, sin = cos_ref[...], sin_ref[...]
    low = (lax.broadcasted_iota(jnp.int32, cos.shape, 1) % 64) < 32
    scale = (QK_NOPE + QK_ROPE) ** -0.5
    k_rope = _rope_quarter(kpe_ref[...], cos, sin, low).astype(ko_ref.dtype)
    for h in range(n_heads):
        a = slice(h * 256, h * 256 + 128)
        b = slice(h * 256 + 128, h * 256 + 256)
        qo_ref[:, a] = (qd_ref[:, a] * scale).astype(qo_ref.dtype)
        qo_ref[:, b] = (_rope_quarter(qd_ref[:, b], cos, sin, low) * scale).astype(qo_ref.dtype)
        ko_ref[:, a] = kvd_ref[:, a].astype(ko_ref.dtype)
        ko_ref[:, b] = k_rope
        vo_ref[:, h * V_D:(h + 1) * V_D] = kvd_ref[:, b].astype(vo_ref.dtype)


def mla_prep(qd, kvd, proj, col_kpe, cos, sin, n_heads, tm=256):
    n = qd.shape[0]
    tm = min(tm, n)
    wd = n_heads * 256
    row = lambda width, blk: pl.BlockSpec((tm, width), lambda i: (i, blk))
    return pl.pallas_call(
        functools.partial(_mla_prep_body, n_heads=n_heads),
        out_shape=[jax.ShapeDtypeStruct((n, wd), BF16), jax.ShapeDtypeStruct((n, wd), BF16),
                   jax.ShapeDtypeStruct((n, n_heads * V_D), BF16)],
        grid=(n // tm,),
        in_specs=[row(wd, 0), row(wd, 0), row(128, col_kpe // 128), row(128, 0), row(128, 0)],
        out_specs=[row(wd, 0), row(wd, 0), row(n_heads * V_D, 0)],
        compiler_params=_cparams(("parallel",)),
        name="mla_prep",
    )(qd, kvd, proj, cos, sin)


def _inv_freq(dim):
    return (np.float32(ROPE_THETA) ** (-np.arange(0, dim, 2, dtype=np.float32) / np.float32(dim))).astype(np.float32)


def _angles(pos, dim):
    ang = pos.astype(F32)[:, None] * jnp.asarray(_inv_freq(dim))[None, :]
    return jnp.cos(ang), jnp.sin(ang)


def _rope_tables(seq_lens):
    pos = jnp.concatenate([jnp.arange(t) for t in seq_lens])
    c, s = _angles(pos, HEAD_DIM)
    ret = (jnp.concatenate([c, c], -1), jnp.concatenate([-s, s], -1))
    cr, sr = _angles(pos // GRID_W, HEAD_DIM // 2)
    cc, sc = _angles(pos % GRID_W, HEAD_DIM // 2)
    axial = (jnp.concatenate([cr, cr, cc, cc], -1), jnp.concatenate([-sr, sr, -sc, sc], -1))
    c1, s1 = _angles(pos, QK_ROPE)
    one, zero = jnp.ones_like(c1), jnp.zeros_like(s1)
    mla = (jnp.concatenate([c1, c1, one, one], -1), jnp.concatenate([-s1, s1, zero, zero], -1))
    return ret, axial, mla


def _pad_cols(w, mult):
    pad = (-w.shape[-1]) % mult
    return jnp.pad(w, ((0, 0), (0, pad))) if pad else w


def _mix_even(h, w_in, log_1m_gamma, rel_bias, ret_tab, seqs):
    seq_starts, seq_ends = seqs
    n_heads = w_in.shape[1] // (7 * HEAD_DIM)
    w = n_heads * HEAD_DIM
    proj = matmul(h, w_in.astype(BF16), out_dtype=F32, name="mm_in_even")
    outs, lses = [], []
    for _, dil in DIL_PATTERNS:
        o, lse = window_attention(proj, rel_bias, dil, n_heads, seq_starts, seq_ends)
        outs.append(o)
        lses.append(lse)
    oa = window_combine(outs, lses)
    log_gamma = jnp.log1p(-jnp.exp(log_1m_gamma.astype(F32)))
    ob = retention(proj, 3 * w, log_gamma, ret_tab[0], ret_tab[1], n_heads, seq_starts, seq_ends)
    return jnp.concatenate([oa, ob], axis=-1)


def _mix_odd(h, w_in, qk_norm, cq_norm, ckv_norm, w_uq, w_ukv, ax_tab, mla_tab, groups, tiles):
    in_odd = w_in.shape[1]
    wkv = KV_C * HEAD_DIM
    wq = in_odd - 2 * wkv - Q_LORA - KV_LORA - QK_ROPE
    hq = wq // HEAD_DIM
    w_in_p = _pad_cols(w_in, 128)
    proj = matmul(h, w_in_p.astype(BF16), out_dtype=F32, tn=_largest_tile(w_in_p.shape[1], 1536), name="mm_in_odd")
    qc, kc, vc = gqa_prep(proj, qk_norm, ax_tab[0], ax_tab[1], hq, KV_C)
    col_cq = wq + 2 * wkv
    cqn, ckvn = latent_norm(proj, col_cq, cq_norm, ckv_norm)
    n_hd = w_uq.shape[1] // (QK_NOPE + QK_ROPE)
    w_uq_p = jnp.pad(w_uq.reshape(Q_LORA, n_hd, QK_NOPE + QK_ROPE), ((0, 0), (0, 0), (0, 256 - QK_NOPE - QK_ROPE)))
    qd = matmul(cqn, w_uq_p.reshape(Q_LORA, n_hd * 256).astype(BF16), out_dtype=F32, name="mm_uq")
    kvd = matmul(ckvn, w_ukv.astype(BF16), out_dtype=F32, name="mm_ukv")
    q_d, k_d, v_d = mla_prep(qd, kvd, proj, col_cq + Q_LORA + KV_LORA, mla_tab[0], mla_tab[1], n_hd)
    assert in_odd == col_cq + Q_LORA + KV_LORA + QK_ROPE
    ocs, ods = [], []
    for row0, batch, seq_len in groups:
        ocs.append(flash_attention(qc, kc, vc, n_kv_heads=KV_C, g_heads=hq // KV_C, dqk=HEAD_DIM, dv=HEAD_DIM,
                                   row0=row0, batch=batch, seq_len=seq_len, tq=tiles[0], tk=tiles[1], name="attn_gqa"))
        ods.append(flash_attention(q_d, k_d, v_d, n_kv_heads=n_hd, g_heads=1, dqk=256, dv=V_D,
                                   row0=row0, batch=batch, seq_len=seq_len, tq=tiles[2], tk=tiles[3], name="attn_mla"))
    return jnp.concatenate([jnp.concatenate(ocs, 0), jnp.concatenate(ods, 0)], axis=-1)


def _largest_tile(n, cap):
    best = 128
    for t in range(128, cap + 1, 128):
        if n % t == 0:
            best = t
    return best


def _conv_ffn(h, w_up, conv_w, w_down, x, gate, seqs, ff_mult):
    seq_starts, seq_ends = seqs
    dff = w_down.shape[0]
    dffp = -(-dff // ff_mult) * ff_mult
    pad = dffp - dff
    split_pad = lambda w: jnp.concatenate([jnp.pad(w[:, :dff], ((0, 0), (0, pad))),
                                           jnp.pad(w[:, dff:], ((0, 0), (0, pad)))], axis=1)
    u = matmul(h, split_pad(w_up).astype(BF16), out_dtype=F32, name="mm_ff_up")
    act = conv_gate(u, split_pad(conv_w), dffp, seq_starts, seq_ends)
    w_down_p = jnp.pad(w_down, ((0, pad), (0, 0))).astype(BF16)
    return matmul(act, w_down_p, out_dtype=F32, resid=x, gate=gate, seq_starts=seq_starts, name="mm_ff_down")


def _trunk(x_prompt, x_sample, c_prompt, c_sample, rel_bias, norm_w, w_mod, b_mod, w_in_even,
           ret_log_1m_gamma, w_out_even, w_in_odd, c_qk_norm, mla_cq_norm, mla_ckv_norm, w_uq,
           w_ukv, w_out_odd, w_ff_up, conv_ff, w_ff_down, final_norm_w, *, attn_tiles, ff_mult):
    bp, tp, d = x_prompt.shape
    bs, ts, _ = x_sample.shape
    depth = norm_w.shape[0]
    seq_lens = (tp,) * bp + (ts,) * bs
    seq_starts = tuple(int(v) for v in np.cumsum((0,) + seq_lens[:-1]))
    seq_ends = tuple(int(v) for v in np.cumsum(seq_lens))
    seqs = (seq_starts, seq_ends)
    groups = ((0, bp, tp), (bp * tp, bs, ts))
    n_seq = len(seq_lens)

    x = jnp.concatenate([x_prompt.reshape(bp * tp, d), x_sample.reshape(bs * ts, d)], axis=0)
    c = jnp.concatenate([c_prompt, c_sample], axis=0)
    c_act = jnp.pad(jax.nn.silu(c), ((0, 8 - n_seq), (0, 0))).astype(BF16)
    ret_tab, ax_tab, mla_tab = _rope_tables(seq_lens)

    for layer in range(depth):
        mod = matmul(c_act, w_mod[layer], out_dtype=F32, tm=8, tn=2048, tk=1024,
                     bias=b_mod[layer].reshape(1, -1), name="mm_mod")
        mod6 = mod[:n_seq].reshape(n_seq * 6, 1, d)
        gate_of = lambda ci: mod6.reshape(n_seq, 6, d)[:, ci].reshape(n_seq, 1, d)
        h = norm_mod(x, norm_w[layer, 0], mod6, 1, 0, seq_starts)
        i = layer // 2
        if layer % 2 == 0:
            mix = _mix_even(h, w_in_even[i], ret_log_1m_gamma[i], rel_bias, ret_tab, seqs)
            w_out = w_out_even[i]
        else:
            mix = _mix_odd(h, w_in_odd[i], c_qk_norm[i], mla_cq_norm[i], mla_ckv_norm[i], w_uq[i], w_ukv[i],
                           ax_tab, mla_tab, groups, attn_tiles)
            w_out = w_out_odd[i]
        x = matmul(mix, w_out.astype(BF16), out_dtype=F32, resid=x, gate=gate_of(2), seq_starts=seq_starts,
                   name="mm_out")
        h = norm_mod(x, norm_w[layer, 1], mod6, 4, 3, seq_starts)
        x = _conv_ffn(h, w_ff_up[layer], conv_ff[layer], w_ff_down[layer], x, gate_of(5), seqs, ff_mult)
    y = final_norm(x, final_norm_w)
    return y[:bp * tp].reshape(bp, tp, d), y[bp * tp:].reshape(bs, ts, d)


def kernel(x_prompt, x_sample, c_prompt, c_sample, rel_bias, norm_w, w_mod, b_mod, w_in_even, ret_log_1m_gamma, w_out_even, w_in_odd, c_qk_norm, mla_cq_norm, mla_ckv_norm, w_uq, w_ukv, w_out_odd, w_ff_up, conv_ff, w_ff_down, final_norm_w):
    return _trunk(x_prompt, x_sample, c_prompt, c_sample, rel_bias, norm_w, w_mod, b_mod, w_in_even,
                  ret_log_1m_gamma, w_out_even, w_in_odd, c_qk_norm, mla_cq_norm, mla_ckv_norm, w_uq,
                  w_ukv, w_out_odd, w_ff_up, conv_ff, w_ff_down, final_norm_w,
                  attn_tiles=(512, 512, 1024, 1024), ff_mult=1024)
```

```python
import functools
import math

import numpy as np
import jax
import jax.numpy as jnp
from jax import lax
from jax.experimental import pallas as pl
from jax.experimental.pallas import tpu as pltpu

F32 = jnp.float32
BF16 = jnp.bfloat16

HEAD_DIM = 128
GRID_W = 64
DIL_PATTERNS = ((128, 1), (512, 4), (2048, 16))
RET_CHUNK = 128
KV_C = 4
Q_LORA = 768
KV_LORA = 256
QK_NOPE = 128
QK_ROPE = 64
V_D = 128
MLA_QK_PAD = 256
N_BUCKETS = 32
MAX_DIST = 1024
ROPE_THETA = 10000.0
EPS = 1e-6
NEG = -1e30
LOG2E = math.log2(math.e)

WIN_HALF = 64
WIN_TQ = 128
BF16_ROWS = 16
VMEM_LIMIT = 56 * 1024 * 1024


def _cparams(sem):
    return pltpu.CompilerParams(dimension_semantics=sem, vmem_limit_bytes=VMEM_LIMIT)


def _seq_id(row, seq_starts):
    s = 0
    for st in seq_starts[1:]:
        s = s + (row >= st).astype(jnp.int32)
    return s


def _is_any(row, marks):
    r = row == marks[0]
    for m in marks[1:]:
        r = jnp.logical_or(r, row == m)
    return r


def _largest_tile(n, cap):
    best = 128
    for t in range(128, cap + 1, 128):
        if n % t == 0:
            best = t
    return best


def _mm_body(*refs, nk, mode):
    a_ref, b_ref = refs[0], refs[1]
    if mode == "plain":
        extra, (o_ref, acc_ref) = (), refs[2:]
    elif mode == "bias":
        extra, (o_ref, acc_ref) = refs[2:3], refs[3:]
    else:
        extra, (o_ref, acc_ref) = refs[2:4], refs[4:]

    def epilogue(r):
        if mode == "bias":
            r = r + extra[0][...]
        elif mode == "resid":
            r = extra[0][...] + extra[1][0] * r
        o_ref[...] = r.astype(o_ref.dtype)

    def prod():
        return jnp.dot(a_ref[...].astype(BF16), b_ref[...].astype(BF16), preferred_element_type=F32)

    if nk == 1:
        epilogue(prod())
        return
    k = pl.program_id(2)

    @pl.when(k == 0)
    def _():
        acc_ref[...] = prod()

    @pl.when(jnp.logical_and(k > 0, k < nk - 1))
    def _():
        acc_ref[...] += prod()

    @pl.when(k == nk - 1)
    def _():
        epilogue(acc_ref[...] + prod())


def matmul(a, b, *, out_dtype, tm=1024, tn=512, tk=4096, bias=None, resid=None, gate=None,
           seq_starts=None, name="mm"):
    m, kd = a.shape
    _, n = b.shape
    tm, tn, tk = min(tm, m), _largest_tile(n, tn), _largest_tile(kd, tk)
    assert m % tm == 0 and n % tn == 0 and kd % tk == 0, (a.shape, b.shape, tm, tn, tk)
    nk = kd // tk
    in_specs = [pl.BlockSpec((tm, tk), lambda i, j, k: (i, k)),
                pl.BlockSpec((tk, tn), lambda i, j, k: (k, j))]
    args = [a, b]
    if bias is not None:
        mode = "bias"
        in_specs.append(pl.BlockSpec((1, tn), lambda i, j, k: (0, j)))
        args.append(bias)
    elif resid is not None:
        mode = "resid"
        in_specs.append(pl.BlockSpec((tm, tn), lambda i, j, k: (i, j)))
        in_specs.append(pl.BlockSpec((1, 1, tn), lambda i, j, k: (_seq_id(i * tm, seq_starts), 0, j)))
        args += [resid, gate]
    else:
        mode = "plain"
    return pl.pallas_call(
        functools.partial(_mm_body, nk=nk, mode=mode),
        out_shape=jax.ShapeDtypeStruct((m, n), out_dtype),
        grid=(m // tm, n // tn, nk),
        in_specs=in_specs,
        out_specs=pl.BlockSpec((tm, tn), lambda i, j, k: (i, j)),
        scratch_shapes=[pltpu.VMEM((tm, tn) if nk > 1 else (8, 128), F32)],
        compiler_params=_cparams(("parallel", "parallel", "arbitrary")),
        name=name,
    )(*args)


def _ffup_body(a_ref, ap_ref, an_ref, bg_ref, bv_ref, cwg_ref, cwv_ref, o_ref, ext_ref, *,
               tm, seq_starts, seq_ends):
    i = pl.program_id(0)
    halo = BF16_ROWS

    @pl.when(pl.program_id(1) == 0)
    def _():
        ext_ref[0:halo] = ap_ref[...]
        ext_ref[halo:halo + tm] = a_ref[...]
        ext_ref[halo + tm:] = an_ref[...]

    first = _is_any(i * tm, seq_starts)
    last = _is_any((i + 1) * tm, seq_ends)
    ext = ext_ref[...]
    rows = tm + 2 * halo
    row = lax.broadcasted_iota(jnp.int32, (tm, o_ref.shape[1]), 0)
    kill_prev = jnp.logical_and(row == 0, first)
    kill_next = jnp.logical_and(row == tm - 1, last)

    def conv(b_ref, cw_ref):
        u = jnp.dot(ext, b_ref[...], preferred_element_type=F32)
        u_prev = jnp.where(kill_prev, 0.0, pltpu.roll(u, 1, axis=0)[halo:halo + tm])
        u_next = jnp.where(kill_next, 0.0, pltpu.roll(u, rows - 1, axis=0)[halo:halo + tm])
        cw = cw_ref[...]
        return u_prev * cw[0:1] + u[halo:halo + tm] * cw[1:2] + u_next * cw[2:3]

    g = conv(bg_ref, cwg_ref)
    val = conv(bv_ref, cwv_ref)
    o_ref[...] = (g * jax.nn.sigmoid(g) * val).astype(o_ref.dtype)


def ff_up_conv_gate(h, w_up, conv_w, dff, seq_starts, seq_ends, tm=1024, tn=512):
    n, d = h.shape
    tm, tn = min(tm, n), _largest_tile(dff, tn)
    nj = dff // tn
    hb = tm // BF16_ROWS
    nhb = n // BF16_ROWS
    prev = lambda i: jnp.maximum(i * hb - 1, 0)
    nxt = lambda i: jnp.minimum((i + 1) * hb, nhb - 1)
    return pl.pallas_call(
        functools.partial(_ffup_body, tm=tm, seq_starts=seq_starts, seq_ends=seq_ends),
        out_shape=jax.ShapeDtypeStruct((n, dff), BF16),
        grid=(n // tm, nj),
        in_specs=[pl.BlockSpec((tm, d), lambda i, j: (i, 0)),
                  pl.BlockSpec((BF16_ROWS, d), lambda i, j: (prev(i), 0)),
                  pl.BlockSpec((BF16_ROWS, d), lambda i, j: (nxt(i), 0)),
                  pl.BlockSpec((d, tn), lambda i, j: (0, j)),
                  pl.BlockSpec((d, tn), lambda i, j: (0, j + nj)),
                  pl.BlockSpec((3, tn), lambda i, j: (0, j)),
                  pl.BlockSpec((3, tn), lambda i, j: (0, j + nj))],
        out_specs=pl.BlockSpec((tm, tn), lambda i, j: (i, j)),
        scratch_shapes=[pltpu.VMEM((tm + 2 * BF16_ROWS, d), BF16)],
        compiler_params=_cparams(("parallel", "arbitrary")),
        name="ff_up_conv_gate",
    )(h, h, h, w_up, w_up, conv_w, conv_w)


def _normmod_body(x_ref, w_ref, sc_ref, sh_ref, o_ref):
    x = x_ref[...]
    y = x * lax.rsqrt(jnp.mean(x * x, axis=-1, keepdims=True) + EPS) * w_ref[...]
    o_ref[...] = (y * (1.0 + sc_ref[0]) + sh_ref[0]).astype(o_ref.dtype)


def norm_mod(x, w, mod6, scale_idx, shift_idx, seq_starts, tm=256):
    n, d = x.shape
    tm = min(tm, n)
    return pl.pallas_call(
        _normmod_body,
        out_shape=jax.ShapeDtypeStruct((n, d), BF16),
        grid=(n // tm,),
        in_specs=[pl.BlockSpec((tm, d), lambda i: (i, 0)),
                  pl.BlockSpec((1, d), lambda i: (0, 0)),
                  pl.BlockSpec((1, 1, d), lambda i: (_seq_id(i * tm, seq_starts) * 6 + scale_idx, 0, 0)),
                  pl.BlockSpec((1, 1, d), lambda i: (_seq_id(i * tm, seq_starts) * 6 + shift_idx, 0, 0))],
        out_specs=pl.BlockSpec((tm, d), lambda i: (i, 0)),
        compiler_params=_cparams(("parallel",)),
        name="norm_mod",
    )(x, w.reshape(1, d), mod6, mod6)


def _final_norm_body(x_ref, w_ref, o_ref):
    x = x_ref[...]
    o_ref[...] = x * lax.rsqrt(jnp.mean(x * x, axis=-1, keepdims=True) + EPS) * w_ref[...]


def final_norm(x, w, tm=256):
    n, d = x.shape
    tm = min(tm, n)
    return pl.pallas_call(
        _final_norm_body,
        out_shape=jax.ShapeDtypeStruct((n, d), F32),
        grid=(n // tm,),
        in_specs=[pl.BlockSpec((tm, d), lambda i: (i, 0)), pl.BlockSpec((1, d), lambda i: (0, 0))],
        out_specs=pl.BlockSpec((tm, d), lambda i: (i, 0)),
        compiler_params=_cparams(("parallel",)),
        name="final_norm",
    )(x, w.reshape(1, d))


def _flash_body(q_ref, k_ref, vt_ref, o_ref, m_ref, l_ref, acc_ref, *, g_heads, dqk, dv, nkv, tq, nsplit):
    j = pl.program_id(3)

    @pl.when(j == 0)
    def _():
        m_ref[...] = jnp.full(m_ref.shape, -jnp.inf, F32)
        l_ref[...] = jnp.zeros(l_ref.shape, F32)
        acc_ref[...] = jnp.zeros(acc_ref.shape, F32)

    k = k_ref[...]
    vt1 = jnp.concatenate([vt_ref[...], jnp.ones((BF16_ROWS, vt_ref.shape[1]), BF16)], axis=0)
    if g_heads == 1:
        qcat = q_ref[...]
    else:
        qcat = jnp.concatenate([q_ref[:, g * dqk:(g + 1) * dqk] for g in range(g_heads)], axis=0)
    cs = (g_heads * tq) // nsplit
    cols = [slice(c * cs, (c + 1) * cs) for c in range(nsplit)]
    scores = [lax.dot_general(k, qcat[sl], (((1,), (1,)), ((), ())), preferred_element_type=F32) for sl in cols]
    for s, sl in zip(scores, cols):
        m_prev = m_ref[:, sl]
        m_new = jnp.maximum(m_prev, jnp.max(s, axis=0, keepdims=True))
        alpha = jnp.exp2(m_prev - m_new)
        p = jnp.exp2(s - m_new)
        pv = jnp.dot(vt1, p.astype(BF16), preferred_element_type=F32)
        l_ref[:, sl] = alpha * l_ref[:, sl] + pv[dv:dv + 1]
        acc_ref[:, sl] = alpha * acc_ref[:, sl] + pv[:dv]
        m_ref[:, sl] = m_new

    @pl.when(j == nkv - 1)
    def _():
        o_t = acc_ref[...] / l_ref[...]
        for g in range(g_heads):
            o_ref[:, g * dv:(g + 1) * dv] = o_t[:, g * tq:(g + 1) * tq].T.astype(o_ref.dtype)


def flash_attention(q, k, vt, *, n_kv_heads, g_heads, dqk, dv, row0, batch, seq_len, tq, tk, nsplit, name):
    tq, tk = min(tq, seq_len), min(tk, seq_len)
    nq, nkv = seq_len // tq, seq_len // tk
    qb0, kb0 = row0 // tq, row0 // tk
    ncol = g_heads * tq
    nsplit = min(nsplit, ncol // 128)
    return pl.pallas_call(
        functools.partial(_flash_body, g_heads=g_heads, dqk=dqk, dv=dv, nkv=nkv, tq=tq, nsplit=nsplit),
        out_shape=jax.ShapeDtypeStruct((batch * seq_len, n_kv_heads * g_heads * dv), BF16),
        grid=(batch, n_kv_heads, nq, nkv),
        in_specs=[pl.BlockSpec((tq, g_heads * dqk), lambda b, h, i, j: (qb0 + b * nq + i, h)),
                  pl.BlockSpec((tk, dqk), lambda b, h, i, j: (kb0 + b * nkv + j, h)),
                  pl.BlockSpec((dv, tk), lambda b, h, i, j: (h, kb0 + b * nkv + j))],
        out_specs=pl.BlockSpec((tq, g_heads * dv), lambda b, h, i, j: (b * nq + i, h)),
        scratch_shapes=[pltpu.VMEM((1, ncol), F32), pltpu.VMEM((1, ncol), F32), pltpu.VMEM((dv, ncol), F32)],
        compiler_params=_cparams(("parallel", "parallel", "parallel", "arbitrary")),
        name=name,
    )(q, k, vt)


def _win_body(q_ref, kp_ref, kc_ref, kn_ref, vp_ref, vc_ref, vn_ref, bias_ref, o_ref,
              kext_ref, vext_ref, *scr, tb, tq, dils, seq_starts, seq_ends):
    i = pl.program_id(0)
    first = _is_any(i * tb, seq_starts)
    last = _is_any((i + 1) * tb, seq_ends)
    hmax = WIN_HALF * max(dils)
    for ext_ref, p_ref, c_ref, n_ref in ((kext_ref, kp_ref, kc_ref, kn_ref), (vext_ref, vp_ref, vc_ref, vn_ref)):
        ext_ref[0:hmax] = p_ref[...]
        ext_ref[hmax:hmax + tb] = c_ref[...]
        ext_ref[hmax + tb:] = n_ref[...]
    nkeys = tq + 2 * WIN_HALF
    col = lax.broadcasted_iota(jnp.int32, (tq, nkeys), 1)
    bad_first = jnp.logical_and(col < WIN_HALF, first)
    bad_last = jnp.logical_and(col >= WIN_HALF + tq, last)
    scale = HEAD_DIM ** -0.5
    o_scr, l_scr = scr[:len(dils)], scr[len(dils):]
    for p, dil in enumerate(dils):
        nsub = tb // (tq * dil)
        bias = bias_ref[p]
        for sub in range(nsub):
            for r in range(dil):
                start = sub * tq * dil + r
                kstart = hmax - WIN_HALF * dil + start
                q = (q_ref[pl.ds(start, tq, stride=dil), :] * scale).astype(BF16)
                k = kext_ref[pl.ds(kstart, nkeys, stride=dil), :].astype(BF16)
                v = vext_ref[pl.ds(kstart, nkeys, stride=dil), :].astype(BF16)
                s = lax.dot_general(q, k, (((1,), (1,)), ((), ())), preferred_element_type=F32) + bias
                if sub == 0:
                    s = jnp.where(bad_first, NEG, s)
                if sub == nsub - 1:
                    s = jnp.where(bad_last, NEG, s)
                m = jnp.max(s, axis=-1, keepdims=True)
                e = jnp.exp(s - m)
                den = jnp.sum(e, axis=-1, keepdims=True)
                o = jnp.dot(e.astype(BF16), v, preferred_element_type=F32) / den
                o_scr[p][pl.ds(start, tq, stride=dil), :] = o
                l_scr[p][pl.ds(start, tq, stride=dil), :] = jnp.broadcast_to(m + jnp.log(den), (tq, HEAD_DIM))
    lses = [l[...] for l in l_scr]
    m = functools.reduce(jnp.maximum, lses)
    ws = [jnp.exp(l - m) for l in lses]
    num = functools.reduce(lambda a, b: a + b, [w * o[...] for w, o in zip(ws, o_scr)])
    o_ref[...] = (num / functools.reduce(lambda a, b: a + b, ws)).astype(o_ref.dtype)


def _t5_bucket(rel):
    nb = N_BUCKETS // 2
    max_exact = nb // 2
    n = jnp.abs(rel)
    log_ratio = jnp.log(jnp.maximum(n, 1).astype(F32) / max_exact) / math.log(MAX_DIST / max_exact)
    large = jnp.minimum(max_exact + (log_ratio * (nb - max_exact)).astype(jnp.int32), nb - 1)
    return jnp.where(rel > 0, nb, 0) + jnp.where(n < max_exact, n, large)


def _window_bias(rel_bias, dil, tq):
    kj = jnp.arange(tq + 2 * WIN_HALF)
    rel = kj[None, :] - WIN_HALF - jnp.arange(tq)[:, None]
    bias = rel_bias[_t5_bucket(rel * dil)].astype(F32)
    bias = jnp.where((jnp.abs(rel) <= WIN_HALF)[:, :, None], bias, NEG)
    return bias.transpose(2, 0, 1)


def window_attention(proj, rel_bias, dils, n_heads, seq_starts, seq_ends):
    n = proj.shape[0]
    tq = WIN_TQ
    hmax = WIN_HALF * max(dils)
    tb = tq * max(dils)
    assert all(s % tb == 0 for s in seq_starts + seq_ends) and tb % hmax == 0
    bias = jnp.stack([_window_bias(rel_bias, dil, tq) for dil in dils])
    hb = tb // hmax
    nhb = n // hmax
    prev = lambda i: jnp.maximum(i * hb - 1, 0)
    nxt = lambda i: jnp.minimum((i + 1) * hb, nhb - 1)
    cur_spec = lambda g: pl.BlockSpec((tb, HEAD_DIM), lambda i, h: (i, g * n_heads + h))
    prev_spec = lambda g: pl.BlockSpec((hmax, HEAD_DIM), lambda i, h: (prev(i), g * n_heads + h))
    next_spec = lambda g: pl.BlockSpec((hmax, HEAD_DIM), lambda i, h: (nxt(i), g * n_heads + h))
    np_ = len(dils)
    return pl.pallas_call(
        functools.partial(_win_body, tb=tb, tq=tq, dils=dils, seq_starts=seq_starts, seq_ends=seq_ends),
        out_shape=jax.ShapeDtypeStruct((n, n_heads * HEAD_DIM), BF16),
        grid=(n // tb, n_heads),
        in_specs=[cur_spec(0), prev_spec(1), cur_spec(1), next_spec(1),
                  prev_spec(2), cur_spec(2), next_spec(2),
                  pl.BlockSpec((np_, None, tq, tq + 2 * WIN_HALF), lambda i, h: (0, h, 0, 0))],
        out_specs=pl.BlockSpec((tb, HEAD_DIM), lambda i, h: (i, h)),
        scratch_shapes=[pltpu.VMEM((tb + 2 * hmax, HEAD_DIM), F32)] * 2 + [pltpu.VMEM((tb, HEAD_DIM), F32)] * (2 * np_),
        compiler_params=_cparams(("parallel", "parallel")),
        name="win_attn",
    )(proj, proj, proj, proj, proj, proj, proj, bias)


def _rope_half(x, cos, sin_signed):
    return x * cos + pltpu.roll(x, HEAD_DIM // 2, axis=1) * sin_signed


def _ret_bwd_body(lg_ref, q_ref, k_ref, v_ref, cos_ref, sin_ref, o_ref, r_ref, *,
                  n_heads, n_chunks, seq_ends):
    c = RET_CHUNK
    g = n_chunks - 1 - pl.program_id(0)

    @pl.when(_is_any((g + 1) * c, seq_ends))
    def _():
        r_ref[...] = jnp.zeros(r_ref.shape, F32)

    cos, sin = cos_ref[...], sin_ref[...]
    n = lax.broadcasted_iota(jnp.int32, (c, HEAD_DIM), 0).astype(F32)
    scale = HEAD_DIM ** -0.5
    for h in range(n_heads):
        sl = slice(h * HEAD_DIM, (h + 1) * HEAD_DIM)
        lg = lg_ref[1, h]
        q = _rope_half(q_ref[:, sl], cos, sin) * scale
        k = _rope_half(k_ref[:, sl], cos, sin)
        v = v_ref[:, sl]
        r = r_ref[h]
        qw = (q * jnp.exp(lg * (c - n))).astype(BF16)
        o_ref[:, sl] = jnp.dot(qw, r.astype(BF16), preferred_element_type=F32)
        kw_t = (k * jnp.exp(lg * n)).T.astype(BF16)
        chunk_decay = jnp.exp(jnp.full(r.shape, lg * c, F32))
        r_ref[h] = chunk_decay * r + jnp.dot(kw_t, v.astype(BF16), preferred_element_type=F32)


def _ret_fwd_body(lg_ref, q_ref, k_ref, v_ref, gate_ref, xb_ref, cos_ref, sin_ref, o_ref, r_ref, *,
                  n_heads, seq_starts):
    c = RET_CHUNK
    g = pl.program_id(0)

    @pl.when(_is_any(g * c, seq_starts))
    def _():
        r_ref[...] = jnp.zeros(r_ref.shape, F32)

    cos, sin = cos_ref[...], sin_ref[...]
    n = lax.broadcasted_iota(jnp.int32, (c, HEAD_DIM), 0).astype(F32)
    rel = (lax.broadcasted_iota(jnp.int32, (c, c), 0) - lax.broadcasted_iota(jnp.int32, (c, c), 1)).astype(F32)
    scale = HEAD_DIM ** -0.5
    for h in range(n_heads):
        sl = slice(h * HEAD_DIM, (h + 1) * HEAD_DIM)
        lg_f, lg_b = lg_ref[0, h], lg_ref[1, h]
        q = _rope_half(q_ref[:, sl], cos, sin) * scale
        k = _rope_half(k_ref[:, sl], cos, sin)
        v = v_ref[:, sl].astype(BF16)
        r = r_ref[h]
        decay = jnp.where(rel >= 0, jnp.exp(lg_f * jnp.maximum(rel, 0.0)), jnp.exp(lg_b * jnp.maximum(-rel, 0.0)))
        qk = lax.dot_general(q.astype(BF16), k.astype(BF16), (((1,), (1,)), ((), ())), preferred_element_type=F32)
        o = jnp.dot((qk * decay).astype(BF16), v, preferred_element_type=F32)
        qw = (q * jnp.exp(lg_f * (n + 1.0))).astype(BF16)
        o = o + jnp.dot(qw, r.astype(BF16), preferred_element_type=F32) + xb_ref[:, sl]
        mu = jnp.mean(o, axis=-1, keepdims=True)
        d = o - mu
        var = jnp.mean(d * d, axis=-1, keepdims=True)
        o = d * lax.rsqrt(var + EPS)
        gate = gate_ref[:, sl]
        o_ref[:, sl] = (gate * jax.nn.sigmoid(gate) * o).astype(o_ref.dtype)
        kw_t = (k * jnp.exp(lg_f * (c - 1.0 - n))).T.astype(BF16)
        chunk_decay = jnp.exp(jnp.full(r.shape, lg_f * c, F32))
        r_ref[h] = chunk_decay * r + jnp.dot(kw_t, v, preferred_element_type=F32)


def retention(proj, col0, log_gamma, cos, sin, n_heads, seq_starts, seq_ends):
    n, _ = proj.shape
    w = n_heads * HEAD_DIM
    cb0 = col0 // w
    c = RET_CHUNK
    n_chunks = n // c
    smem = pl.BlockSpec(memory_space=pltpu.SMEM)
    rev = lambda off: pl.BlockSpec((c, w), lambda s: (n_chunks - 1 - s, cb0 + off))
    rev_tab = pl.BlockSpec((c, HEAD_DIM), lambda s: (n_chunks - 1 - s, 0))
    cross_b = pl.pallas_call(
        functools.partial(_ret_bwd_body, n_heads=n_heads, n_chunks=n_chunks, seq_ends=seq_ends),
        out_shape=jax.ShapeDtypeStruct((n, w), F32),
        grid=(n_chunks,),
        in_specs=[smem, rev(0), rev(1), rev(2), rev_tab, rev_tab],
        out_specs=pl.BlockSpec((c, w), lambda s: (n_chunks - 1 - s, 0)),
        scratch_shapes=[pltpu.VMEM((n_heads, HEAD_DIM, HEAD_DIM), F32)],
        compiler_params=_cparams(("arbitrary",)),
        name="ret_bwd",
    )(log_gamma, proj, proj, proj, cos, sin)
    fwd = lambda off: pl.BlockSpec((c, w), lambda s: (s, cb0 + off))
    tab = pl.BlockSpec((c, HEAD_DIM), lambda s: (s, 0))
    return pl.pallas_call(
        functools.partial(_ret_fwd_body, n_heads=n_heads, seq_starts=seq_starts),
        out_shape=jax.ShapeDtypeStruct((n, w), BF16),
        grid=(n_chunks,),
        in_specs=[smem, fwd(0), fwd(1), fwd(2), fwd(3), pl.BlockSpec((c, w), lambda s: (s, 0)), tab, tab],
        out_specs=pl.BlockSpec((c, w), lambda s: (s, 0)),
        scratch_shapes=[pltpu.VMEM((n_heads, HEAD_DIM, HEAD_DIM), F32)],
        compiler_params=_cparams(("arbitrary",)),
        name="ret_fwd",
    )(log_gamma, proj, proj, proj, proj, cross_b, cos, sin)


def _rope_quarter(x, cos, sin_signed, low):
    partner = jnp.where(low, pltpu.roll(x, HEAD_DIM - 32, axis=1), pltpu.roll(x, 32, axis=1))
    return x * cos + partner * sin_signed


def _gqa_prep_body(q_ref, k_ref, v_ref, nw_ref, cos_ref, sin_ref, qo_ref, ko_ref, vto_ref, *, hq, hk):
    cos, sin = cos_ref[...], sin_ref[...]
    low = (lax.broadcasted_iota(jnp.int32, cos.shape, 1) % 64) < 32
    scale = HEAD_DIM ** -0.5 * LOG2E

    def norm_rope(x, w):
        y = x * lax.rsqrt(jnp.mean(x * x, axis=-1, keepdims=True) + EPS) * w
        return _rope_quarter(y, cos, sin, low)

    for h in range(hq):
        sl = slice(h * HEAD_DIM, (h + 1) * HEAD_DIM)
        qo_ref[:, sl] = (norm_rope(q_ref[:, sl], nw_ref[0:1, :]) * scale).astype(qo_ref.dtype)
    for h in range(hk):
        sl = slice(h * HEAD_DIM, (h + 1) * HEAD_DIM)
        ko_ref[:, sl] = norm_rope(k_ref[:, sl], nw_ref[1:2, :]).astype(ko_ref.dtype)
        vto_ref[sl, :] = v_ref[:, sl].T.astype(vto_ref.dtype)


def gqa_prep(proj, qk_norm, cos, sin, hq, hk, tm=256):
    n = proj.shape[0]
    tm = min(tm, n)
    wq, wk = hq * HEAD_DIM, hk * HEAD_DIM
    assert wq % wk == 0
    row = lambda width, blk: pl.BlockSpec((tm, width), lambda i: (i, blk))
    return pl.pallas_call(
        functools.partial(_gqa_prep_body, hq=hq, hk=hk),
        out_shape=[jax.ShapeDtypeStruct((n, wq), BF16), jax.ShapeDtypeStruct((n, wk), BF16),
                   jax.ShapeDtypeStruct((wk, n), BF16)],
        grid=(n // tm,),
        in_specs=[row(wq, 0), row(wk, wq // wk), row(wk, wq // wk + 1),
                  pl.BlockSpec((2, HEAD_DIM), lambda i: (0, 0)), row(HEAD_DIM, 0), row(HEAD_DIM, 0)],
        out_specs=[row(wq, 0), row(wk, 0), pl.BlockSpec((wk, tm), lambda i: (0, i))],
        compiler_params=_cparams(("parallel",)),
        name="gqa_prep",
    )(proj, proj, proj, qk_norm, cos, sin)


def _latent_norm_body(cq_ref, ckv_ref, wq_ref, wkv_ref, qo_ref, kvo_ref):
    def rms(x, w):
        return x * lax.rsqrt(jnp.mean(x * x, axis=-1, keepdims=True) + EPS) * w
    qo_ref[...] = rms(cq_ref[...], wq_ref[...]).astype(qo_ref.dtype)
    kvo_ref[...] = rms(ckv_ref[...], wkv_ref[...]).astype(kvo_ref.dtype)


def latent_norm(proj, col_cq, cq_norm, ckv_norm, tm=512):
    n = proj.shape[0]
    tm = min(tm, n)
    assert col_cq % Q_LORA == 0 and (col_cq + Q_LORA) % KV_LORA == 0
    return pl.pallas_call(
        _latent_norm_body,
        out_shape=[jax.ShapeDtypeStruct((n, Q_LORA), BF16), jax.ShapeDtypeStruct((n, KV_LORA), BF16)],
        grid=(n // tm,),
        in_specs=[pl.BlockSpec((tm, Q_LORA), lambda i: (i, col_cq // Q_LORA)),
                  pl.BlockSpec((tm, KV_LORA), lambda i: (i, (col_cq + Q_LORA) // KV_LORA)),
                  pl.BlockSpec((1, Q_LORA), lambda i: (0, 0)), pl.BlockSpec((1, KV_LORA), lambda i: (0, 0))],
        out_specs=[pl.BlockSpec((tm, Q_LORA), lambda i: (i, 0)), pl.BlockSpec((tm, KV_LORA), lambda i: (i, 0))],
        compiler_params=_cparams(("parallel",)),
        name="latent_norm",
    )(proj, proj, cq_norm.reshape(1, Q_LORA), ckv_norm.reshape(1, KV_LORA))


def _mla_prep_body(qd_ref, kvd_ref, kpe_ref, cos_ref, sin_ref, qo_ref, ko_ref, vto_ref, *, n_heads):
    cos, sin = cos_ref[...], sin_ref[...]
    low = (lax.broadcasted_iota(jnp.int32, cos.shape, 1) % 64) < 32
    scale = (QK_NOPE + QK_ROPE) ** -0.5 * LOG2E
    k_rope = _rope_quarter(kpe_ref[...], cos, sin, low).astype(ko_ref.dtype)
    for h in range(n_heads):
        a = slice(h * MLA_QK_PAD, h * MLA_QK_PAD + QK_NOPE)
        b = slice(h * MLA_QK_PAD + QK_NOPE, (h + 1) * MLA_QK_PAD)
        qo_ref[:, a] = (qd_ref[:, a] * scale).astype(qo_ref.dtype)
        qo_ref[:, b] = (_rope_quarter(qd_ref[:, b], cos, sin, low) * scale).astype(qo_ref.dtype)
        ko_ref[:, a] = kvd_ref[:, a].astype(ko_ref.dtype)
        ko_ref[:, b] = k_rope
        vto_ref[h * V_D:(h + 1) * V_D, :] = kvd_ref[:, b].T.astype(vto_ref.dtype)


def mla_prep(qd, kvd, proj, col_kpe, cos, sin, n_heads, tm=256):
    n = qd.shape[0]
    tm = min(tm, n)
    wd = n_heads * MLA_QK_PAD
    assert col_kpe % 128 == 0
    row = lambda width, blk: pl.BlockSpec((tm, width), lambda i: (i, blk))
    return pl.pallas_call(
        functools.partial(_mla_prep_body, n_heads=n_heads),
        out_shape=[jax.ShapeDtypeStruct((n, wd), BF16), jax.ShapeDtypeStruct((n, wd), BF16),
                   jax.ShapeDtypeStruct((n_heads * V_D, n), BF16)],
        grid=(n // tm,),
        in_specs=[row(wd, 0), row(wd, 0), row(128, col_kpe // 128), row(128, 0), row(128, 0)],
        out_specs=[row(wd, 0), row(wd, 0), pl.BlockSpec((n_heads * V_D, tm), lambda i: (0, i))],
        compiler_params=_cparams(("parallel",)),
        name="mla_prep",
    )(qd, kvd, proj, cos, sin)


def _inv_freq(dim):
    return (np.float32(ROPE_THETA) ** (-np.arange(0, dim, 2, dtype=np.float32) / np.float32(dim))).astype(np.float32)


def _angles(pos, dim):
    ang = pos.astype(F32)[:, None] * jnp.asarray(_inv_freq(dim))[None, :]
    return jnp.cos(ang), jnp.sin(ang)


def _rope_tables(seq_lens):
    pos = jnp.concatenate([jnp.arange(t) for t in seq_lens])
    c, s = _angles(pos, HEAD_DIM)
    ret = (jnp.concatenate([c, c], -1), jnp.concatenate([-s, s], -1))
    cr, sr = _angles(pos // GRID_W, HEAD_DIM // 2)
    cc, sc = _angles(pos % GRID_W, HEAD_DIM // 2)
    axial = (jnp.concatenate([cr, cr, cc, cc], -1), jnp.concatenate([-sr, sr, -sc, sc], -1))
    c1, s1 = _angles(pos, QK_ROPE)
    one, zero = jnp.ones_like(c1), jnp.zeros_like(s1)
    mla = (jnp.concatenate([c1, c1, one, one], -1), jnp.concatenate([-s1, s1, zero, zero], -1))
    return ret, axial, mla


def _pad_cols(w, mult):
    pad = (-w.shape[-1]) % mult
    return jnp.pad(w, ((0, 0), (0, pad))) if pad else w


def _mix_even(h, w_in, log_1m_gamma, rel_bias, ret_tab, seqs):
    seq_starts, seq_ends = seqs
    n_heads = w_in.shape[1] // (7 * HEAD_DIM)
    w = n_heads * HEAD_DIM
    proj = matmul(h, w_in.astype(BF16), out_dtype=F32, name="mm_in_even")
    assert all(window // (2 * dil) == WIN_HALF for window, dil in DIL_PATTERNS)
    dils = tuple(dil for _, dil in DIL_PATTERNS)
    oa = window_attention(proj, rel_bias, dils, n_heads, seq_starts, seq_ends)
    log_gamma = jnp.log1p(-jnp.exp(log_1m_gamma.astype(F32)))
    ob = retention(proj, 3 * w, log_gamma, ret_tab[0], ret_tab[1], n_heads, seq_starts, seq_ends)
    return jnp.concatenate([oa, ob], axis=-1)


def _mix_odd(h, w_in, qk_norm, cq_norm, ckv_norm, w_uq, w_ukv, ax_tab, mla_tab, groups, tiles):
    in_odd = w_in.shape[1]
    wkv = KV_C * HEAD_DIM
    wq = in_odd - 2 * wkv - Q_LORA - KV_LORA - QK_ROPE
    hq = wq // HEAD_DIM
    proj = matmul(h, _pad_cols(w_in, 512).astype(BF16), out_dtype=F32, name="mm_in_odd")
    qc, kc, vtc = gqa_prep(proj, qk_norm, ax_tab[0], ax_tab[1], hq, KV_C)
    col_cq = wq + 2 * wkv
    cqn, ckvn = latent_norm(proj, col_cq, cq_norm, ckv_norm)
    n_hd = w_uq.shape[1] // (QK_NOPE + QK_ROPE)
    w_uq_p = jnp.pad(w_uq.reshape(Q_LORA, n_hd, QK_NOPE + QK_ROPE),
                     ((0, 0), (0, 0), (0, MLA_QK_PAD - QK_NOPE - QK_ROPE)))
    qd = matmul(cqn, w_uq_p.reshape(Q_LORA, n_hd * MLA_QK_PAD).astype(BF16), out_dtype=F32, tn=1024, name="mm_uq")
    kvd = matmul(ckvn, w_ukv.astype(BF16), out_dtype=F32, tn=1024, name="mm_ukv")
    q_d, k_d, vt_d = mla_prep(qd, kvd, proj, col_cq + Q_LORA + KV_LORA, mla_tab[0], mla_tab[1], n_hd)
    ocs, ods = [], []
    for row0, batch, seq_len in groups:
        ocs.append(flash_attention(qc, kc, vtc, n_kv_heads=KV_C, g_heads=hq // KV_C, dqk=HEAD_DIM, dv=HEAD_DIM,
                                   row0=row0, batch=batch, seq_len=seq_len, tq=tiles[0], tk=tiles[1],
                                   nsplit=tiles[2], name="attn_gqa"))
        ods.append(flash_attention(q_d, k_d, vt_d, n_kv_heads=n_hd, g_heads=1, dqk=MLA_QK_PAD, dv=V_D,
                                   row0=row0, batch=batch, seq_len=seq_len, tq=tiles[3], tk=tiles[4],
                                   nsplit=tiles[5], name="attn_mla"))
    return jnp.concatenate([jnp.concatenate(ocs, 0), jnp.concatenate(ods, 0)], axis=-1)


def _conv_ffn(h, w_up, conv_w, w_down, x, gate, seqs, ff_mult):
    seq_starts, seq_ends = seqs
    dff = w_down.shape[0]
    dffp = -(-dff // ff_mult) * ff_mult
    pad = dffp - dff
    split_pad = lambda w: jnp.concatenate([jnp.pad(w[:, :dff], ((0, 0), (0, pad))),
                                           jnp.pad(w[:, dff:], ((0, 0), (0, pad)))], axis=1)
    act = ff_up_conv_gate(h, split_pad(w_up).astype(BF16), split_pad(conv_w), dffp, seq_starts, seq_ends)
    w_down_p = jnp.pad(w_down, ((0, pad), (0, 0))).astype(BF16)
    return matmul(act, w_down_p, out_dtype=F32, tk=2816, resid=x, gate=gate, seq_starts=seq_starts,
                  name="mm_ff_down")


def _trunk(x_prompt, x_sample, c_prompt, c_sample, rel_bias, norm_w, w_mod, b_mod, w_in_even,
           ret_log_1m_gamma, w_out_even, w_in_odd, c_qk_norm, mla_cq_norm, mla_ckv_norm, w_uq,
           w_ukv, w_out_odd, w_ff_up, conv_ff, w_ff_down, final_norm_w, *, attn_tiles, ff_mult):
    bp, tp, d = x_prompt.shape
    bs, ts, _ = x_sample.shape
    depth = norm_w.shape[0]
    seq_lens = (tp,) * bp + (ts,) * bs
    seq_starts = tuple(int(v) for v in np.cumsum((0,) + seq_lens[:-1]))
    seq_ends = tuple(int(v) for v in np.cumsum(seq_lens))
    seqs = (seq_starts, seq_ends)
    groups = ((0, bp, tp), (bp * tp, bs, ts))
    n_seq = len(seq_lens)

    x = jnp.concatenate([x_prompt.reshape(bp * tp, d), x_sample.reshape(bs * ts, d)], axis=0)
    c = jnp.concatenate([c_prompt, c_sample], axis=0)
    c_act = jnp.pad(jax.nn.silu(c), ((0, 8 - n_seq), (0, 0))).astype(BF16)
    ret_tab, ax_tab, mla_tab = _rope_tables(seq_lens)

    for layer in range(depth):
        mod = matmul(c_act, w_mod[layer], out_dtype=F32, tm=8, tn=2048, tk=1024,
                     bias=b_mod[layer].reshape(1, -1), name="mm_mod")
        mod6 = mod[:n_seq].reshape(n_seq * 6, 1, d)
        gate_of = lambda ci: mod6.reshape(n_seq, 6, d)[:, ci].reshape(n_seq, 1, d)
        h = norm_mod(x, norm_w[layer, 0], mod6, 1, 0, seq_starts)
        i = layer // 2
        if layer % 2 == 0:
            mix = _mix_even(h, w_in_even[i], ret_log_1m_gamma[i], rel_bias, ret_tab, seqs)
            w_out = w_out_even[i]
        else:
            mix = _mix_odd(h, w_in_odd[i], c_qk_norm[i], mla_cq_norm[i], mla_ckv_norm[i], w_uq[i], w_ukv[i],
                           ax_tab, mla_tab, groups, attn_tiles)
            w_out = w_out_odd[i]
        x = matmul(mix, w_out.astype(BF16), out_dtype=F32, resid=x, gate=gate_of(2), seq_starts=seq_starts,
                   name="mm_out")
        h = norm_mod(x, norm_w[layer, 1], mod6, 4, 3, seq_starts)
        x = _conv_ffn(h, w_ff_up[layer], conv_ff[layer], w_ff_down[layer], x, gate_of(5), seqs, ff_mult)
    y = final_norm(x, final_norm_w)
    return y[:bp * tp].reshape(bp, tp, d), y[bp * tp:].reshape(bs, ts, d)


def kernel(x_prompt, x_sample, c_prompt, c_sample, rel_bias, norm_w, w_mod, b_mod, w_in_even, ret_log_1m_gamma, w_out_even, w_in_odd, c_qk_norm, mla_cq_norm, mla_ckv_norm, w_uq, w_ukv, w_out_odd, w_ff_up, conv_ff, w_ff_down, final_norm_w):
    return _trunk(x_prompt, x_sample, c_prompt, c_sample, rel_bias, norm_w, w_mod, b_mod, w_in_even,
                  ret_log_1m_gamma, w_out_even, w_in_odd, c_qk_norm, mla_cq_norm, mla_ckv_norm, w_uq,
                  w_ukv, w_out_odd, w_ff_up, conv_ff, w_ff_down, final_norm_w,
                  attn_tiles=(512, 1024, 8, 2048, 1024, 8), ff_mult=1024)
```

```python
import functools
import math

import numpy as np
import jax
import jax.numpy as jnp
from jax import lax
from jax.experimental import pallas as pl
from jax.experimental.pallas import tpu as pltpu

F32 = jnp.float32
BF16 = jnp.bfloat16

HEAD_DIM = 128
GRID_W = 64
DIL_PATTERNS = ((128, 1), (512, 4), (2048, 16))
RET_CHUNK = 128
KV_C = 4
Q_LORA = 768
KV_LORA = 256
QK_NOPE = 128
QK_ROPE = 64
V_D = 128
MLA_QK_PAD = 256
N_BUCKETS = 32
MAX_DIST = 1024
ROPE_THETA = 10000.0
EPS = 1e-6
NEG = -1e30
LOG2E = math.log2(math.e)

WIN_HALF = 64
WIN_TQ = 128
BF16_ROWS = 16
VMEM_LIMIT = 56 * 1024 * 1024


def _cparams(sem):
    return pltpu.CompilerParams(dimension_semantics=sem, vmem_limit_bytes=VMEM_LIMIT)


def _seq_id(row, seq_starts):
    s = 0
    for st in seq_starts[1:]:
        s = s + (row >= st).astype(jnp.int32)
    return s


def _is_any(row, marks):
    r = row == marks[0]
    for m in marks[1:]:
        r = jnp.logical_or(r, row == m)
    return r


def _largest_tile(n, cap):
    best = 128
    for t in range(128, cap + 1, 128):
        if n % t == 0:
            best = t
    return best


def _mm_body(*refs, nk, mode):
    a_ref, b_ref = refs[0], refs[1]
    if mode == "plain":
        extra, (o_ref, acc_ref) = (), refs[2:]
    elif mode == "bias":
        extra, (o_ref, acc_ref) = refs[2:3], refs[3:]
    else:
        extra, (o_ref, acc_ref) = refs[2:4], refs[4:]

    def epilogue(r):
        if mode == "bias":
            r = r + extra[0][...]
        elif mode == "resid":
            r = extra[0][...] + extra[1][0] * r
        o_ref[...] = r.astype(o_ref.dtype)

    def prod():
        return jnp.dot(a_ref[...].astype(BF16), b_ref[...].astype(BF16), preferred_element_type=F32)

    if nk == 1:
        epilogue(prod())
        return
    k = pl.program_id(2)

    @pl.when(k == 0)
    def _():
        acc_ref[...] = prod()

    @pl.when(jnp.logical_and(k > 0, k < nk - 1))
    def _():
        acc_ref[...] += prod()

    @pl.when(k == nk - 1)
    def _():
        epilogue(acc_ref[...] + prod())


def matmul(a, b, *, out_dtype, tm=1024, tn=512, tk=4096, bias=None, resid=None, gate=None,
           seq_starts=None, name="mm"):
    m, kd = a.shape
    _, n = b.shape
    tm, tn, tk = min(tm, m), _largest_tile(n, tn), _largest_tile(kd, tk)
    assert m % tm == 0 and n % tn == 0 and kd % tk == 0, (a.shape, b.shape, tm, tn, tk)
    nk = kd // tk
    in_specs = [pl.BlockSpec((tm, tk), lambda i, j, k: (i, k)),
                pl.BlockSpec((tk, tn), lambda i, j, k: (k, j))]
    args = [a, b]
    if bias is not None:
        mode = "bias"
        in_specs.append(pl.BlockSpec((1, tn), lambda i, j, k: (0, j)))
        args.append(bias)
    elif resid is not None:
        mode = "resid"
        in_specs.append(pl.BlockSpec((tm, tn), lambda i, j, k: (i, j)))
        in_specs.append(pl.BlockSpec((1, 1, tn), lambda i, j, k: (_seq_id(i * tm, seq_starts), 0, j)))
        args += [resid, gate]
    else:
        mode = "plain"
    return pl.pallas_call(
        functools.partial(_mm_body, nk=nk, mode=mode),
        out_shape=jax.ShapeDtypeStruct((m, n), out_dtype),
        grid=(m // tm, n // tn, nk),
        in_specs=in_specs,
        out_specs=pl.BlockSpec((tm, tn), lambda i, j, k: (i, j)),
        scratch_shapes=[pltpu.VMEM((tm, tn) if nk > 1 else (8, 128), F32)],
        compiler_params=_cparams(("parallel", "parallel", "arbitrary")),
        name=name,
    )(*args)


def _mm_cat_body(a1_ref, a2_ref, b_ref, x_ref, g_ref, o_ref, *, k1):
    r = jnp.dot(a1_ref[...], b_ref[0:k1], preferred_element_type=F32)
    r = r + jnp.dot(a2_ref[...], b_ref[k1:], preferred_element_type=F32)
    o_ref[...] = (x_ref[...] + g_ref[0] * r).astype(o_ref.dtype)


def matmul_cat_resid(a1, a2, b, resid, gate, seq_starts, tm=1024, tn=512, name="mm_out"):
    m, k1 = a1.shape
    k2 = a2.shape[1]
    n = b.shape[1]
    tm, tn = min(tm, m), _largest_tile(n, tn)
    assert b.shape[0] == k1 + k2 and m % tm == 0 and k1 % BF16_ROWS == 0
    return pl.pallas_call(
        functools.partial(_mm_cat_body, k1=k1),
        out_shape=jax.ShapeDtypeStruct((m, n), resid.dtype),
        grid=(m // tm, n // tn),
        in_specs=[pl.BlockSpec((tm, k1), lambda i, j: (i, 0)),
                  pl.BlockSpec((tm, k2), lambda i, j: (i, 0)),
                  pl.BlockSpec((k1 + k2, tn), lambda i, j: (0, j)),
                  pl.BlockSpec((tm, tn), lambda i, j: (i, j)),
                  pl.BlockSpec((1, 1, tn), lambda i, j: (_seq_id(i * tm, seq_starts), 0, j))],
        out_specs=pl.BlockSpec((tm, tn), lambda i, j: (i, j)),
        compiler_params=_cparams(("parallel", "parallel")),
        name=name,
    )(a1, a2, b, resid, gate)


def _ffup_body(a_ref, ap_ref, an_ref, bg_ref, bv_ref, cwg_ref, cwv_ref, o_ref, ext_ref, *,
               tm, seq_starts, seq_ends):
    i = pl.program_id(0)
    halo = BF16_ROWS

    @pl.when(pl.program_id(1) == 0)
    def _():
        ext_ref[0:halo] = ap_ref[...]
        ext_ref[halo:halo + tm] = a_ref[...]
        ext_ref[halo + tm:] = an_ref[...]

    first = _is_any(i * tm, seq_starts)
    last = _is_any((i + 1) * tm, seq_ends)
    ext = ext_ref[...]
    rows = tm + 2 * halo
    row = lax.broadcasted_iota(jnp.int32, (tm, o_ref.shape[1]), 0)
    kill_prev = jnp.logical_and(row == 0, first)
    kill_next = jnp.logical_and(row == tm - 1, last)

    def conv(b_ref, cw_ref):
        u = jnp.dot(ext, b_ref[...], preferred_element_type=F32)
        u_prev = jnp.where(kill_prev, 0.0, pltpu.roll(u, 1, axis=0)[halo:halo + tm])
        u_next = jnp.where(kill_next, 0.0, pltpu.roll(u, rows - 1, axis=0)[halo:halo + tm])
        cw = cw_ref[...]
        return u_prev * cw[0:1] + u[halo:halo + tm] * cw[1:2] + u_next * cw[2:3]

    g = conv(bg_ref, cwg_ref)
    val = conv(bv_ref, cwv_ref)
    o_ref[...] = (g * jax.nn.sigmoid(g) * val).astype(o_ref.dtype)


def ff_up_conv_gate(h, w_up, conv_w, dff, seq_starts, seq_ends, tm=1024, tn=512):
    n, d = h.shape
    tm, tn = min(tm, n), _largest_tile(dff, tn)
    nj = dff // tn
    hb = tm // BF16_ROWS
    nhb = n // BF16_ROWS
    prev = lambda i: jnp.maximum(i * hb - 1, 0)
    nxt = lambda i: jnp.minimum((i + 1) * hb, nhb - 1)
    return pl.pallas_call(
        functools.partial(_ffup_body, tm=tm, seq_starts=seq_starts, seq_ends=seq_ends),
        out_shape=jax.ShapeDtypeStruct((n, dff), BF16),
        grid=(n // tm, nj),
        in_specs=[pl.BlockSpec((tm, d), lambda i, j: (i, 0)),
                  pl.BlockSpec((BF16_ROWS, d), lambda i, j: (prev(i), 0)),
                  pl.BlockSpec((BF16_ROWS, d), lambda i, j: (nxt(i), 0)),
                  pl.BlockSpec((d, tn), lambda i, j: (0, j)),
                  pl.BlockSpec((d, tn), lambda i, j: (0, j + nj)),
                  pl.BlockSpec((3, tn), lambda i, j: (0, j)),
                  pl.BlockSpec((3, tn), lambda i, j: (0, j + nj))],
        out_specs=pl.BlockSpec((tm, tn), lambda i, j: (i, j)),
        scratch_shapes=[pltpu.VMEM((tm + 2 * BF16_ROWS, d), BF16)],
        compiler_params=_cparams(("parallel", "arbitrary")),
        name="ff_up_conv_gate",
    )(h, h, h, w_up, w_up, conv_w, conv_w)


def _normmod_body(x_ref, w_ref, sc_ref, sh_ref, o_ref):
    x = x_ref[...]
    y = x * lax.rsqrt(jnp.mean(x * x, axis=-1, keepdims=True) + EPS) * w_ref[...]
    o_ref[...] = (y * (1.0 + sc_ref[0]) + sh_ref[0]).astype(o_ref.dtype)


def norm_mod(x, w, mod6, scale_idx, shift_idx, seq_starts, tm=256):
    n, d = x.shape
    tm = min(tm, n)
    return pl.pallas_call(
        _normmod_body,
        out_shape=jax.ShapeDtypeStruct((n, d), BF16),
        grid=(n // tm,),
        in_specs=[pl.BlockSpec((tm, d), lambda i: (i, 0)),
                  pl.BlockSpec((1, d), lambda i: (0, 0)),
                  pl.BlockSpec((1, 1, d), lambda i: (_seq_id(i * tm, seq_starts) * 6 + scale_idx, 0, 0)),
                  pl.BlockSpec((1, 1, d), lambda i: (_seq_id(i * tm, seq_starts) * 6 + shift_idx, 0, 0))],
        out_specs=pl.BlockSpec((tm, d), lambda i: (i, 0)),
        compiler_params=_cparams(("parallel",)),
        name="norm_mod",
    )(x, w.reshape(1, d), mod6, mod6)


def _final_norm_body(x_ref, w_ref, o_ref):
    x = x_ref[...]
    o_ref[...] = x * lax.rsqrt(jnp.mean(x * x, axis=-1, keepdims=True) + EPS) * w_ref[...]


def final_norm(x, w, tm=256):
    n, d = x.shape
    tm = min(tm, n)
    return pl.pallas_call(
        _final_norm_body,
        out_shape=jax.ShapeDtypeStruct((n, d), F32),
        grid=(n // tm,),
        in_specs=[pl.BlockSpec((tm, d), lambda i: (i, 0)), pl.BlockSpec((1, d), lambda i: (0, 0))],
        out_specs=pl.BlockSpec((tm, d), lambda i: (i, 0)),
        compiler_params=_cparams(("parallel",)),
        name="final_norm",
    )(x, w.reshape(1, d))


def _flash_body(q_ref, k_ref, vt_ref, o_ref, m_ref, l_ref, acc_ref, *, g_heads, dqk, dv, nkv, tq, nsplit):
    j = pl.program_id(3)

    @pl.when(j == 0)
    def _():
        m_ref[...] = jnp.full(m_ref.shape, -jnp.inf, F32)
        l_ref[...] = jnp.zeros(l_ref.shape, F32)
        acc_ref[...] = jnp.zeros(acc_ref.shape, F32)

    k = k_ref[...]
    vt1 = jnp.concatenate([vt_ref[...], jnp.ones((BF16_ROWS, vt_ref.shape[1]), BF16)], axis=0)
    if g_heads == 1:
        qcat = q_ref[...]
    else:
        qcat = jnp.concatenate([q_ref[:, g * dqk:(g + 1) * dqk] for g in range(g_heads)], axis=0)
    cs = (g_heads * tq) // nsplit
    cols = [slice(c * cs, (c + 1) * cs) for c in range(nsplit)]
    scores = [lax.dot_general(k, qcat[sl], (((1,), (1,)), ((), ())), preferred_element_type=F32) for sl in cols]
    for s, sl in zip(scores, cols):
        m_prev = m_ref[:, sl]
        m_new = jnp.maximum(m_prev, jnp.max(s, axis=0, keepdims=True))
        alpha = jnp.exp2(m_prev - m_new)
        p = jnp.exp2(s - m_new)
        pv = jnp.dot(vt1, p.astype(BF16), preferred_element_type=F32)
        l_ref[:, sl] = alpha * l_ref[:, sl] + pv[dv:dv + 1]
        acc_ref[:, sl] = alpha * acc_ref[:, sl] + pv[:dv]
        m_ref[:, sl] = m_new

    @pl.when(j == nkv - 1)
    def _():
        o_t = acc_ref[...] / l_ref[...]
        for g in range(g_heads):
            o_ref[:, g * dv:(g + 1) * dv] = o_t[:, g * tq:(g + 1) * tq].T.astype(o_ref.dtype)


def flash_attention(q, k, vt, *, n_kv_heads, g_heads, dqk, dv, row0, batch, seq_len, tq, tk, nsplit, name):
    tq, tk = min(tq, seq_len), min(tk, seq_len)
    nq, nkv = seq_len // tq, seq_len // tk
    qb0, kb0 = row0 // tq, row0 // tk
    ncol = g_heads * tq
    nsplit = min(nsplit, ncol // 128)
    return pl.pallas_call(
        functools.partial(_flash_body, g_heads=g_heads, dqk=dqk, dv=dv, nkv=nkv, tq=tq, nsplit=nsplit),
        out_shape=jax.ShapeDtypeStruct((batch * seq_len, n_kv_heads * g_heads * dv), BF16),
        grid=(batch, n_kv_heads, nq, nkv),
        in_specs=[pl.BlockSpec((tq, g_heads * dqk), lambda b, h, i, j: (qb0 + b * nq + i, h)),
                  pl.BlockSpec((tk, dqk), lambda b, h, i, j: (kb0 + b * nkv + j, h)),
                  pl.BlockSpec((dv, tk), lambda b, h, i, j: (h, kb0 + b * nkv + j))],
        out_specs=pl.BlockSpec((tq, g_heads * dv), lambda b, h, i, j: (b * nq + i, h)),
        scratch_shapes=[pltpu.VMEM((1, ncol), F32), pltpu.VMEM((1, ncol), F32), pltpu.VMEM((dv, ncol), F32)],
        compiler_params=_cparams(("parallel", "parallel", "parallel", "arbitrary")),
        name=name,
    )(q, k, vt)


def _win_body(q_ref, kp_ref, kc_ref, kn_ref, vp_ref, vc_ref, vn_ref, bias_ref, o_ref,
              kext_ref, vext_ref, *scr, tb, tq, dils, seq_starts, seq_ends):
    i = pl.program_id(0)
    first = _is_any(i * tb, seq_starts)
    last = _is_any((i + 1) * tb, seq_ends)
    hmax = WIN_HALF * max(dils)
    for ext_ref, p_ref, c_ref, n_ref in ((kext_ref, kp_ref, kc_ref, kn_ref), (vext_ref, vp_ref, vc_ref, vn_ref)):
        ext_ref[0:hmax] = p_ref[...]
        ext_ref[hmax:hmax + tb] = c_ref[...]
        ext_ref[hmax + tb:] = n_ref[...]
    nkeys = tq + 2 * WIN_HALF
    col = lax.broadcasted_iota(jnp.int32, (tq, nkeys), 1)
    bad_first = jnp.logical_and(col < WIN_HALF, first)
    bad_last = jnp.logical_and(col >= WIN_HALF + tq, last)
    scale = HEAD_DIM ** -0.5
    o_scr, l_scr = scr[:len(dils)], scr[len(dils):]
    for p, dil in enumerate(dils):
        nsub = tb // (tq * dil)
        bias = bias_ref[p]
        for sub in range(nsub):
            for r in range(dil):
                start = sub * tq * dil + r
                kstart = hmax - WIN_HALF * dil + start
                q = (q_ref[pl.ds(start, tq, stride=dil), :] * scale).astype(BF16)
                k = kext_ref[pl.ds(kstart, nkeys, stride=dil), :].astype(BF16)
                v = vext_ref[pl.ds(kstart, nkeys, stride=dil), :].astype(BF16)
                s = lax.dot_general(q, k, (((1,), (1,)), ((), ())), preferred_element_type=F32) + bias
                if sub == 0:
                    s = jnp.where(bad_first, NEG, s)
                if sub == nsub - 1:
                    s = jnp.where(bad_last, NEG, s)
                m = jnp.max(s, axis=-1, keepdims=True)
                e = jnp.exp(s - m)
                den = jnp.sum(e, axis=-1, keepdims=True)
                o = jnp.dot(e.astype(BF16), v, preferred_element_type=F32) / den
                o_scr[p][pl.ds(start, tq, stride=dil), :] = o
                l_scr[p][pl.ds(start, tq, stride=dil), :] = jnp.broadcast_to(m + jnp.log(den), (tq, HEAD_DIM))
    lses = [l[...] for l in l_scr]
    m = functools.reduce(jnp.maximum, lses)
    ws = [jnp.exp(l - m) for l in lses]
    num = functools.reduce(lambda a, b: a + b, [w * o[...] for w, o in zip(ws, o_scr)])
    o_ref[...] = (num / functools.reduce(lambda a, b: a + b, ws)).astype(o_ref.dtype)


def _t5_bucket(rel):
    nb = N_BUCKETS // 2
    max_exact = nb // 2
    n = jnp.abs(rel)
    log_ratio = jnp.log(jnp.maximum(n, 1).astype(F32) / max_exact) / math.log(MAX_DIST / max_exact)
    large = jnp.minimum(max_exact + (log_ratio * (nb - max_exact)).astype(jnp.int32), nb - 1)
    return jnp.where(rel > 0, nb, 0) + jnp.where(n < max_exact, n, large)


def _window_bias(rel_bias, dil, tq):
    kj = jnp.arange(tq + 2 * WIN_HALF)
    rel = kj[None, :] - WIN_HALF - jnp.arange(tq)[:, None]
    bias = rel_bias[_t5_bucket(rel * dil)].astype(F32)
    bias = jnp.where((jnp.abs(rel) <= WIN_HALF)[:, :, None], bias, NEG)
    return bias.transpose(2, 0, 1)


def window_attention(proj, rel_bias, dils, n_heads, seq_starts, seq_ends):
    n = proj.shape[0]
    tq = WIN_TQ
    hmax = WIN_HALF * max(dils)
    tb = tq * max(dils)
    assert all(s % tb == 0 for s in seq_starts + seq_ends) and tb % hmax == 0
    bias = jnp.stack([_window_bias(rel_bias, dil, tq) for dil in dils])
    hb = tb // hmax
    nhb = n // hmax
    prev = lambda i: jnp.maximum(i * hb - 1, 0)
    nxt = lambda i: jnp.minimum((i + 1) * hb, nhb - 1)
    cur_spec = lambda g: pl.BlockSpec((tb, HEAD_DIM), lambda i, h: (i, g * n_heads + h))
    prev_spec = lambda g: pl.BlockSpec((hmax, HEAD_DIM), lambda i, h: (prev(i), g * n_heads + h))
    next_spec = lambda g: pl.BlockSpec((hmax, HEAD_DIM), lambda i, h: (nxt(i), g * n_heads + h))
    np_ = len(dils)
    return pl.pallas_call(
        functools.partial(_win_body, tb=tb, tq=tq, dils=dils, seq_starts=seq_starts, seq_ends=seq_ends),
        out_shape=jax.ShapeDtypeStruct((n, n_heads * HEAD_DIM), BF16),
        grid=(n // tb, n_heads),
        in_specs=[cur_spec(0), prev_spec(1), cur_spec(1), next_spec(1),
                  prev_spec(2), cur_spec(2), next_spec(2),
                  pl.BlockSpec((np_, None, tq, tq + 2 * WIN_HALF), lambda i, h: (0, h, 0, 0))],
        out_specs=pl.BlockSpec((tb, HEAD_DIM), lambda i, h: (i, h)),
        scratch_shapes=[pltpu.VMEM((tb + 2 * hmax, HEAD_DIM), F32)] * 2 + [pltpu.VMEM((tb, HEAD_DIM), F32)] * (2 * np_),
        compiler_params=_cparams(("parallel", "parallel")),
        name="win_attn",
    )(proj, proj, proj, proj, proj, proj, proj, bias)


def _rope_half(x, cos, sin_signed):
    return x * cos + pltpu.roll(x, HEAD_DIM // 2, axis=1) * sin_signed


def _ret_bwd_body(lg_ref, q_ref, k_ref, v_ref, cos_ref, sin_ref, o_ref, r_ref, *,
                  n_heads, n_chunks, seq_ends):
    c = RET_CHUNK
    g = n_chunks - 1 - pl.program_id(0)

    @pl.when(_is_any((g + 1) * c, seq_ends))
    def _():
        r_ref[...] = jnp.zeros(r_ref.shape, F32)

    cos, sin = cos_ref[...], sin_ref[...]
    n = lax.broadcasted_iota(jnp.int32, (c, HEAD_DIM), 0).astype(F32)
    scale = HEAD_DIM ** -0.5
    for h in range(n_heads):
        sl = slice(h * HEAD_DIM, (h + 1) * HEAD_DIM)
        lg = lg_ref[1, h]
        q = _rope_half(q_ref[:, sl], cos, sin) * scale
        k = _rope_half(k_ref[:, sl], cos, sin)
        v = v_ref[:, sl]
        r = r_ref[h]
        qw = (q * jnp.exp(lg * (c - n))).astype(BF16)
        o_ref[:, sl] = jnp.dot(qw, r.astype(BF16), preferred_element_type=F32)
        kw_t = (k * jnp.exp(lg * n)).T.astype(BF16)
        chunk_decay = jnp.exp(jnp.full(r.shape, lg * c, F32))
        r_ref[h] = chunk_decay * r + jnp.dot(kw_t, v.astype(BF16), preferred_element_type=F32)


def _ret_fwd_body(lg_ref, q_ref, k_ref, v_ref, gate_ref, xb_ref, cos_ref, sin_ref, o_ref, r_ref, *,
                  n_heads, seq_starts):
    c = RET_CHUNK
    g = pl.program_id(0)

    @pl.when(_is_any(g * c, seq_starts))
    def _():
        r_ref[...] = jnp.zeros(r_ref.shape, F32)

    cos, sin = cos_ref[...], sin_ref[...]
    n = lax.broadcasted_iota(jnp.int32, (c, HEAD_DIM), 0).astype(F32)
    rel = (lax.broadcasted_iota(jnp.int32, (c, c), 0) - lax.broadcasted_iota(jnp.int32, (c, c), 1)).astype(F32)
    scale = HEAD_DIM ** -0.5
    for h in range(n_heads):
        sl = slice(h * HEAD_DIM, (h + 1) * HEAD_DIM)
        lg_f, lg_b = lg_ref[0, h], lg_ref[1, h]
        q = _rope_half(q_ref[:, sl], cos, sin) * scale
        k = _rope_half(k_ref[:, sl], cos, sin)
        v = v_ref[:, sl].astype(BF16)
        r = r_ref[h]
        decay = jnp.where(rel >= 0, jnp.exp(lg_f * jnp.maximum(rel, 0.0)), jnp.exp(lg_b * jnp.maximum(-rel, 0.0)))
        qk = lax.dot_general(q.astype(BF16), k.astype(BF16), (((1,), (1,)), ((), ())), preferred_element_type=F32)
        o = jnp.dot((qk * decay).astype(BF16), v, preferred_element_type=F32)
        qw = (q * jnp.exp(lg_f * (n + 1.0))).astype(BF16)
        o = o + jnp.dot(qw, r.astype(BF16), preferred_element_type=F32) + xb_ref[:, sl]
        mu = jnp.mean(o, axis=-1, keepdims=True)
        d = o - mu
        var = jnp.mean(d * d, axis=-1, keepdims=True)
        o = d * lax.rsqrt(var + EPS)
        gate = gate_ref[:, sl]
        o_ref[:, sl] = (gate * jax.nn.sigmoid(gate) * o).astype(o_ref.dtype)
        kw_t = (k * jnp.exp(lg_f * (c - 1.0 - n))).T.astype(BF16)
        chunk_decay = jnp.exp(jnp.full(r.shape, lg_f * c, F32))
        r_ref[h] = chunk_decay * r + jnp.dot(kw_t, v, preferred_element_type=F32)


def retention(proj, col0, log_gamma, cos, sin, n_heads, seq_starts, seq_ends):
    n, _ = proj.shape
    w = n_heads * HEAD_DIM
    cb0 = col0 // w
    c = RET_CHUNK
    n_chunks = n // c
    smem = pl.BlockSpec(memory_space=pltpu.SMEM)
    rev = lambda off: pl.BlockSpec((c, w), lambda s: (n_chunks - 1 - s, cb0 + off))
    rev_tab = pl.BlockSpec((c, HEAD_DIM), lambda s: (n_chunks - 1 - s, 0))
    cross_b = pl.pallas_call(
        functools.partial(_ret_bwd_body, n_heads=n_heads, n_chunks=n_chunks, seq_ends=seq_ends),
        out_shape=jax.ShapeDtypeStruct((n, w), F32),
        grid=(n_chunks,),
        in_specs=[smem, rev(0), rev(1), rev(2), rev_tab, rev_tab],
        out_specs=pl.BlockSpec((c, w), lambda s: (n_chunks - 1 - s, 0)),
        scratch_shapes=[pltpu.VMEM((n_heads, HEAD_DIM, HEAD_DIM), F32)],
        compiler_params=_cparams(("arbitrary",)),
        name="ret_bwd",
    )(log_gamma, proj, proj, proj, cos, sin)
    fwd = lambda off: pl.BlockSpec((c, w), lambda s: (s, cb0 + off))
    tab = pl.BlockSpec((c, HEAD_DIM), lambda s: (s, 0))
    return pl.pallas_call(
        functools.partial(_ret_fwd_body, n_heads=n_heads, seq_starts=seq_starts),
        out_shape=jax.ShapeDtypeStruct((n, w), BF16),
        grid=(n_chunks,),
        in_specs=[smem, fwd(0), fwd(1), fwd(2), fwd(3), pl.BlockSpec((c, w), lambda s: (s, 0)), tab, tab],
        out_specs=pl.BlockSpec((c, w), lambda s: (s, 0)),
        scratch_shapes=[pltpu.VMEM((n_heads, HEAD_DIM, HEAD_DIM), F32)],
        compiler_params=_cparams(("arbitrary",)),
        name="ret_fwd",
    )(log_gamma, proj, proj, proj, proj, cross_b, cos, sin)


def _rope_quarter(x, cos, sin_signed, low):
    partner = jnp.where(low, pltpu.roll(x, HEAD_DIM - 32, axis=1), pltpu.roll(x, 32, axis=1))
    return x * cos + partner * sin_signed


def _gqa_prep_body(q_ref, k_ref, v_ref, nw_ref, cos_ref, sin_ref, qo_ref, ko_ref, vto_ref, *, hq, hk):
    cos, sin = cos_ref[...], sin_ref[...]
    low = (lax.broadcasted_iota(jnp.int32, cos.shape, 1) % 64) < 32
    scale = HEAD_DIM ** -0.5 * LOG2E

    def norm_rope(x, w):
        y = x * lax.rsqrt(jnp.mean(x * x, axis=-1, keepdims=True) + EPS) * w
        return _rope_quarter(y, cos, sin, low)

    for h in range(hq):
        sl = slice(h * HEAD_DIM, (h + 1) * HEAD_DIM)
        qo_ref[:, sl] = (norm_rope(q_ref[:, sl], nw_ref[0:1, :]) * scale).astype(qo_ref.dtype)
    for h in range(hk):
        sl = slice(h * HEAD_DIM, (h + 1) * HEAD_DIM)
        ko_ref[:, sl] = norm_rope(k_ref[:, sl], nw_ref[1:2, :]).astype(ko_ref.dtype)
        vto_ref[sl, :] = v_ref[:, sl].T.astype(vto_ref.dtype)


def gqa_prep(proj, qk_norm, cos, sin, hq, hk, tm=256):
    n = proj.shape[0]
    tm = min(tm, n)
    wq, wk = hq * HEAD_DIM, hk * HEAD_DIM
    assert wq % wk == 0
    row = lambda width, blk: pl.BlockSpec((tm, width), lambda i: (i, blk))
    return pl.pallas_call(
        functools.partial(_gqa_prep_body, hq=hq, hk=hk),
        out_shape=[jax.ShapeDtypeStruct((n, wq), BF16), jax.ShapeDtypeStruct((n, wk), BF16),
                   jax.ShapeDtypeStruct((wk, n), BF16)],
        grid=(n // tm,),
        in_specs=[row(wq, 0), row(wk, wq // wk), row(wk, wq // wk + 1),
                  pl.BlockSpec((2, HEAD_DIM), lambda i: (0, 0)), row(HEAD_DIM, 0), row(HEAD_DIM, 0)],
        out_specs=[row(wq, 0), row(wk, 0), pl.BlockSpec((wk, tm), lambda i: (0, i))],
        compiler_params=_cparams(("parallel",)),
        name="gqa_prep",
    )(proj, proj, proj, qk_norm, cos, sin)


def _latent_norm_body(cq_ref, ckv_ref, wq_ref, wkv_ref, qo_ref, kvo_ref):
    def rms(x, w):
        return x * lax.rsqrt(jnp.mean(x * x, axis=-1, keepdims=True) + EPS) * w
    qo_ref[...] = rms(cq_ref[...], wq_ref[...]).astype(qo_ref.dtype)
    kvo_ref[...] = rms(ckv_ref[...], wkv_ref[...]).astype(kvo_ref.dtype)


def latent_norm(proj, col_cq, cq_norm, ckv_norm, tm=512):
    n = proj.shape[0]
    tm = min(tm, n)
    assert col_cq % Q_LORA == 0 and (col_cq + Q_LORA) % KV_LORA == 0
    return pl.pallas_call(
        _latent_norm_body,
        out_shape=[jax.ShapeDtypeStruct((n, Q_LORA), BF16), jax.ShapeDtypeStruct((n, KV_LORA), BF16)],
        grid=(n // tm,),
        in_specs=[pl.BlockSpec((tm, Q_LORA), lambda i: (i, col_cq // Q_LORA)),
                  pl.BlockSpec((tm, KV_LORA), lambda i: (i, (col_cq + Q_LORA) // KV_LORA)),
                  pl.BlockSpec((1, Q_LORA), lambda i: (0, 0)), pl.BlockSpec((1, KV_LORA), lambda i: (0, 0))],
        out_specs=[pl.BlockSpec((tm, Q_LORA), lambda i: (i, 0)), pl.BlockSpec((tm, KV_LORA), lambda i: (i, 0))],
        compiler_params=_cparams(("parallel",)),
        name="latent_norm",
    )(proj, proj, cq_norm.reshape(1, Q_LORA), ckv_norm.reshape(1, KV_LORA))


def _mla_prep_body(qd_ref, kvd_ref, kpe_ref, cos_ref, sin_ref, qo_ref, ko_ref, vto_ref, *, n_heads):
    cos, sin = cos_ref[...], sin_ref[...]
    low = (lax.broadcasted_iota(jnp.int32, cos.shape, 1) % 64) < 32
    scale = (QK_NOPE + QK_ROPE) ** -0.5 * LOG2E
    k_rope = _rope_quarter(kpe_ref[...], cos, sin, low).astype(ko_ref.dtype)
    for h in range(n_heads):
        a = slice(h * MLA_QK_PAD, h * MLA_QK_PAD + QK_NOPE)
        b = slice(h * MLA_QK_PAD + QK_NOPE, (h + 1) * MLA_QK_PAD)
        qo_ref[:, a] = (qd_ref[:, a] * scale).astype(qo_ref.dtype)
        qo_ref[:, b] = (_rope_quarter(qd_ref[:, b], cos, sin, low) * scale).astype(qo_ref.dtype)
        ko_ref[:, a] = kvd_ref[:, a].astype(ko_ref.dtype)
        ko_ref[:, b] = k_rope
        vto_ref[h * V_D:(h + 1) * V_D, :] = kvd_ref[:, b].T.astype(vto_ref.dtype)


def mla_prep(qd, kvd, proj, col_kpe, cos, sin, n_heads, tm=256):
    n = qd.shape[0]
    tm = min(tm, n)
    wd = n_heads * MLA_QK_PAD
    assert col_kpe % 128 == 0
    row = lambda width, blk: pl.BlockSpec((tm, width), lambda i: (i, blk))
    return pl.pallas_call(
        functools.partial(_mla_prep_body, n_heads=n_heads),
        out_shape=[jax.ShapeDtypeStruct((n, wd), BF16), jax.ShapeDtypeStruct((n, wd), BF16),
                   jax.ShapeDtypeStruct((n_heads * V_D, n), BF16)],
        grid=(n // tm,),
        in_specs=[row(wd, 0), row(wd, 0), row(128, col_kpe // 128), row(128, 0), row(128, 0)],
        out_specs=[row(wd, 0), row(wd, 0), pl.BlockSpec((n_heads * V_D, tm), lambda i: (0, i))],
        compiler_params=_cparams(("parallel",)),
        name="mla_prep",
    )(qd, kvd, proj, cos, sin)


def _inv_freq(dim):
    return (np.float32(ROPE_THETA) ** (-np.arange(0, dim, 2, dtype=np.float32) / np.float32(dim))).astype(np.float32)


def _angles(pos, dim):
    ang = pos.astype(F32)[:, None] * jnp.asarray(_inv_freq(dim))[None, :]
    return jnp.cos(ang), jnp.sin(ang)


def _rope_tables(seq_lens):
    pos = jnp.concatenate([jnp.arange(t) for t in seq_lens])
    c, s = _angles(pos, HEAD_DIM)
    ret = (jnp.concatenate([c, c], -1), jnp.concatenate([-s, s], -1))
    cr, sr = _angles(pos // GRID_W, HEAD_DIM // 2)
    cc, sc = _angles(pos % GRID_W, HEAD_DIM // 2)
    axial = (jnp.concatenate([cr, cr, cc, cc], -1), jnp.concatenate([-sr, sr, -sc, sc], -1))
    c1, s1 = _angles(pos, QK_ROPE)
    one, zero = jnp.ones_like(c1), jnp.zeros_like(s1)
    mla = (jnp.concatenate([c1, c1, one, one], -1), jnp.concatenate([-s1, s1, zero, zero], -1))
    return ret, axial, mla


def _pad_cols(w, mult):
    pad = (-w.shape[-1]) % mult
    return jnp.pad(w, ((0, 0), (0, pad))) if pad else w


def _mix_even(h, w_in, log_1m_gamma, rel_bias, ret_tab, seqs):
    seq_starts, seq_ends = seqs
    n_heads = w_in.shape[1] // (7 * HEAD_DIM)
    w = n_heads * HEAD_DIM
    proj = matmul(h, w_in.astype(BF16), out_dtype=F32, name="mm_in_even")
    assert all(window // (2 * dil) == WIN_HALF for window, dil in DIL_PATTERNS)
    dils = tuple(dil for _, dil in DIL_PATTERNS)
    oa = window_attention(proj, rel_bias, dils, n_heads, seq_starts, seq_ends)
    log_gamma = jnp.log1p(-jnp.exp(log_1m_gamma.astype(F32)))
    ob = retention(proj, 3 * w, log_gamma, ret_tab[0], ret_tab[1], n_heads, seq_starts, seq_ends)
    return oa, ob


def _mix_odd(h, w_in, qk_norm, cq_norm, ckv_norm, w_uq, w_ukv, ax_tab, mla_tab, groups, tiles):
    in_odd = w_in.shape[1]
    wkv = KV_C * HEAD_DIM
    wq = in_odd - 2 * wkv - Q_LORA - KV_LORA - QK_ROPE
    hq = wq // HEAD_DIM
    proj = matmul(h, _pad_cols(w_in, 512).astype(BF16), out_dtype=F32, name="mm_in_odd")
    qc, kc, vtc = gqa_prep(proj, qk_norm, ax_tab[0], ax_tab[1], hq, KV_C)
    col_cq = wq + 2 * wkv
    cqn, ckvn = latent_norm(proj, col_cq, cq_norm, ckv_norm)
    n_hd = w_uq.shape[1] // (QK_NOPE + QK_ROPE)
    w_uq_p = jnp.pad(w_uq.reshape(Q_LORA, n_hd, QK_NOPE + QK_ROPE),
                     ((0, 0), (0, 0), (0, MLA_QK_PAD - QK_NOPE - QK_ROPE)))
    qd = matmul(cqn, w_uq_p.reshape(Q_LORA, n_hd * MLA_QK_PAD).astype(BF16), out_dtype=F32, tn=1024, name="mm_uq")
    kvd = matmul(ckvn, w_ukv.astype(BF16), out_dtype=F32, tn=1024, name="mm_ukv")
    q_d, k_d, vt_d = mla_prep(qd, kvd, proj, col_cq + Q_LORA + KV_LORA, mla_tab[0], mla_tab[1], n_hd)
    ocs, ods = [], []
    for row0, batch, seq_len in groups:
        ocs.append(flash_attention(qc, kc, vtc, n_kv_heads=KV_C, g_heads=hq // KV_C, dqk=HEAD_DIM, dv=HEAD_DIM,
                                   row0=row0, batch=batch, seq_len=seq_len, tq=tiles[0], tk=tiles[1],
                                   nsplit=tiles[2], name="attn_gqa"))
        ods.append(flash_attention(q_d, k_d, vt_d, n_kv_heads=n_hd, g_heads=1, dqk=MLA_QK_PAD, dv=V_D,
                                   row0=row0, batch=batch, seq_len=seq_len, tq=tiles[3], tk=tiles[4],
                                   nsplit=tiles[5], name="attn_mla"))
    return jnp.concatenate(ocs, 0), jnp.concatenate(ods, 0)


def _conv_ffn(h, w_up, conv_w, w_down, x, gate, seqs, ff_mult):
    seq_starts, seq_ends = seqs
    dff = w_down.shape[0]
    dffp = -(-dff // ff_mult) * ff_mult
    pad = dffp - dff
    split_pad = lambda w: jnp.concatenate([jnp.pad(w[:, :dff], ((0, 0), (0, pad))),
                                           jnp.pad(w[:, dff:], ((0, 0), (0, pad)))], axis=1)
    act = ff_up_conv_gate(h, split_pad(w_up).astype(BF16), split_pad(conv_w), dffp, seq_starts, seq_ends)
    w_down_p = jnp.pad(w_down, ((0, pad), (0, 0))).astype(BF16)
    return matmul(act, w_down_p, out_dtype=F32, tn=1024, tk=2816, resid=x, gate=gate, seq_starts=seq_starts,
                  name="mm_ff_down")


def _trunk(x_prompt, x_sample, c_prompt, c_sample, rel_bias, norm_w, w_mod, b_mod, w_in_even,
           ret_log_1m_gamma, w_out_even, w_in_odd, c_qk_norm, mla_cq_norm, mla_ckv_norm, w_uq,
           w_ukv, w_out_odd, w_ff_up, conv_ff, w_ff_down, final_norm_w, *, attn_tiles, ff_mult):
    bp, tp, d = x_prompt.shape
    bs, ts, _ = x_sample.shape
    depth = norm_w.shape[0]
    seq_lens = (tp,) * bp + (ts,) * bs
    seq_starts = tuple(int(v) for v in np.cumsum((0,) + seq_lens[:-1]))
    seq_ends = tuple(int(v) for v in np.cumsum(seq_lens))
    seqs = (seq_starts, seq_ends)
    groups = ((0, bp, tp), (bp * tp, bs, ts))
    n_seq = len(seq_lens)

    x = jnp.concatenate([x_prompt.reshape(bp * tp, d), x_sample.reshape(bs * ts, d)], axis=0)
    c = jnp.concatenate([c_prompt, c_sample], axis=0)
    c_act = jnp.pad(jax.nn.silu(c), ((0, 8 - n_seq), (0, 0))).astype(BF16)
    ret_tab, ax_tab, mla_tab = _rope_tables(seq_lens)

    for layer in range(depth):
        mod = matmul(c_act, w_mod[layer], out_dtype=F32, tm=8, tn=2048, tk=1024,
                     bias=b_mod[layer].reshape(1, -1), name="mm_mod")
        mod6 = mod[:n_seq].reshape(n_seq * 6, 1, d)
        gate_of = lambda ci: mod6.reshape(n_seq, 6, d)[:, ci].reshape(n_seq, 1, d)
        h = norm_mod(x, norm_w[layer, 0], mod6, 1, 0, seq_starts)
        i = layer // 2
        if layer % 2 == 0:
            mix = _mix_even(h, w_in_even[i], ret_log_1m_gamma[i], rel_bias, ret_tab, seqs)
            w_out = w_out_even[i]
        else:
            mix = _mix_odd(h, w_in_odd[i], c_qk_norm[i], mla_cq_norm[i], mla_ckv_norm[i], w_uq[i], w_ukv[i],
                           ax_tab, mla_tab, groups, attn_tiles)
            w_out = w_out_odd[i]
        x = matmul_cat_resid(mix[0], mix[1], w_out.astype(BF16), x, gate_of(2), seq_starts)
        h = norm_mod(x, norm_w[layer, 1], mod6, 4, 3, seq_starts)
        x = _conv_ffn(h, w_ff_up[layer], conv_ff[layer], w_ff_down[layer], x, gate_of(5), seqs, ff_mult)
    y = final_norm(x, final_norm_w)
    return y[:bp * tp].reshape(bp, tp, d), y[bp * tp:].reshape(bs, ts, d)


def kernel(x_prompt, x_sample, c_prompt, c_sample, rel_bias, norm_w, w_mod, b_mod, w_in_even, ret_log_1m_gamma, w_out_even, w_in_odd, c_qk_norm, mla_cq_norm, mla_ckv_norm, w_uq, w_ukv, w_out_odd, w_ff_up, conv_ff, w_ff_down, final_norm_w):
    return _trunk(x_prompt, x_sample, c_prompt, c_sample, rel_bias, norm_w, w_mod, b_mod, w_in_even,
                  ret_log_1m_gamma, w_out_even, w_in_odd, c_qk_norm, mla_cq_norm, mla_ckv_norm, w_uq,
                  w_ukv, w_out_odd, w_ff_up, conv_ff, w_ff_down, final_norm_w,
                  attn_tiles=(512, 2048, 8, 2048, 2048, 8), ff_mult=1024)
```

```python
import functools
import math

import numpy as np
import jax
import jax.numpy as jnp
from jax import lax
from jax.experimental import pallas as pl
from jax.experimental.pallas import tpu as pltpu

F32 = jnp.float32
BF16 = jnp.bfloat16

HEAD_DIM = 128
GRID_W = 64
DIL_PATTERNS = ((128, 1), (512, 4), (2048, 16))
RET_CHUNK = 128
KV_C = 4
Q_LORA = 768
KV_LORA = 256
QK_NOPE = 128
QK_ROPE = 64
V_D = 128
MLA_QK_PAD = 256
N_BUCKETS = 32
MAX_DIST = 1024
ROPE_THETA = 10000.0
EPS = 1e-6
NEG = -1e30
LOG2E = math.log2(math.e)

WIN_HALF = 64
WIN_TQ = 128
BF16_ROWS = 16
VMEM_LIMIT = 56 * 1024 * 1024


def _cparams(sem):
    return pltpu.CompilerParams(dimension_semantics=sem, vmem_limit_bytes=VMEM_LIMIT)


def _seq_id(row, seq_starts):
    s = 0
    for st in seq_starts[1:]:
        s = s + (row >= st).astype(jnp.int32)
    return s


def _is_any(row, marks):
    r = row == marks[0]
    for m in marks[1:]:
        r = jnp.logical_or(r, row == m)
    return r


def _largest_tile(n, cap):
    best = 128
    for t in range(128, cap + 1, 128):
        if n % t == 0:
            best = t
    return best


def _mm_body(*refs, nk, mode):
    a_ref, b_ref = refs[0], refs[1]
    if mode == "plain":
        extra, (o_ref, acc_ref) = (), refs[2:]
    elif mode == "bias":
        extra, (o_ref, acc_ref) = refs[2:3], refs[3:]
    else:
        extra, (o_ref, acc_ref) = refs[2:4], refs[4:]

    def epilogue(r):
        if mode == "bias":
            r = r + extra[0][...]
        elif mode == "resid":
            r = extra[0][...] + extra[1][0] * r
        o_ref[...] = r.astype(o_ref.dtype)

    def prod():
        return jnp.dot(a_ref[...].astype(BF16), b_ref[...].astype(BF16), preferred_element_type=F32)

    if nk == 1:
        epilogue(prod())
        return
    k = pl.program_id(2)

    @pl.when(k == 0)
    def _():
        acc_ref[...] = prod()

    @pl.when(jnp.logical_and(k > 0, k < nk - 1))
    def _():
        acc_ref[...] += prod()

    @pl.when(k == nk - 1)
    def _():
        epilogue(acc_ref[...] + prod())


def matmul(a, b, *, out_dtype, tm=1024, tn=512, tk=4096, bias=None, resid=None, gate=None,
           seq_starts=None, name="mm"):
    m, kd = a.shape
    _, n = b.shape
    tm, tn, tk = min(tm, m), _largest_tile(n, tn), _largest_tile(kd, tk)
    assert m % tm == 0 and n % tn == 0 and kd % tk == 0, (a.shape, b.shape, tm, tn, tk)
    nk = kd // tk
    in_specs = [pl.BlockSpec((tm, tk), lambda i, j, k: (i, k)),
                pl.BlockSpec((tk, tn), lambda i, j, k: (k, j))]
    args = [a, b]
    if bias is not None:
        mode = "bias"
        in_specs.append(pl.BlockSpec((1, tn), lambda i, j, k: (0, j)))
        args.append(bias)
    elif resid is not None:
        mode = "resid"
        in_specs.append(pl.BlockSpec((tm, tn), lambda i, j, k: (i, j)))
        in_specs.append(pl.BlockSpec((1, 1, tn), lambda i, j, k: (_seq_id(i * tm, seq_starts), 0, j)))
        args += [resid, gate]
    else:
        mode = "plain"
    return pl.pallas_call(
        functools.partial(_mm_body, nk=nk, mode=mode),
        out_shape=jax.ShapeDtypeStruct((m, n), out_dtype),
        grid=(m // tm, n // tn, nk),
        in_specs=in_specs,
        out_specs=pl.BlockSpec((tm, tn), lambda i, j, k: (i, j)),
        scratch_shapes=[pltpu.VMEM((tm, tn) if nk > 1 else (8, 128), F32)],
        compiler_params=_cparams(("parallel", "parallel", "arbitrary")),
        name=name,
    )(*args)


def _mm_cat_body(a1_ref, a2_ref, b_ref, x_ref, g_ref, o_ref, *, k1):
    r = jnp.dot(a1_ref[...], b_ref[0:k1], preferred_element_type=F32)
    r = r + jnp.dot(a2_ref[...], b_ref[k1:], preferred_element_type=F32)
    o_ref[...] = (x_ref[...] + g_ref[0] * r).astype(o_ref.dtype)


def matmul_cat_resid(a1, a2, b, resid, gate, seq_starts, tm=1024, tn=512, name="mm_out"):
    m, k1 = a1.shape
    k2 = a2.shape[1]
    n = b.shape[1]
    tm, tn = min(tm, m), _largest_tile(n, tn)
    assert b.shape[0] == k1 + k2 and m % tm == 0 and k1 % BF16_ROWS == 0
    return pl.pallas_call(
        functools.partial(_mm_cat_body, k1=k1),
        out_shape=jax.ShapeDtypeStruct((m, n), resid.dtype),
        grid=(m // tm, n // tn),
        in_specs=[pl.BlockSpec((tm, k1), lambda i, j: (i, 0)),
                  pl.BlockSpec((tm, k2), lambda i, j: (i, 0)),
                  pl.BlockSpec((k1 + k2, tn), lambda i, j: (0, j)),
                  pl.BlockSpec((tm, tn), lambda i, j: (i, j)),
                  pl.BlockSpec((1, 1, tn), lambda i, j: (_seq_id(i * tm, seq_starts), 0, j))],
        out_specs=pl.BlockSpec((tm, tn), lambda i, j: (i, j)),
        compiler_params=_cparams(("parallel", "parallel")),
        name=name,
    )(a1, a2, b, resid, gate)


def _ffup_body(a_ref, ap_ref, an_ref, bg_ref, bv_ref, cwg_ref, cwv_ref, o_ref, ext_ref, *,
               tm, seq_starts, seq_ends):
    i = pl.program_id(0)
    halo = BF16_ROWS

    @pl.when(pl.program_id(1) == 0)
    def _():
        ext_ref[0:tm] = a_ref[...]
        hrow = lax.broadcasted_iota(jnp.int32, ap_ref.shape, 0)
        ext_ref[tm:] = jnp.where(hrow < halo // 2, an_ref[...], ap_ref[...])

    first = _is_any(i * tm, seq_starts)
    last = _is_any((i + 1) * tm, seq_ends)
    ext = ext_ref[...]
    rows = tm + halo
    row = lax.broadcasted_iota(jnp.int32, (tm, o_ref.shape[1]), 0)
    kill_prev = jnp.logical_and(row == 0, first)
    kill_next = jnp.logical_and(row == tm - 1, last)

    def conv(b_ref, cw_ref):
        u = jnp.dot(ext, b_ref[...], preferred_element_type=F32)
        u_prev = jnp.where(kill_prev, 0.0, pltpu.roll(u, 1, axis=0)[0:tm])
        u_next = jnp.where(kill_next, 0.0, pltpu.roll(u, rows - 1, axis=0)[0:tm])
        cw = cw_ref[...]
        return u_prev * cw[0:1] + u[0:tm] * cw[1:2] + u_next * cw[2:3]

    g = conv(bg_ref, cwg_ref)
    val = conv(bv_ref, cwv_ref)
    o_ref[...] = (g * jax.nn.sigmoid(g) * val).astype(o_ref.dtype)


def ff_up_conv_gate(h, w_up, conv_w, dff, seq_starts, seq_ends, tm=1024, tn=512):
    n, d = h.shape
    tm, tn = min(tm, n), _largest_tile(dff, tn)
    nj = dff // tn
    hb = tm // BF16_ROWS
    nhb = n // BF16_ROWS
    prev = lambda i: jnp.maximum(i * hb - 1, 0)
    nxt = lambda i: jnp.minimum((i + 1) * hb, nhb - 1)
    return pl.pallas_call(
        functools.partial(_ffup_body, tm=tm, seq_starts=seq_starts, seq_ends=seq_ends),
        out_shape=jax.ShapeDtypeStruct((n, dff), BF16),
        grid=(n // tm, nj),
        in_specs=[pl.BlockSpec((tm, d), lambda i, j: (i, 0)),
                  pl.BlockSpec((BF16_ROWS, d), lambda i, j: (prev(i), 0)),
                  pl.BlockSpec((BF16_ROWS, d), lambda i, j: (nxt(i), 0)),
                  pl.BlockSpec((d, tn), lambda i, j: (0, j)),
                  pl.BlockSpec((d, tn), lambda i, j: (0, j + nj)),
                  pl.BlockSpec((3, tn), lambda i, j: (0, j)),
                  pl.BlockSpec((3, tn), lambda i, j: (0, j + nj))],
        out_specs=pl.BlockSpec((tm, tn), lambda i, j: (i, j)),
        scratch_shapes=[pltpu.VMEM((tm + BF16_ROWS, d), BF16)],
        compiler_params=_cparams(("parallel", "arbitrary")),
        name="ff_up_conv_gate",
    )(h, h, h, w_up, w_up, conv_w, conv_w)


def _normmod_body(x_ref, w_ref, sc_ref, sh_ref, o_ref):
    x = x_ref[...]
    y = x * lax.rsqrt(jnp.mean(x * x, axis=-1, keepdims=True) + EPS) * w_ref[...]
    o_ref[...] = (y * (1.0 + sc_ref[0]) + sh_ref[0]).astype(o_ref.dtype)


def norm_mod(x, w, mod6, scale_idx, shift_idx, seq_starts, tm=256):
    n, d = x.shape
    tm = min(tm, n)
    return pl.pallas_call(
        _normmod_body,
        out_shape=jax.ShapeDtypeStruct((n, d), BF16),
        grid=(n // tm,),
        in_specs=[pl.BlockSpec((tm, d), lambda i: (i, 0)),
                  pl.BlockSpec((1, d), lambda i: (0, 0)),
                  pl.BlockSpec((1, 1, d), lambda i: (_seq_id(i * tm, seq_starts) * 6 + scale_idx, 0, 0)),
                  pl.BlockSpec((1, 1, d), lambda i: (_seq_id(i * tm, seq_starts) * 6 + shift_idx, 0, 0))],
        out_specs=pl.BlockSpec((tm, d), lambda i: (i, 0)),
        compiler_params=_cparams(("parallel",)),
        name="norm_mod",
    )(x, w.reshape(1, d), mod6, mod6)


def _final_norm_body(x_ref, w_ref, oa_ref, ob_ref, *, na_tiles):
    x = x_ref[...]
    y = x * lax.rsqrt(jnp.mean(x * x, axis=-1, keepdims=True) + EPS) * w_ref[...]
    i = pl.program_id(0)

    @pl.when(i < na_tiles)
    def _():
        oa_ref[...] = y

    @pl.when(i >= na_tiles)
    def _():
        ob_ref[...] = y


def final_norm(x, w, n_first, tm=256):
    n, d = x.shape
    tm = min(tm, n_first, n - n_first)
    assert n_first % tm == 0 and n % tm == 0
    na = n_first // tm
    return pl.pallas_call(
        functools.partial(_final_norm_body, na_tiles=na),
        out_shape=[jax.ShapeDtypeStruct((n_first, d), F32), jax.ShapeDtypeStruct((n - n_first, d), F32)],
        grid=(n // tm,),
        in_specs=[pl.BlockSpec((tm, d), lambda i: (i, 0)), pl.BlockSpec((1, d), lambda i: (0, 0))],
        out_specs=[pl.BlockSpec((tm, d), lambda i: (jnp.minimum(i, na - 1), 0)),
                   pl.BlockSpec((tm, d), lambda i: (jnp.maximum(i - na, 0), 0))],
        compiler_params=_cparams(("arbitrary",)),
        name="final_norm",
    )(x, w.reshape(1, d))


def _flash_body(q_ref, k_ref, vt_ref, o_ref, m_ref, l_ref, acc_ref, *, g_heads, dqk, dv, nkv, tq, nsplit):
    j = pl.program_id(3)

    @pl.when(j == 0)
    def _():
        m_ref[...] = jnp.full(m_ref.shape, -jnp.inf, F32)
        l_ref[...] = jnp.zeros(l_ref.shape, F32)
        acc_ref[...] = jnp.zeros(acc_ref.shape, F32)

    k = k_ref[...]
    vt1 = jnp.concatenate([vt_ref[...], jnp.ones((BF16_ROWS, vt_ref.shape[1]), BF16)], axis=0)
    if g_heads == 1:
        qcat = q_ref[...]
    else:
        qcat = jnp.concatenate([q_ref[:, g * dqk:(g + 1) * dqk] for g in range(g_heads)], axis=0)
    cs = (g_heads * tq) // nsplit
    cols = [slice(c * cs, (c + 1) * cs) for c in range(nsplit)]
    scores = [lax.dot_general(k, qcat[sl], (((1,), (1,)), ((), ())), preferred_element_type=F32) for sl in cols]
    for s, sl in zip(scores, cols):
        m_prev = m_ref[:, sl]
        m_new = jnp.maximum(m_prev, jnp.max(s, axis=0, keepdims=True))
        alpha = jnp.exp2(m_prev - m_new)
        p = jnp.exp2(s - m_new)
        pv = jnp.dot(vt1, p.astype(BF16), preferred_element_type=F32)
        l_ref[:, sl] = alpha * l_ref[:, sl] + pv[dv:dv + 1]
        acc_ref[:, sl] = alpha * acc_ref[:, sl] + pv[:dv]
        m_ref[:, sl] = m_new

    @pl.when(j == nkv - 1)
    def _():
        o_t = acc_ref[...] / l_ref[...]
        for g in range(g_heads):
            o_ref[:, g * dv:(g + 1) * dv] = o_t[:, g * tq:(g + 1) * tq].T.astype(o_ref.dtype)


def flash_attention(q, k, vt, *, n_kv_heads, g_heads, dqk, dv, row0, batch, seq_len, tq, tk, nsplit, name):
    tq, tk = min(tq, seq_len), min(tk, seq_len)
    nq, nkv = seq_len // tq, seq_len // tk
    qb0, kb0 = row0 // tq, row0 // tk
    ncol = g_heads * tq
    nsplit = min(nsplit, ncol // 128)
    return pl.pallas_call(
        functools.partial(_flash_body, g_heads=g_heads, dqk=dqk, dv=dv, nkv=nkv, tq=tq, nsplit=nsplit),
        out_shape=jax.ShapeDtypeStruct((batch * seq_len, n_kv_heads * g_heads * dv), BF16),
        grid=(batch, n_kv_heads, nq, nkv),
        in_specs=[pl.BlockSpec((tq, g_heads * dqk), lambda b, h, i, j: (qb0 + b * nq + i, h)),
                  pl.BlockSpec((tk, dqk), lambda b, h, i, j: (kb0 + b * nkv + j, h)),
                  pl.BlockSpec((dv, tk), lambda b, h, i, j: (h, kb0 + b * nkv + j))],
        out_specs=pl.BlockSpec((tq, g_heads * dv), lambda b, h, i, j: (b * nq + i, h)),
        scratch_shapes=[pltpu.VMEM((1, ncol), F32), pltpu.VMEM((1, ncol), F32), pltpu.VMEM((dv, ncol), F32)],
        compiler_params=_cparams(("parallel", "parallel", "parallel", "arbitrary")),
        name=name,
    )(q, k, vt)


def _win_body(q_ref, kp_ref, kc_ref, kn_ref, vp_ref, vc_ref, vn_ref, bias_ref, o_ref,
              kext_ref, vext_ref, *scr, tb, tq, dils, seq_starts, seq_ends):
    i = pl.program_id(0)
    first = _is_any(i * tb, seq_starts)
    last = _is_any((i + 1) * tb, seq_ends)
    hmax = WIN_HALF * max(dils)
    for ext_ref, p_ref, c_ref, n_ref in ((kext_ref, kp_ref, kc_ref, kn_ref), (vext_ref, vp_ref, vc_ref, vn_ref)):
        ext_ref[0:hmax] = p_ref[...]
        ext_ref[hmax:hmax + tb] = c_ref[...]
        ext_ref[hmax + tb:] = n_ref[...]
    nkeys = tq + 2 * WIN_HALF
    col = lax.broadcasted_iota(jnp.int32, (tq, nkeys), 1)
    bad_first = jnp.logical_and(col < WIN_HALF, first)
    bad_last = jnp.logical_and(col >= WIN_HALF + tq, last)
    scale = HEAD_DIM ** -0.5
    o_scr, l_scr = scr[:len(dils)], scr[len(dils):]
    for p, dil in enumerate(dils):
        nsub = tb // (tq * dil)
        bias = bias_ref[p]
        for sub in range(nsub):
            for r in range(dil):
                start = sub * tq * dil + r
                kstart = hmax - WIN_HALF * dil + start
                q = (q_ref[pl.ds(start, tq, stride=dil), :] * scale).astype(BF16)
                k = kext_ref[pl.ds(kstart, nkeys, stride=dil), :].astype(BF16)
                v = vext_ref[pl.ds(kstart, nkeys, stride=dil), :].astype(BF16)
                s = lax.dot_general(q, k, (((1,), (1,)), ((), ())), preferred_element_type=F32) + bias
                if sub == 0:
                    s = jnp.where(bad_first, NEG, s)
                if sub == nsub - 1:
                    s = jnp.where(bad_last, NEG, s)
                m = jnp.max(s, axis=-1, keepdims=True)
                e = jnp.exp(s - m)
                den = jnp.sum(e, axis=-1, keepdims=True)
                o = jnp.dot(e.astype(BF16), v, preferred_element_type=F32) / den
                o_scr[p][pl.ds(start, tq, stride=dil), :] = o
                l_scr[p][pl.ds(start, tq, stride=dil), :] = jnp.broadcast_to(m + jnp.log(den), (tq, HEAD_DIM))
    lses = [l[...] for l in l_scr]
    m = functools.reduce(jnp.maximum, lses)
    ws = [jnp.exp(l - m) for l in lses]
    num = functools.reduce(lambda a, b: a + b, [w * o[...] for w, o in zip(ws, o_scr)])
    o_ref[...] = (num / functools.reduce(lambda a, b: a + b, ws)).astype(o_ref.dtype)


def _t5_bucket(rel):
    nb = N_BUCKETS // 2
    max_exact = nb // 2
    n = jnp.abs(rel)
    log_ratio = jnp.log(jnp.maximum(n, 1).astype(F32) / max_exact) / math.log(MAX_DIST / max_exact)
    large = jnp.minimum(max_exact + (log_ratio * (nb - max_exact)).astype(jnp.int32), nb - 1)
    return jnp.where(rel > 0, nb, 0) + jnp.where(n < max_exact, n, large)


def _window_bias(rel_bias, dil, tq):
    kj = jnp.arange(tq + 2 * WIN_HALF)
    rel = kj[None, :] - WIN_HALF - jnp.arange(tq)[:, None]
    bias = rel_bias[_t5_bucket(rel * dil)].astype(F32)
    bias = jnp.where((jnp.abs(rel) <= WIN_HALF)[:, :, None], bias, NEG)
    return bias.transpose(2, 0, 1)


def window_attention(proj, rel_bias, dils, n_heads, seq_starts, seq_ends):
    n = proj.shape[0]
    tq = WIN_TQ
    hmax = WIN_HALF * max(dils)
    tb = tq * max(dils)
    assert all(s % tb == 0 for s in seq_starts + seq_ends) and tb % hmax == 0
    bias = jnp.stack([_window_bias(rel_bias, dil, tq) for dil in dils])
    hb = tb // hmax
    nhb = n // hmax
    prev = lambda i: jnp.maximum(i * hb - 1, 0)
    nxt = lambda i: jnp.minimum((i + 1) * hb, nhb - 1)
    cur_spec = lambda g: pl.BlockSpec((tb, HEAD_DIM), lambda i, h: (i, g * n_heads + h))
    prev_spec = lambda g: pl.BlockSpec((hmax, HEAD_DIM), lambda i, h: (prev(i), g * n_heads + h))
    next_spec = lambda g: pl.BlockSpec((hmax, HEAD_DIM), lambda i, h: (nxt(i), g * n_heads + h))
    np_ = len(dils)
    return pl.pallas_call(
        functools.partial(_win_body, tb=tb, tq=tq, dils=dils, seq_starts=seq_starts, seq_ends=seq_ends),
        out_shape=jax.ShapeDtypeStruct((n, n_heads * HEAD_DIM), BF16),
        grid=(n // tb, n_heads),
        in_specs=[cur_spec(0), prev_spec(1), cur_spec(1), next_spec(1),
                  prev_spec(2), cur_spec(2), next_spec(2),
                  pl.BlockSpec((np_, None, tq, tq + 2 * WIN_HALF), lambda i, h: (0, h, 0, 0))],
        out_specs=pl.BlockSpec((tb, HEAD_DIM), lambda i, h: (i, h)),
        scratch_shapes=[pltpu.VMEM((tb + 2 * hmax, HEAD_DIM), F32)] * 2 + [pltpu.VMEM((tb, HEAD_DIM), F32)] * (2 * np_),
        compiler_params=_cparams(("parallel", "parallel")),
        name="win_attn",
    )(proj, proj, proj, proj, proj, proj, proj, bias)


def _rope_half(x, cos, sin_signed):
    return x * cos + pltpu.roll(x, HEAD_DIM // 2, axis=1) * sin_signed


def _ret_bwd_body(lg_ref, q_ref, k_ref, v_ref, cos_ref, sin_ref, o_ref, r_ref, *,
                  n_heads, n_chunks, seq_ends):
    c = RET_CHUNK
    g = n_chunks - 1 - pl.program_id(0)

    @pl.when(_is_any((g + 1) * c, seq_ends))
    def _():
        r_ref[...] = jnp.zeros(r_ref.shape, F32)

    cos, sin = cos_ref[...], sin_ref[...]
    n = lax.broadcasted_iota(jnp.int32, (c, HEAD_DIM), 0).astype(F32)
    scale = HEAD_DIM ** -0.5
    for h in range(n_heads):
        sl = slice(h * HEAD_DIM, (h + 1) * HEAD_DIM)
        lg = lg_ref[1, h]
        q = _rope_half(q_ref[:, sl], cos, sin) * scale
        k = _rope_half(k_ref[:, sl], cos, sin)
        v = v_ref[:, sl]
        r = r_ref[h]
        qw = (q * jnp.exp(lg * (c - n))).astype(BF16)
        o_ref[:, sl] = jnp.dot(qw, r.astype(BF16), preferred_element_type=F32)
        kw_t = (k * jnp.exp(lg * n)).T.astype(BF16)
        chunk_decay = jnp.exp(jnp.full(r.shape, lg * c, F32))
        r_ref[h] = chunk_decay * r + jnp.dot(kw_t, v.astype(BF16), preferred_element_type=F32)


def _ret_fwd_body(lg_ref, q_ref, k_ref, v_ref, gate_ref, xb_ref, cos_ref, sin_ref, o_ref, r_ref, *,
                  n_heads, seq_starts):
    c = RET_CHUNK
    g = pl.program_id(0)

    @pl.when(_is_any(g * c, seq_starts))
    def _():
        r_ref[...] = jnp.zeros(r_ref.shape, F32)

    cos, sin = cos_ref[...], sin_ref[...]
    n = lax.broadcasted_iota(jnp.int32, (c, HEAD_DIM), 0).astype(F32)
    rel = (lax.broadcasted_iota(jnp.int32, (c, c), 0) - lax.broadcasted_iota(jnp.int32, (c, c), 1)).astype(F32)
    scale = HEAD_DIM ** -0.5
    for h in range(n_heads):
        sl = slice(h * HEAD_DIM, (h + 1) * HEAD_DIM)
        lg_f, lg_b = lg_ref[0, h], lg_ref[1, h]
        q = _rope_half(q_ref[:, sl], cos, sin) * scale
        k = _rope_half(k_ref[:, sl], cos, sin)
        v = v_ref[:, sl].astype(BF16)
        r = r_ref[h]
        decay = jnp.where(rel >= 0, jnp.exp(lg_f * jnp.maximum(rel, 0.0)), jnp.exp(lg_b * jnp.maximum(-rel, 0.0)))
        qk = lax.dot_general(q.astype(BF16), k.astype(BF16), (((1,), (1,)), ((), ())), preferred_element_type=F32)
        o = jnp.dot((qk * decay).astype(BF16), v, preferred_element_type=F32)
        qw = (q * jnp.exp(lg_f * (n + 1.0))).astype(BF16)
        o = o + jnp.dot(qw, r.astype(BF16), preferred_element_type=F32) + xb_ref[:, sl]
        mu = jnp.mean(o, axis=-1, keepdims=True)
        d = o - mu
        var = jnp.mean(d * d, axis=-1, keepdims=True)
        o = d * lax.rsqrt(var + EPS)
        gate = gate_ref[:, sl]
        o_ref[:, sl] = (gate * jax.nn.sigmoid(gate) * o).astype(o_ref.dtype)
        kw_t = (k * jnp.exp(lg_f * (c - 1.0 - n))).T.astype(BF16)
        chunk_decay = jnp.exp(jnp.full(r.shape, lg_f * c, F32))
        r_ref[h] = chunk_decay * r + jnp.dot(kw_t, v, preferred_element_type=F32)


def retention(proj, col0, log_gamma, cos, sin, n_heads, seq_starts, seq_ends):
    n, _ = proj.shape
    w = n_heads * HEAD_DIM
    cb0 = col0 // w
    c = RET_CHUNK
    n_chunks = n // c
    smem = pl.BlockSpec(memory_space=pltpu.SMEM)
    rev = lambda off: pl.BlockSpec((c, w), lambda s: (n_chunks - 1 - s, cb0 + off))
    rev_tab = pl.BlockSpec((c, HEAD_DIM), lambda s: (n_chunks - 1 - s, 0))
    cross_b = pl.pallas_call(
        functools.partial(_ret_bwd_body, n_heads=n_heads, n_chunks=n_chunks, seq_ends=seq_ends),
        out_shape=jax.ShapeDtypeStruct((n, w), F32),
        grid=(n_chunks,),
        in_specs=[smem, rev(0), rev(1), rev(2), rev_tab, rev_tab],
        out_specs=pl.BlockSpec((c, w), lambda s: (n_chunks - 1 - s, 0)),
        scratch_shapes=[pltpu.VMEM((n_heads, HEAD_DIM, HEAD_DIM), F32)],
        compiler_params=_cparams(("arbitrary",)),
        name="ret_bwd",
    )(log_gamma, proj, proj, proj, cos, sin)
    fwd = lambda off: pl.BlockSpec((c, w), lambda s: (s, cb0 + off))
    tab = pl.BlockSpec((c, HEAD_DIM), lambda s: (s, 0))
    return pl.pallas_call(
        functools.partial(_ret_fwd_body, n_heads=n_heads, seq_starts=seq_starts),
        out_shape=jax.ShapeDtypeStruct((n, w), BF16),
        grid=(n_chunks,),
        in_specs=[smem, fwd(0), fwd(1), fwd(2), fwd(3), pl.BlockSpec((c, w), lambda s: (s, 0)), tab, tab],
        out_specs=pl.BlockSpec((c, w), lambda s: (s, 0)),
        scratch_shapes=[pltpu.VMEM((n_heads, HEAD_DIM, HEAD_DIM), F32)],
        compiler_params=_cparams(("arbitrary",)),
        name="ret_fwd",
    )(log_gamma, proj, proj, proj, proj, cross_b, cos, sin)


def _rope_quarter(x, cos, sin_signed, low):
    partner = jnp.where(low, pltpu.roll(x, HEAD_DIM - 32, axis=1), pltpu.roll(x, 32, axis=1))
    return x * cos + partner * sin_signed


def _gqa_prep_body(q_ref, k_ref, v_ref, nw_ref, cos_ref, sin_ref, qo_ref, ko_ref, vto_ref, *, hq, hk):
    cos, sin = cos_ref[...], sin_ref[...]
    low = (lax.broadcasted_iota(jnp.int32, cos.shape, 1) % 64) < 32
    scale = HEAD_DIM ** -0.5 * LOG2E

    def norm_rope(x, w):
        y = x * lax.rsqrt(jnp.mean(x * x, axis=-1, keepdims=True) + EPS) * w
        return _rope_quarter(y, cos, sin, low)

    for h in range(hq):
        sl = slice(h * HEAD_DIM, (h + 1) * HEAD_DIM)
        qo_ref[:, sl] = (norm_rope(q_ref[:, sl], nw_ref[0:1, :]) * scale).astype(qo_ref.dtype)
    for h in range(hk):
        sl = slice(h * HEAD_DIM, (h + 1) * HEAD_DIM)
        ko_ref[:, sl] = norm_rope(k_ref[:, sl], nw_ref[1:2, :]).astype(ko_ref.dtype)
        vto_ref[sl, :] = v_ref[:, sl].T.astype(vto_ref.dtype)


def gqa_prep(proj, qk_norm, cos, sin, hq, hk, tm=256):
    n = proj.shape[0]
    tm = min(tm, n)
    wq, wk = hq * HEAD_DIM, hk * HEAD_DIM
    assert wq % wk == 0
    row = lambda width, blk: pl.BlockSpec((tm, width), lambda i: (i, blk))
    return pl.pallas_call(
        functools.partial(_gqa_prep_body, hq=hq, hk=hk),
        out_shape=[jax.ShapeDtypeStruct((n, wq), BF16), jax.ShapeDtypeStruct((n, wk), BF16),
                   jax.ShapeDtypeStruct((wk, n), BF16)],
        grid=(n // tm,),
        in_specs=[row(wq, 0), row(wk, wq // wk), row(wk, wq // wk + 1),
                  pl.BlockSpec((2, HEAD_DIM), lambda i: (0, 0)), row(HEAD_DIM, 0), row(HEAD_DIM, 0)],
        out_specs=[row(wq, 0), row(wk, 0), pl.BlockSpec((wk, tm), lambda i: (0, i))],
        compiler_params=_cparams(("parallel",)),
        name="gqa_prep",
    )(proj, proj, proj, qk_norm, cos, sin)


def _latent_norm_body(cq_ref, ckv_ref, wq_ref, wkv_ref, qo_ref, kvo_ref):
    def rms(x, w):
        return x * lax.rsqrt(jnp.mean(x * x, axis=-1, keepdims=True) + EPS) * w
    qo_ref[...] = rms(cq_ref[...], wq_ref[...]).astype(qo_ref.dtype)
    kvo_ref[...] = rms(ckv_ref[...], wkv_ref[...]).astype(kvo_ref.dtype)


def latent_norm(proj, col_cq, cq_norm, ckv_norm, tm=512):
    n = proj.shape[0]
    tm = min(tm, n)
    assert col_cq % Q_LORA == 0 and (col_cq + Q_LORA) % KV_LORA == 0
    return pl.pallas_call(
        _latent_norm_body,
        out_shape=[jax.ShapeDtypeStruct((n, Q_LORA), BF16), jax.ShapeDtypeStruct((n, KV_LORA), BF16)],
        grid=(n // tm,),
        in_specs=[pl.BlockSpec((tm, Q_LORA), lambda i: (i, col_cq // Q_LORA)),
                  pl.BlockSpec((tm, KV_LORA), lambda i: (i, (col_cq + Q_LORA) // KV_LORA)),
                  pl.BlockSpec((1, Q_LORA), lambda i: (0, 0)), pl.BlockSpec((1, KV_LORA), lambda i: (0, 0))],
        out_specs=[pl.BlockSpec((tm, Q_LORA), lambda i: (i, 0)), pl.BlockSpec((tm, KV_LORA), lambda i: (i, 0))],
        compiler_params=_cparams(("parallel",)),
        name="latent_norm",
    )(proj, proj, cq_norm.reshape(1, Q_LORA), ckv_norm.reshape(1, KV_LORA))


def _mla_q_body(a_ref, b_ref, cos_ref, sin_ref, qo_ref, *, heads):
    cos, sin = cos_ref[...], sin_ref[...]
    low = (lax.broadcasted_iota(jnp.int32, cos.shape, 1) % 64) < 32
    scale = (QK_NOPE + QK_ROPE) ** -0.5 * LOG2E
    r = jnp.dot(a_ref[...], b_ref[...], preferred_element_type=F32)
    for h in range(heads):
        a = slice(h * MLA_QK_PAD, h * MLA_QK_PAD + QK_NOPE)
        b = slice(h * MLA_QK_PAD + QK_NOPE, (h + 1) * MLA_QK_PAD)
        qo_ref[:, a] = (r[:, a] * scale).astype(qo_ref.dtype)
        qo_ref[:, b] = (_rope_quarter(r[:, b], cos, sin, low) * scale).astype(qo_ref.dtype)


def _mla_kv_body(a_ref, b_ref, kpe_ref, cos_ref, sin_ref, ko_ref, vto_ref, *, heads):
    cos, sin = cos_ref[...], sin_ref[...]
    low = (lax.broadcasted_iota(jnp.int32, cos.shape, 1) % 64) < 32
    k_rope = _rope_quarter(kpe_ref[...], cos, sin, low).astype(ko_ref.dtype)
    r = jnp.dot(a_ref[...], b_ref[...], preferred_element_type=F32)
    for h in range(heads):
        a = slice(h * MLA_QK_PAD, h * MLA_QK_PAD + QK_NOPE)
        b = slice(h * MLA_QK_PAD + QK_NOPE, (h + 1) * MLA_QK_PAD)
        ko_ref[:, a] = r[:, a].astype(ko_ref.dtype)
        ko_ref[:, b] = k_rope
        vto_ref[h * V_D:(h + 1) * V_D, :] = r[:, b].T.astype(vto_ref.dtype)


def mla_up_proj(cqn, ckvn, w_uq, w_ukv, proj, col_kpe, cos, sin, n_heads, tm=512, heads_per_step=4):
    n = cqn.shape[0]
    tm = min(tm, n)
    hs = min(heads_per_step, n_heads)
    assert n_heads % hs == 0 and col_kpe % 128 == 0
    tn = hs * MLA_QK_PAD
    wd = n_heads * MLA_QK_PAD
    grid = (n // tm, n_heads // hs)
    tab = pl.BlockSpec((tm, 128), lambda i, j: (i, 0))
    q = pl.pallas_call(
        functools.partial(_mla_q_body, heads=hs),
        out_shape=jax.ShapeDtypeStruct((n, wd), BF16),
        grid=grid,
        in_specs=[pl.BlockSpec((tm, Q_LORA), lambda i, j: (i, 0)), pl.BlockSpec((Q_LORA, tn), lambda i, j: (0, j)),
                  tab, tab],
        out_specs=pl.BlockSpec((tm, tn), lambda i, j: (i, j)),
        compiler_params=_cparams(("parallel", "parallel")),
        name="mla_q_proj",
    )(cqn, w_uq, cos, sin)
    k, vt = pl.pallas_call(
        functools.partial(_mla_kv_body, heads=hs),
        out_shape=[jax.ShapeDtypeStruct((n, wd), BF16), jax.ShapeDtypeStruct((n_heads * V_D, n), BF16)],
        grid=grid,
        in_specs=[pl.BlockSpec((tm, KV_LORA), lambda i, j: (i, 0)), pl.BlockSpec((KV_LORA, tn), lambda i, j: (0, j)),
                  pl.BlockSpec((tm, 128), lambda i, j: (i, col_kpe // 128)), tab, tab],
        out_specs=[pl.BlockSpec((tm, tn), lambda i, j: (i, j)),
                   pl.BlockSpec((hs * V_D, tm), lambda i, j: (j, i))],
        compiler_params=_cparams(("parallel", "parallel")),
        name="mla_kv_proj",
    )(ckvn, w_ukv, proj, cos, sin)
    return q, k, vt


def _inv_freq(dim):
    return (np.float32(ROPE_THETA) ** (-np.arange(0, dim, 2, dtype=np.float32) / np.float32(dim))).astype(np.float32)


def _angles(pos, dim):
    ang = pos.astype(F32)[:, None] * jnp.asarray(_inv_freq(dim))[None, :]
    return jnp.cos(ang), jnp.sin(ang)


def _rope_tables(seq_lens):
    tmax = max(seq_lens)
    pos = jnp.arange(tmax)
    per_token = lambda t: jnp.concatenate([t[:n] for n in seq_lens], axis=0)
    c, s = _angles(pos, HEAD_DIM)
    ret = (per_token(jnp.concatenate([c, c], -1)), per_token(jnp.concatenate([-s, s], -1)))
    rows = -(-tmax // GRID_W)
    cr, sr = (jnp.repeat(t, GRID_W, axis=0)[:tmax] for t in _angles(jnp.arange(rows), HEAD_DIM // 2))
    cc, sc = (jnp.tile(t, (rows, 1))[:tmax] for t in _angles(jnp.arange(GRID_W), HEAD_DIM // 2))
    axial = (per_token(jnp.concatenate([cr, cr, cc, cc], -1)), per_token(jnp.concatenate([-sr, sr, -sc, sc], -1)))
    c1, s1 = _angles(pos, QK_ROPE)
    one, zero = jnp.ones_like(c1), jnp.zeros_like(s1)
    mla = (per_token(jnp.concatenate([c1, c1, one, one], -1)), per_token(jnp.concatenate([-s1, s1, zero, zero], -1)))
    return ret, axial, mla


def _pad_cols(w, mult):
    pad = (-w.shape[-1]) % mult
    return jnp.pad(w, ((0, 0), (0, pad))) if pad else w


def _mix_even(h, w_in, log_1m_gamma, rel_bias, ret_tab, seqs):
    seq_starts, seq_ends = seqs
    n_heads = w_in.shape[1] // (7 * HEAD_DIM)
    w = n_heads * HEAD_DIM
    proj = matmul(h, w_in.astype(BF16), out_dtype=F32, name="mm_in_even")
    assert all(window // (2 * dil) == WIN_HALF for window, dil in DIL_PATTERNS)
    dils = tuple(dil for _, dil in DIL_PATTERNS)
    oa = window_attention(proj, rel_bias, dils, n_heads, seq_starts, seq_ends)
    log_gamma = jnp.log1p(-jnp.exp(log_1m_gamma.astype(F32)))
    ob = retention(proj, 3 * w, log_gamma, ret_tab[0], ret_tab[1], n_heads, seq_starts, seq_ends)
    return oa, ob


def _mix_odd(h, w_in, qk_norm, cq_norm, ckv_norm, w_uq, w_ukv, ax_tab, mla_tab, groups, tiles):
    in_odd = w_in.shape[1]
    wkv = KV_C * HEAD_DIM
    wq = in_odd - 2 * wkv - Q_LORA - KV_LORA - QK_ROPE
    hq = wq // HEAD_DIM
    proj = matmul(h, _pad_cols(w_in, 512).astype(BF16), out_dtype=F32, name="mm_in_odd")
    qc, kc, vtc = gqa_prep(proj, qk_norm, ax_tab[0], ax_tab[1], hq, KV_C)
    col_cq = wq + 2 * wkv
    cqn, ckvn = latent_norm(proj, col_cq, cq_norm, ckv_norm)
    n_hd = w_uq.shape[1] // (QK_NOPE + QK_ROPE)
    w_uq_p = jnp.pad(w_uq.reshape(Q_LORA, n_hd, QK_NOPE + QK_ROPE),
                     ((0, 0), (0, 0), (0, MLA_QK_PAD - QK_NOPE - QK_ROPE)))
    assert V_D == MLA_QK_PAD - QK_NOPE and w_ukv.shape[1] == n_hd * MLA_QK_PAD
    q_d, k_d, vt_d = mla_up_proj(cqn, ckvn, w_uq_p.reshape(Q_LORA, n_hd * MLA_QK_PAD).astype(BF16),
                                 w_ukv.astype(BF16), proj, col_cq + Q_LORA + KV_LORA, mla_tab[0], mla_tab[1], n_hd)
    ocs, ods = [], []
    for row0, batch, seq_len in groups:
        ocs.append(flash_attention(qc, kc, vtc, n_kv_heads=KV_C, g_heads=hq // KV_C, dqk=HEAD_DIM, dv=HEAD_DIM,
                                   row0=row0, batch=batch, seq_len=seq_len, tq=tiles[0], tk=tiles[1],
                                   nsplit=tiles[2], name="attn_gqa"))
        ods.append(flash_attention(q_d, k_d, vt_d, n_kv_heads=n_hd, g_heads=1, dqk=MLA_QK_PAD, dv=V_D,
                                   row0=row0, batch=batch, seq_len=seq_len, tq=tiles[3], tk=tiles[4],
                                   nsplit=tiles[5], name="attn_mla"))
    return jnp.concatenate(ocs, 0), jnp.concatenate(ods, 0)


def _conv_ffn(h, w_up, conv_w, w_down, x, gate, seqs, ff_mult):
    seq_starts, seq_ends = seqs
    dff = w_down.shape[0]
    dffp = -(-dff // ff_mult) * ff_mult
    pad = dffp - dff
    split_pad = lambda w: jnp.concatenate([jnp.pad(w[:, :dff], ((0, 0), (0, pad))),
                                           jnp.pad(w[:, dff:], ((0, 0), (0, pad)))], axis=1)
    act = ff_up_conv_gate(h, split_pad(w_up).astype(BF16), split_pad(conv_w), dffp, seq_starts, seq_ends)
    w_down_p = jnp.pad(w_down, ((0, pad), (0, 0))).astype(BF16)
    return matmul(act, w_down_p, out_dtype=F32, tn=1024, tk=2816, resid=x, gate=gate, seq_starts=seq_starts,
                  name="mm_ff_down")


def _trunk(x_prompt, x_sample, c_prompt, c_sample, rel_bias, norm_w, w_mod, b_mod, w_in_even,
           ret_log_1m_gamma, w_out_even, w_in_odd, c_qk_norm, mla_cq_norm, mla_ckv_norm, w_uq,
           w_ukv, w_out_odd, w_ff_up, conv_ff, w_ff_down, final_norm_w, *, attn_tiles, ff_mult):
    bp, tp, d = x_prompt.shape
    bs, ts, _ = x_sample.shape
    depth = norm_w.shape[0]
    seq_lens = (tp,) * bp + (ts,) * bs
    seq_starts = tuple(int(v) for v in np.cumsum((0,) + seq_lens[:-1]))
    seq_ends = tuple(int(v) for v in np.cumsum(seq_lens))
    seqs = (seq_starts, seq_ends)
    groups = ((0, bp, tp), (bp * tp, bs, ts))
    n_seq = len(seq_lens)

    x = jnp.concatenate([x_prompt.reshape(bp * tp, d), x_sample.reshape(bs * ts, d)], axis=0)
    c = jnp.concatenate([c_prompt, c_sample], axis=0)
    c_act = jnp.pad(jax.nn.silu(c), ((0, 8 - n_seq), (0, 0))).astype(BF16)
    ret_tab, ax_tab, mla_tab = _rope_tables(seq_lens)

    for layer in range(depth):
        mod = matmul(c_act, w_mod[layer], out_dtype=F32, tm=8, tn=2048, tk=1024,
                     bias=b_mod[layer].reshape(1, -1), name="mm_mod")
        mod6 = mod[:n_seq].reshape(n_seq * 6, 1, d)
        gate_of = lambda ci: mod6.reshape(n_seq, 6, d)[:, ci].reshape(n_seq, 1, d)
        h = norm_mod(x, norm_w[layer, 0], mod6, 1, 0, seq_starts)
        i = layer // 2
        if layer % 2 == 0:
            mix = _mix_even(h, w_in_even[i], ret_log_1m_gamma[i], rel_bias, ret_tab, seqs)
            w_out = w_out_even[i]
        else:
            mix = _mix_odd(h, w_in_odd[i], c_qk_norm[i], mla_cq_norm[i], mla_ckv_norm[i], w_uq[i], w_ukv[i],
                           ax_tab, mla_tab, groups, attn_tiles)
            w_out = w_out_odd[i]
        x = matmul_cat_resid(mix[0], mix[1], w_out.astype(BF16), x, gate_of(2), seq_starts)
        h = norm_mod(x, norm_w[layer, 1], mod6, 4, 3, seq_starts)
        x = _conv_ffn(h, w_ff_up[layer], conv_ff[layer], w_ff_down[layer], x, gate_of(5), seqs, ff_mult)
    y_p, y_s = final_norm(x, final_norm_w, bp * tp)
    return y_p.reshape(bp, tp, d), y_s.reshape(bs, ts, d)


def kernel(x_prompt, x_sample, c_prompt, c_sample, rel_bias, norm_w, w_mod, b_mod, w_in_even, ret_log_1m_gamma, w_out_even, w_in_odd, c_qk_norm, mla_cq_norm, mla_ckv_norm, w_uq, w_ukv, w_out_odd, w_ff_up, conv_ff, w_ff_down, final_norm_w):
    return _trunk(x_prompt, x_sample, c_prompt, c_sample, rel_bias, norm_w, w_mod, b_mod, w_in_even,
                  ret_log_1m_gamma, w_out_even, w_in_odd, c_qk_norm, mla_cq_norm, mla_ckv_norm, w_uq,
                  w_ukv, w_out_odd, w_ff_up, conv_ff, w_ff_down, final_norm_w,
                  attn_tiles=(512, 2048, 8, 2048, 2048, 8), ff_mult=1024)
```

```python
import functools
import math

import numpy as np
import jax
import jax.numpy as jnp
from jax import lax
from jax.experimental import pallas as pl
from jax.experimental.pallas import tpu as pltpu

F32 = jnp.float32
BF16 = jnp.bfloat16

HEAD_DIM = 128
GRID_W = 64
DIL_PATTERNS = ((128, 1), (512, 4), (2048, 16))
RET_CHUNK = 128
KV_C = 4
Q_LORA = 768
KV_LORA = 256
QK_NOPE = 128
QK_ROPE = 64
V_D = 128
MLA_QK_PAD = 256
N_BUCKETS = 32
MAX_DIST = 1024
ROPE_THETA = 10000.0
EPS = 1e-6
NEG = -1e30
LOG2E = math.log2(math.e)
FLASH_HEADROOM = 60.0

WIN_HALF = 64
WIN_TQ = 128
BF16_ROWS = 16
VMEM_LIMIT = 56 * 1024 * 1024


def _cparams(sem):
    return pltpu.CompilerParams(dimension_semantics=sem, vmem_limit_bytes=VMEM_LIMIT)


def _seq_id(row, seq_starts):
    s = 0
    for st in seq_starts[1:]:
        s = s + (row >= st).astype(jnp.int32)
    return s


def _is_any(row, marks):
    r = row == marks[0]
    for m in marks[1:]:
        r = jnp.logical_or(r, row == m)
    return r


def _largest_tile(n, cap):
    best = 128
    for t in range(128, cap + 1, 128):
        if n % t == 0:
            best = t
    return best


def _mm_body(*refs, nk, mode):
    a_ref, b_ref = refs[0], refs[1]
    if mode == "plain":
        extra, (o_ref, acc_ref) = (), refs[2:]
    elif mode == "bias":
        extra, (o_ref, acc_ref) = refs[2:3], refs[3:]
    else:
        extra, (o_ref, acc_ref) = refs[2:4], refs[4:]

    def epilogue(r):
        if mode == "bias":
            r = r + extra[0][...]
        elif mode == "resid":
            r = extra[0][...] + extra[1][0] * r
        o_ref[...] = r.astype(o_ref.dtype)

    def prod():
        return jnp.dot(a_ref[...].astype(BF16), b_ref[...].astype(BF16), preferred_element_type=F32)

    if nk == 1:
        epilogue(prod())
        return
    k = pl.program_id(2)

    @pl.when(k == 0)
    def _():
        acc_ref[...] = prod()

    @pl.when(jnp.logical_and(k > 0, k < nk - 1))
    def _():
        acc_ref[...] += prod()

    @pl.when(k == nk - 1)
    def _():
        epilogue(acc_ref[...] + prod())


def matmul(a, b, *, out_dtype, tm=1024, tn=512, tk=4096, bias=None, resid=None, gate=None,
           seq_starts=None, name="mm"):
    m, kd = a.shape
    _, n = b.shape
    tm, tn, tk = min(tm, m), _largest_tile(n, tn), _largest_tile(kd, tk)
    assert m % tm == 0 and n % tn == 0 and kd % tk == 0, (a.shape, b.shape, tm, tn, tk)
    nk = kd // tk
    in_specs = [pl.BlockSpec((tm, tk), lambda i, j, k: (i, k)),
                pl.BlockSpec((tk, tn), lambda i, j, k: (k, j))]
    args = [a, b]
    if bias is not None:
        mode = "bias"
        in_specs.append(pl.BlockSpec((1, tn), lambda i, j, k: (0, j)))
        args.append(bias)
    elif resid is not None:
        mode = "resid"
        in_specs.append(pl.BlockSpec((tm, tn), lambda i, j, k: (i, j)))
        in_specs.append(pl.BlockSpec((1, 1, tn), lambda i, j, k: (_seq_id(i * tm, seq_starts), 0, j)))
        args += [resid, gate]
    else:
        mode = "plain"
    return pl.pallas_call(
        functools.partial(_mm_body, nk=nk, mode=mode),
        out_shape=jax.ShapeDtypeStruct((m, n), out_dtype),
        grid=(m // tm, n // tn, nk),
        in_specs=in_specs,
        out_specs=pl.BlockSpec((tm, tn), lambda i, j, k: (i, j)),
        scratch_shapes=[pltpu.VMEM((tm, tn) if nk > 1 else (8, 128), F32)],
        compiler_params=_cparams(("parallel", "parallel", "arbitrary")),
        name=name,
    )(*args)


def _mm_cat_body(a1_ref, a2_ref, b_ref, x_ref, g_ref, o_ref, *, k1):
    r = jnp.dot(a1_ref[...], b_ref[0:k1], preferred_element_type=F32)
    r = r + jnp.dot(a2_ref[...], b_ref[k1:], preferred_element_type=F32)
    o_ref[...] = (x_ref[...] + g_ref[0] * r).astype(o_ref.dtype)


def matmul_cat_resid(a1, a2, b, resid, gate, seq_starts, tm=1024, tn=512, name="mm_out"):
    m, k1 = a1.shape
    k2 = a2.shape[1]
    n = b.shape[1]
    tm, tn = min(tm, m), _largest_tile(n, tn)
    assert b.shape[0] == k1 + k2 and m % tm == 0 and k1 % BF16_ROWS == 0
    return pl.pallas_call(
        functools.partial(_mm_cat_body, k1=k1),
        out_shape=jax.ShapeDtypeStruct((m, n), resid.dtype),
        grid=(m // tm, n // tn),
        in_specs=[pl.BlockSpec((tm, k1), lambda i, j: (i, 0)),
                  pl.BlockSpec((tm, k2), lambda i, j: (i, 0)),
                  pl.BlockSpec((k1 + k2, tn), lambda i, j: (0, j)),
                  pl.BlockSpec((tm, tn), lambda i, j: (i, j)),
                  pl.BlockSpec((1, 1, tn), lambda i, j: (_seq_id(i * tm, seq_starts), 0, j))],
        out_specs=pl.BlockSpec((tm, tn), lambda i, j: (i, j)),
        compiler_params=_cparams(("parallel", "parallel")),
        name=name,
    )(a1, a2, b, resid, gate)


def _ffup_body(a_ref, ap_ref, an_ref, bg_ref, bv_ref, cwg_ref, cwv_ref, o_ref, ext_ref, *,
               tm, seq_starts, seq_ends):
    i = pl.program_id(0)
    halo = BF16_ROWS

    @pl.when(pl.program_id(1) == 0)
    def _():
        ext_ref[0:tm] = a_ref[...]
        hrow = lax.broadcasted_iota(jnp.int32, ap_ref.shape, 0)
        ext_ref[tm:] = jnp.where(hrow < halo // 2, an_ref[...], ap_ref[...])

    first = _is_any(i * tm, seq_starts)
    last = _is_any((i + 1) * tm, seq_ends)
    ext = ext_ref[...]
    rows = tm + halo
    row = lax.broadcasted_iota(jnp.int32, (tm, o_ref.shape[1]), 0)
    kill_prev = jnp.logical_and(row == 0, first)
    kill_next = jnp.logical_and(row == tm - 1, last)

    def conv(b_ref, cw_ref):
        u = jnp.dot(ext, b_ref[...], preferred_element_type=F32)
        u_prev = jnp.where(kill_prev, 0.0, pltpu.roll(u, 1, axis=0)[0:tm])
        u_next = jnp.where(kill_next, 0.0, pltpu.roll(u, rows - 1, axis=0)[0:tm])
        cw = cw_ref[...]
        return u_prev * cw[0:1] + u[0:tm] * cw[1:2] + u_next * cw[2:3]

    g = conv(bg_ref, cwg_ref)
    val = conv(bv_ref, cwv_ref)
    o_ref[...] = (g * jax.nn.sigmoid(g) * val).astype(o_ref.dtype)


def ff_up_conv_gate(h, w_up, conv_w, dff, seq_starts, seq_ends, tm=1024, tn=512):
    n, d = h.shape
    tm, tn = min(tm, n), _largest_tile(dff, tn)
    nj = dff // tn
    hb = tm // BF16_ROWS
    nhb = n // BF16_ROWS
    prev = lambda i: jnp.maximum(i * hb - 1, 0)
    nxt = lambda i: jnp.minimum((i + 1) * hb, nhb - 1)
    return pl.pallas_call(
        functools.partial(_ffup_body, tm=tm, seq_starts=seq_starts, seq_ends=seq_ends),
        out_shape=jax.ShapeDtypeStruct((n, dff), BF16),
        grid=(n // tm, nj),
        in_specs=[pl.BlockSpec((tm, d), lambda i, j: (i, 0)),
                  pl.BlockSpec((BF16_ROWS, d), lambda i, j: (prev(i), 0)),
                  pl.BlockSpec((BF16_ROWS, d), lambda i, j: (nxt(i), 0)),
                  pl.BlockSpec((d, tn), lambda i, j: (0, j)),
                  pl.BlockSpec((d, tn), lambda i, j: (0, j + nj)),
                  pl.BlockSpec((3, tn), lambda i, j: (0, j)),
                  pl.BlockSpec((3, tn), lambda i, j: (0, j + nj))],
        out_specs=pl.BlockSpec((tm, tn), lambda i, j: (i, j)),
        scratch_shapes=[pltpu.VMEM((tm + BF16_ROWS, d), BF16)],
        compiler_params=_cparams(("parallel", "arbitrary")),
        name="ff_up_conv_gate",
    )(h, h, h, w_up, w_up, conv_w, conv_w)


def _normmod_body(x_ref, w_ref, sc_ref, sh_ref, o_ref):
    x = x_ref[...]
    y = x * lax.rsqrt(jnp.mean(x * x, axis=-1, keepdims=True) + EPS) * w_ref[...]
    o_ref[...] = (y * (1.0 + sc_ref[0]) + sh_ref[0]).astype(o_ref.dtype)


def norm_mod(x, w, mod6, scale_idx, shift_idx, seq_starts, tm=256):
    n, d = x.shape
    tm = min(tm, n)
    return pl.pallas_call(
        _normmod_body,
        out_shape=jax.ShapeDtypeStruct((n, d), BF16),
        grid=(n // tm,),
        in_specs=[pl.BlockSpec((tm, d), lambda i: (i, 0)),
                  pl.BlockSpec((1, d), lambda i: (0, 0)),
                  pl.BlockSpec((1, 1, d), lambda i: (_seq_id(i * tm, seq_starts) * 6 + scale_idx, 0, 0)),
                  pl.BlockSpec((1, 1, d), lambda i: (_seq_id(i * tm, seq_starts) * 6 + shift_idx, 0, 0))],
        out_specs=pl.BlockSpec((tm, d), lambda i: (i, 0)),
        compiler_params=_cparams(("parallel",)),
        name="norm_mod",
    )(x, w.reshape(1, d), mod6, mod6)


def _final_norm_body(x_ref, w_ref, oa_ref, ob_ref, *, na_tiles):
    x = x_ref[...]
    y = x * lax.rsqrt(jnp.mean(x * x, axis=-1, keepdims=True) + EPS) * w_ref[...]
    i = pl.program_id(0)

    @pl.when(i < na_tiles)
    def _():
        oa_ref[...] = y

    @pl.when(i >= na_tiles)
    def _():
        ob_ref[...] = y


def final_norm(x, w, n_first, tm=256):
    n, d = x.shape
    tm = min(tm, n_first, n - n_first)
    assert n_first % tm == 0 and n % tm == 0
    na = n_first // tm
    return pl.pallas_call(
        functools.partial(_final_norm_body, na_tiles=na),
        out_shape=[jax.ShapeDtypeStruct((n_first, d), F32), jax.ShapeDtypeStruct((n - n_first, d), F32)],
        grid=(n // tm,),
        in_specs=[pl.BlockSpec((tm, d), lambda i: (i, 0)), pl.BlockSpec((1, d), lambda i: (0, 0))],
        out_specs=[pl.BlockSpec((tm, d), lambda i: (jnp.minimum(i, na - 1), 0)),
                   pl.BlockSpec((tm, d), lambda i: (jnp.maximum(i - na, 0), 0))],
        compiler_params=_cparams(("arbitrary",)),
        name="final_norm",
    )(x, w.reshape(1, d))


def _flash_body(q_ref, k_ref, vt_ref, o_ref, m_ref, l_ref, acc_ref, pv_ref, mc_ref, *,
                g_heads, dqk, dv, nkv, tq, nsplit):
    j = pl.program_id(3)

    @pl.when(j == 0)
    def _():
        m_ref[...] = jnp.full(m_ref.shape, -jnp.inf, F32)
        l_ref[...] = jnp.zeros(l_ref.shape, F32)
        acc_ref[...] = jnp.zeros(acc_ref.shape, F32)
        mc_ref[...] = jnp.zeros(mc_ref.shape, F32)

    k = k_ref[...]
    vt1 = jnp.concatenate([vt_ref[...], jnp.ones((BF16_ROWS, vt_ref.shape[1]), BF16)], axis=0)
    if g_heads == 1:
        qcat = q_ref[...]
    else:
        qcat = jnp.concatenate([q_ref[:, g * dqk:(g + 1) * dqk] for g in range(g_heads)], axis=0)
    cs = (g_heads * tq) // nsplit
    cols = [slice(c * cs, (c + 1) * cs) for c in range(nsplit)]
    qk = lambda sl: lax.dot_general(k, qcat[sl], (((1,), (1,)), ((), ())), preferred_element_type=F32)

    @pl.when(j > 0)
    def _():
        ps = []
        for sl in cols:
            s = qk(sl)
            mc_ref[:, sl] = jnp.max(s, axis=0, keepdims=True)
            ps.append(jnp.exp2(s - m_ref[:, sl]).astype(BF16))
        for p, sl in zip(ps, cols):
            pv_ref[:, sl] = jnp.dot(vt1, p, preferred_element_type=F32)

    single_pass_ok = jnp.max(mc_ref[...] - m_ref[...]) <= FLASH_HEADROOM

    @pl.when(single_pass_ok)
    def _():
        m_old = m_ref[...]
        m_new = jnp.maximum(m_old, mc_ref[...])
        alpha = jnp.exp2(m_old - m_new)
        l_ref[...] = alpha * (l_ref[...] + pv_ref[dv:dv + 1, :])
        acc_ref[...] = alpha * (acc_ref[...] + pv_ref[0:dv, :])
        m_ref[...] = m_new

    @pl.when(jnp.logical_not(single_pass_ok))
    def _():
        scores = [qk(sl) for sl in cols]
        for s, sl in zip(scores, cols):
            m_prev = m_ref[:, sl]
            m_new = jnp.maximum(m_prev, jnp.max(s, axis=0, keepdims=True))
            alpha = jnp.exp2(m_prev - m_new)
            p = jnp.exp2(s - m_new)
            pv = jnp.dot(vt1, p.astype(BF16), preferred_element_type=F32)
            l_ref[:, sl] = alpha * l_ref[:, sl] + pv[dv:dv + 1]
            acc_ref[:, sl] = alpha * acc_ref[:, sl] + pv[:dv]
            m_ref[:, sl] = m_new

    @pl.when(j == nkv - 1)
    def _():
        o_t = acc_ref[...] / l_ref[...]
        for g in range(g_heads):
            o_ref[:, g * dv:(g + 1) * dv] = o_t[:, g * tq:(g + 1) * tq].T.astype(o_ref.dtype)


def flash_attention(q, k, vt, *, n_kv_heads, g_heads, dqk, dv, row0, batch, seq_len, tq, tk, nsplit, name):
    tq, tk = min(tq, seq_len), min(tk, seq_len)
    nq, nkv = seq_len // tq, seq_len // tk
    qb0, kb0 = row0 // tq, row0 // tk
    ncol = g_heads * tq
    nsplit = min(nsplit, ncol // 128)
    return pl.pallas_call(
        functools.partial(_flash_body, g_heads=g_heads, dqk=dqk, dv=dv, nkv=nkv, tq=tq, nsplit=nsplit),
        out_shape=jax.ShapeDtypeStruct((batch * seq_len, n_kv_heads * g_heads * dv), BF16),
        grid=(batch, n_kv_heads, nq, nkv),
        in_specs=[pl.BlockSpec((tq, g_heads * dqk), lambda b, h, i, j: (qb0 + b * nq + i, h)),
                  pl.BlockSpec((tk, dqk), lambda b, h, i, j: (kb0 + b * nkv + j, h)),
                  pl.BlockSpec((dv, tk), lambda b, h, i, j: (h, kb0 + b * nkv + j))],
        out_specs=pl.BlockSpec((tq, g_heads * dv), lambda b, h, i, j: (b * nq + i, h)),
        scratch_shapes=[pltpu.VMEM((1, ncol), F32), pltpu.VMEM((1, ncol), F32), pltpu.VMEM((dv, ncol), F32),
                        pltpu.VMEM((dv + BF16_ROWS, ncol), F32), pltpu.VMEM((1, ncol), F32)],
        compiler_params=_cparams(("parallel", "parallel", "parallel", "arbitrary")),
        name=name,
    )(q, k, vt)


def _win_body(q_ref, kp_ref, kc_ref, kn_ref, vp_ref, vc_ref, vn_ref, bias_ref, o_ref,
              kext_ref, vext_ref, *scr, tb, tq, dils, seq_starts, seq_ends):
    i = pl.program_id(0)
    first = _is_any(i * tb, seq_starts)
    last = _is_any((i + 1) * tb, seq_ends)
    hmax = WIN_HALF * max(dils)
    for ext_ref, p_ref, c_ref, n_ref in ((kext_ref, kp_ref, kc_ref, kn_ref), (vext_ref, vp_ref, vc_ref, vn_ref)):
        ext_ref[0:hmax] = p_ref[...]
        ext_ref[hmax:hmax + tb] = c_ref[...]
        ext_ref[hmax + tb:] = n_ref[...]
    nkeys = tq + 2 * WIN_HALF
    col = lax.broadcasted_iota(jnp.int32, (tq, nkeys), 1)
    bad_first = jnp.logical_and(col < WIN_HALF, first)
    bad_last = jnp.logical_and(col >= WIN_HALF + tq, last)
    scale = HEAD_DIM ** -0.5
    o_scr, l_scr = scr[:len(dils)], scr[len(dils):]
    for p, dil in enumerate(dils):
        nsub = tb // (tq * dil)
        bias = bias_ref[p]
        for sub in range(nsub):
            for r in range(dil):
                start = sub * tq * dil + r
                kstart = hmax - WIN_HALF * dil + start
                q = (q_ref[pl.ds(start, tq, stride=dil), :] * scale).astype(BF16)
                k = kext_ref[pl.ds(kstart, nkeys, stride=dil), :].astype(BF16)
                v = vext_ref[pl.ds(kstart, nkeys, stride=dil), :].astype(BF16)
                s = lax.dot_general(q, k, (((1,), (1,)), ((), ())), preferred_element_type=F32) + bias
                if sub == 0:
                    s = jnp.where(bad_first, NEG, s)
                if sub == nsub - 1:
                    s = jnp.where(bad_last, NEG, s)
                m = jnp.max(s, axis=-1, keepdims=True)
                e = jnp.exp(s - m)
                den = jnp.sum(e, axis=-1, keepdims=True)
                o = jnp.dot(e.astype(BF16), v, preferred_element_type=F32) / den
                o_scr[p][pl.ds(start, tq, stride=dil), :] = o
                l_scr[p][pl.ds(start, tq, stride=dil), :] = jnp.broadcast_to(m + jnp.log(den), (tq, HEAD_DIM))
    lses = [l[...] for l in l_scr]
    m = functools.reduce(jnp.maximum, lses)
    ws = [jnp.exp(l - m) for l in lses]
    num = functools.reduce(lambda a, b: a + b, [w * o[...] for w, o in zip(ws, o_scr)])
    o_ref[...] = (num / functools.reduce(lambda a, b: a + b, ws)).astype(o_ref.dtype)


def _t5_bucket(rel):
    nb = N_BUCKETS // 2
    max_exact = nb // 2
    n = jnp.abs(rel)
    log_ratio = jnp.log(jnp.maximum(n, 1).astype(F32) / max_exact) / math.log(MAX_DIST / max_exact)
    large = jnp.minimum(max_exact + (log_ratio * (nb - max_exact)).astype(jnp.int32), nb - 1)
    return jnp.where(rel > 0, nb, 0) + jnp.where(n < max_exact, n, large)


def _window_bias(rel_bias, dil, tq):
    kj = jnp.arange(tq + 2 * WIN_HALF)
    rel = kj[None, :] - WIN_HALF - jnp.arange(tq)[:, None]
    bias = rel_bias[_t5_bucket(rel * dil)].astype(F32)
    bias = jnp.where((jnp.abs(rel) <= WIN_HALF)[:, :, None], bias, NEG)
    return bias.transpose(2, 0, 1)


def window_attention(proj, rel_bias, dils, n_heads, seq_starts, seq_ends):
    n = proj.shape[0]
    tq = WIN_TQ
    hmax = WIN_HALF * max(dils)
    tb = tq * max(dils)
    assert all(s % tb == 0 for s in seq_starts + seq_ends) and tb % hmax == 0
    bias = jnp.stack([_window_bias(rel_bias, dil, tq) for dil in dils])
    hb = tb // hmax
    nhb = n // hmax
    prev = lambda i: jnp.maximum(i * hb - 1, 0)
    nxt = lambda i: jnp.minimum((i + 1) * hb, nhb - 1)
    cur_spec = lambda g: pl.BlockSpec((tb, HEAD_DIM), lambda i, h: (i, g * n_heads + h))
    prev_spec = lambda g: pl.BlockSpec((hmax, HEAD_DIM), lambda i, h: (prev(i), g * n_heads + h))
    next_spec = lambda g: pl.BlockSpec((hmax, HEAD_DIM), lambda i, h: (nxt(i), g * n_heads + h))
    np_ = len(dils)
    return pl.pallas_call(
        functools.partial(_win_body, tb=tb, tq=tq, dils=dils, seq_starts=seq_starts, seq_ends=seq_ends),
        out_shape=jax.ShapeDtypeStruct((n, n_heads * HEAD_DIM), BF16),
        grid=(n // tb, n_heads),
        in_specs=[cur_spec(0), prev_spec(1), cur_spec(1), next_spec(1),
                  prev_spec(2), cur_spec(2), next_spec(2),
                  pl.BlockSpec((np_, None, tq, tq + 2 * WIN_HALF), lambda i, h: (0, h, 0, 0))],
        out_specs=pl.BlockSpec((tb, HEAD_DIM), lambda i, h: (i, h)),
        scratch_shapes=[pltpu.VMEM((tb + 2 * hmax, HEAD_DIM), F32)] * 2 + [pltpu.VMEM((tb, HEAD_DIM), F32)] * (2 * np_),
        compiler_params=_cparams(("parallel", "parallel")),
        name="win_attn",
    )(proj, proj, proj, proj, proj, proj, proj, bias)


def _rope_half(x, cos, sin_signed):
    return x * cos + pltpu.roll(x, HEAD_DIM // 2, axis=1) * sin_signed


def _ret_bwd_body(lg_ref, q_ref, k_ref, v_ref, cos_ref, sin_ref, o_ref, r_ref, *,
                  n_heads, n_chunks, seq_ends):
    c = RET_CHUNK
    g = n_chunks - 1 - pl.program_id(0)

    @pl.when(_is_any((g + 1) * c, seq_ends))
    def _():
        r_ref[...] = jnp.zeros(r_ref.shape, F32)

    cos, sin = cos_ref[...], sin_ref[...]
    n = lax.broadcasted_iota(jnp.int32, (c, HEAD_DIM), 0).astype(F32)
    scale = HEAD_DIM ** -0.5
    for h in range(n_heads):
        sl = slice(h * HEAD_DIM, (h + 1) * HEAD_DIM)
        lg = lg_ref[1, h]
        q = _rope_half(q_ref[:, sl], cos, sin) * scale
        k = _rope_half(k_ref[:, sl], cos, sin)
        v = v_ref[:, sl]
        r = r_ref[h]
        qw = (q * jnp.exp(lg * (c - n))).astype(BF16)
        o_ref[:, sl] = jnp.dot(qw, r.astype(BF16), preferred_element_type=F32)
        kw_t = (k * jnp.exp(lg * n)).T.astype(BF16)
        chunk_decay = jnp.exp(jnp.full(r.shape, lg * c, F32))
        r_ref[h] = chunk_decay * r + jnp.dot(kw_t, v.astype(BF16), preferred_element_type=F32)


def _ret_fwd_body(lg_ref, q_ref, k_ref, v_ref, gate_ref, xb_ref, cos_ref, sin_ref, o_ref, r_ref, *,
                  n_heads, seq_starts):
    c = RET_CHUNK
    g = pl.program_id(0)

    @pl.when(_is_any(g * c, seq_starts))
    def _():
        r_ref[...] = jnp.zeros(r_ref.shape, F32)

    cos, sin = cos_ref[...], sin_ref[...]
    n = lax.broadcasted_iota(jnp.int32, (c, HEAD_DIM), 0).astype(F32)
    rel = (lax.broadcasted_iota(jnp.int32, (c, c), 0) - lax.broadcasted_iota(jnp.int32, (c, c), 1)).astype(F32)
    scale = HEAD_DIM ** -0.5
    for h in range(n_heads):
        sl = slice(h * HEAD_DIM, (h + 1) * HEAD_DIM)
        lg_f, lg_b = lg_ref[0, h], lg_ref[1, h]
        q = _rope_half(q_ref[:, sl], cos, sin) * scale
        k = _rope_half(k_ref[:, sl], cos, sin)
        v = v_ref[:, sl].astype(BF16)
        r = r_ref[h]
        decay = jnp.where(rel >= 0, jnp.exp(lg_f * jnp.maximum(rel, 0.0)), jnp.exp(lg_b * jnp.maximum(-rel, 0.0)))
        qk = lax.dot_general(q.astype(BF16), k.astype(BF16), (((1,), (1,)), ((), ())), preferred_element_type=F32)
        o = jnp.dot((qk * decay).astype(BF16), v, preferred_element_type=F32)
        qw = (q * jnp.exp(lg_f * (n + 1.0))).astype(BF16)
        o = o + jnp.dot(qw, r.astype(BF16), preferred_element_type=F32) + xb_ref[:, sl]
        mu = jnp.mean(o, axis=-1, keepdims=True)
        d = o - mu
        var = jnp.mean(d * d, axis=-1, keepdims=True)
        o = d * lax.rsqrt(var + EPS)
        gate = gate_ref[:, sl]
        o_ref[:, sl] = (gate * jax.nn.sigmoid(gate) * o).astype(o_ref.dtype)
        kw_t = (k * jnp.exp(lg_f * (c - 1.0 - n))).T.astype(BF16)
        chunk_decay = jnp.exp(jnp.full(r.shape, lg_f * c, F32))
        r_ref[h] = chunk_decay * r + jnp.dot(kw_t, v, preferred_element_type=F32)


def retention(proj, col0, log_gamma, cos, sin, n_heads, seq_starts, seq_ends):
    n, _ = proj.shape
    w = n_heads * HEAD_DIM
    cb0 = col0 // w
    c = RET_CHUNK
    n_chunks = n // c
    smem = pl.BlockSpec(memory_space=pltpu.SMEM)
    rev = lambda off: pl.BlockSpec((c, w), lambda s: (n_chunks - 1 - s, cb0 + off))
    rev_tab = pl.BlockSpec((c, HEAD_DIM), lambda s: (n_chunks - 1 - s, 0))
    cross_b = pl.pallas_call(
        functools.partial(_ret_bwd_body, n_heads=n_heads, n_chunks=n_chunks, seq_ends=seq_ends),
        out_shape=jax.ShapeDtypeStruct((n, w), F32),
        grid=(n_chunks,),
        in_specs=[smem, rev(0), rev(1), rev(2), rev_tab, rev_tab],
        out_specs=pl.BlockSpec((c, w), lambda s: (n_chunks - 1 - s, 0)),
        scratch_shapes=[pltpu.VMEM((n_heads, HEAD_DIM, HEAD_DIM), F32)],
        compiler_params=_cparams(("arbitrary",)),
        name="ret_bwd",
    )(log_gamma, proj, proj, proj, cos, sin)
    fwd = lambda off: pl.BlockSpec((c, w), lambda s: (s, cb0 + off))
    tab = pl.BlockSpec((c, HEAD_DIM), lambda s: (s, 0))
    return pl.pallas_call(
        functools.partial(_ret_fwd_body, n_heads=n_heads, seq_starts=seq_starts),
        out_shape=jax.ShapeDtypeStruct((n, w), BF16),
        grid=(n_chunks,),
        in_specs=[smem, fwd(0), fwd(1), fwd(2), fwd(3), pl.BlockSpec((c, w), lambda s: (s, 0)), tab, tab],
        out_specs=pl.BlockSpec((c, w), lambda s: (s, 0)),
        scratch_shapes=[pltpu.VMEM((n_heads, HEAD_DIM, HEAD_DIM), F32)],
        compiler_params=_cparams(("arbitrary",)),
        name="ret_fwd",
    )(log_gamma, proj, proj, proj, proj, cross_b, cos, sin)


def _rope_quarter(x, cos, sin_signed, low):
    partner = jnp.where(low, pltpu.roll(x, HEAD_DIM - 32, axis=1), pltpu.roll(x, 32, axis=1))
    return x * cos + partner * sin_signed


def _gqa_prep_body(q_ref, k_ref, v_ref, nw_ref, cos_ref, sin_ref, qo_ref, ko_ref, vto_ref, *, hq, hk):
    cos, sin = cos_ref[...], sin_ref[...]
    low = (lax.broadcasted_iota(jnp.int32, cos.shape, 1) % 64) < 32
    scale = HEAD_DIM ** -0.5 * LOG2E

    def norm_rope(x, w):
        y = x * lax.rsqrt(jnp.mean(x * x, axis=-1, keepdims=True) + EPS) * w
        return _rope_quarter(y, cos, sin, low)

    for h in range(hq):
        sl = slice(h * HEAD_DIM, (h + 1) * HEAD_DIM)
        qo_ref[:, sl] = (norm_rope(q_ref[:, sl], nw_ref[0:1, :]) * scale).astype(qo_ref.dtype)
    for h in range(hk):
        sl = slice(h * HEAD_DIM, (h + 1) * HEAD_DIM)
        ko_ref[:, sl] = norm_rope(k_ref[:, sl], nw_ref[1:2, :]).astype(ko_ref.dtype)
        vto_ref[sl, :] = v_ref[:, sl].T.astype(vto_ref.dtype)


def gqa_prep(proj, qk_norm, cos, sin, hq, hk, tm=256):
    n = proj.shape[0]
    tm = min(tm, n)
    wq, wk = hq * HEAD_DIM, hk * HEAD_DIM
    assert wq % wk == 0
    row = lambda width, blk: pl.BlockSpec((tm, width), lambda i: (i, blk))
    return pl.pallas_call(
        functools.partial(_gqa_prep_body, hq=hq, hk=hk),
        out_shape=[jax.ShapeDtypeStruct((n, wq), BF16), jax.ShapeDtypeStruct((n, wk), BF16),
                   jax.ShapeDtypeStruct((wk, n), BF16)],
        grid=(n // tm,),
        in_specs=[row(wq, 0), row(wk, wq // wk), row(wk, wq // wk + 1),
                  pl.BlockSpec((2, HEAD_DIM), lambda i: (0, 0)), row(HEAD_DIM, 0), row(HEAD_DIM, 0)],
        out_specs=[row(wq, 0), row(wk, 0), pl.BlockSpec((wk, tm), lambda i: (0, i))],
        compiler_params=_cparams(("parallel",)),
        name="gqa_prep",
    )(proj, proj, proj, qk_norm, cos, sin)


def _latent_norm_body(cq_ref, ckv_ref, wq_ref, wkv_ref, qo_ref, kvo_ref):
    def rms(x, w):
        return x * lax.rsqrt(jnp.mean(x * x, axis=-1, keepdims=True) + EPS) * w
    qo_ref[...] = rms(cq_ref[...], wq_ref[...]).astype(qo_ref.dtype)
    kvo_ref[...] = rms(ckv_ref[...], wkv_ref[...]).astype(kvo_ref.dtype)


def latent_norm(proj, col_cq, cq_norm, ckv_norm, tm=512):
    n = proj.shape[0]
    tm = min(tm, n)
    assert col_cq % Q_LORA == 0 and (col_cq + Q_LORA) % KV_LORA == 0
    return pl.pallas_call(
        _latent_norm_body,
        out_shape=[jax.ShapeDtypeStruct((n, Q_LORA), BF16), jax.ShapeDtypeStruct((n, KV_LORA), BF16)],
        grid=(n // tm,),
        in_specs=[pl.BlockSpec((tm, Q_LORA), lambda i: (i, col_cq // Q_LORA)),
                  pl.BlockSpec((tm, KV_LORA), lambda i: (i, (col_cq + Q_LORA) // KV_LORA)),
                  pl.BlockSpec((1, Q_LORA), lambda i: (0, 0)), pl.BlockSpec((1, KV_LORA), lambda i: (0, 0))],
        out_specs=[pl.BlockSpec((tm, Q_LORA), lambda i: (i, 0)), pl.BlockSpec((tm, KV_LORA), lambda i: (i, 0))],
        compiler_params=_cparams(("parallel",)),
        name="latent_norm",
    )(proj, proj, cq_norm.reshape(1, Q_LORA), ckv_norm.reshape(1, KV_LORA))


def _mla_q_body(a_ref, b_ref, cos_ref, sin_ref, qo_ref, *, heads):
    cos, sin = cos_ref[...], sin_ref[...]
    low = (lax.broadcasted_iota(jnp.int32, cos.shape, 1) % 64) < 32
    scale = (QK_NOPE + QK_ROPE) ** -0.5 * LOG2E
    r = jnp.dot(a_ref[...], b_ref[...], preferred_element_type=F32)
    for h in range(heads):
        a = slice(h * MLA_QK_PAD, h * MLA_QK_PAD + QK_NOPE)
        b = slice(h * MLA_QK_PAD + QK_NOPE, (h + 1) * MLA_QK_PAD)
        qo_ref[:, a] = (r[:, a] * scale).astype(qo_ref.dtype)
        qo_ref[:, b] = (_rope_quarter(r[:, b], cos, sin, low) * scale).astype(qo_ref.dtype)


def _mla_kv_body(a_ref, b_ref, kpe_ref, cos_ref, sin_ref, ko_ref, vto_ref, *, heads):
    cos, sin = cos_ref[...], sin_ref[...]
    low = (lax.broadcasted_iota(jnp.int32, cos.shape, 1) % 64) < 32
    k_rope = _rope_quarter(kpe_ref[...], cos, sin, low).astype(ko_ref.dtype)
    r = jnp.dot(a_ref[...], b_ref[...], preferred_element_type=F32)
    for h in range(heads):
        a = slice(h * MLA_QK_PAD, h * MLA_QK_PAD + QK_NOPE)
        b = slice(h * MLA_QK_PAD + QK_NOPE, (h + 1) * MLA_QK_PAD)
        ko_ref[:, a] = r[:, a].astype(ko_ref.dtype)
        ko_ref[:, b] = k_rope
        vto_ref[h * V_D:(h + 1) * V_D, :] = r[:, b].T.astype(vto_ref.dtype)


def mla_up_proj(cqn, ckvn, w_uq, w_ukv, proj, col_kpe, cos, sin, n_heads, tm=512, heads_per_step=4):
    n = cqn.shape[0]
    tm = min(tm, n)
    hs = min(heads_per_step, n_heads)
    assert n_heads % hs == 0 and col_kpe % 128 == 0
    tn = hs * MLA_QK_PAD
    wd = n_heads * MLA_QK_PAD
    grid = (n // tm, n_heads // hs)
    tab = pl.BlockSpec((tm, 128), lambda i, j: (i, 0))
    q = pl.pallas_call(
        functools.partial(_mla_q_body, heads=hs),
        out_shape=jax.ShapeDtypeStruct((n, wd), BF16),
        grid=grid,
        in_specs=[pl.BlockSpec((tm, Q_LORA), lambda i, j: (i, 0)), pl.BlockSpec((Q_LORA, tn), lambda i, j: (0, j)),
                  tab, tab],
        out_specs=pl.BlockSpec((tm, tn), lambda i, j: (i, j)),
        compiler_params=_cparams(("parallel", "parallel")),
        name="mla_q_proj",
    )(cqn, w_uq, cos, sin)
    k, vt = pl.pallas_call(
        functools.partial(_mla_kv_body, heads=hs),
        out_shape=[jax.ShapeDtypeStruct((n, wd), BF16), jax.ShapeDtypeStruct((n_heads * V_D, n), BF16)],
        grid=grid,
        in_specs=[pl.BlockSpec((tm, KV_LORA), lambda i, j: (i, 0)), pl.BlockSpec((KV_LORA, tn), lambda i, j: (0, j)),
                  pl.BlockSpec((tm, 128), lambda i, j: (i, col_kpe // 128)), tab, tab],
        out_specs=[pl.BlockSpec((tm, tn), lambda i, j: (i, j)),
                   pl.BlockSpec((hs * V_D, tm), lambda i, j: (j, i))],
        compiler_params=_cparams(("parallel", "parallel")),
        name="mla_kv_proj",
    )(ckvn, w_ukv, proj, cos, sin)
    return q, k, vt


def _inv_freq(dim):
    return (np.float32(ROPE_THETA) ** (-np.arange(0, dim, 2, dtype=np.float32) / np.float32(dim))).astype(np.float32)


def _angles(pos, dim):
    ang = pos.astype(F32)[:, None] * jnp.asarray(_inv_freq(dim))[None, :]
    return jnp.cos(ang), jnp.sin(ang)


def _rope_tables(seq_lens):
    tmax = max(seq_lens)
    pos = jnp.arange(tmax)
    per_token = lambda t: jnp.concatenate([t[:n] for n in seq_lens], axis=0)
    c, s = _angles(pos, HEAD_DIM)
    ret = (per_token(jnp.concatenate([c, c], -1)), per_token(jnp.concatenate([-s, s], -1)))
    rows = -(-tmax // GRID_W)
    cr, sr = (jnp.repeat(t, GRID_W, axis=0)[:tmax] for t in _angles(jnp.arange(rows), HEAD_DIM // 2))
    cc, sc = (jnp.tile(t, (rows, 1))[:tmax] for t in _angles(jnp.arange(GRID_W), HEAD_DIM // 2))
    axial = (per_token(jnp.concatenate([cr, cr, cc, cc], -1)), per_token(jnp.concatenate([-sr, sr, -sc, sc], -1)))
    c1, s1 = _angles(pos, QK_ROPE)
    one, zero = jnp.ones_like(c1), jnp.zeros_like(s1)
    mla = (per_token(jnp.concatenate([c1, c1, one, one], -1)), per_token(jnp.concatenate([-s1, s1, zero, zero], -1)))
    return ret, axial, mla


def _pad_cols(w, mult):
    pad = (-w.shape[-1]) % mult
    return jnp.pad(w, ((0, 0), (0, pad))) if pad else w


def _mix_even(h, w_in, log_1m_gamma, rel_bias, ret_tab, seqs):
    seq_starts, seq_ends = seqs
    n_heads = w_in.shape[1] // (7 * HEAD_DIM)
    w = n_heads * HEAD_DIM
    proj = matmul(h, w_in.astype(BF16), out_dtype=F32, name="mm_in_even")
    assert all(window // (2 * dil) == WIN_HALF for window, dil in DIL_PATTERNS)
    dils = tuple(dil for _, dil in DIL_PATTERNS)
    oa = window_attention(proj, rel_bias, dils, n_heads, seq_starts, seq_ends)
    log_gamma = jnp.log1p(-jnp.exp(log_1m_gamma.astype(F32)))
    ob = retention(proj, 3 * w, log_gamma, ret_tab[0], ret_tab[1], n_heads, seq_starts, seq_ends)
    return oa, ob


def _mix_odd(h, w_in, qk_norm, cq_norm, ckv_norm, w_uq, w_ukv, ax_tab, mla_tab, groups, tiles):
    in_odd = w_in.shape[1]
    wkv = KV_C * HEAD_DIM
    wq = in_odd - 2 * wkv - Q_LORA - KV_LORA - QK_ROPE
    hq = wq // HEAD_DIM
    proj = matmul(h, _pad_cols(w_in, 512).astype(BF16), out_dtype=F32, name="mm_in_odd")
    qc, kc, vtc = gqa_prep(proj, qk_norm, ax_tab[0], ax_tab[1], hq, KV_C)
    col_cq = wq + 2 * wkv
    cqn, ckvn = latent_norm(proj, col_cq, cq_norm, ckv_norm)
    n_hd = w_uq.shape[1] // (QK_NOPE + QK_ROPE)
    w_uq_p = jnp.pad(w_uq.reshape(Q_LORA, n_hd, QK_NOPE + QK_ROPE),
                     ((0, 0), (0, 0), (0, MLA_QK_PAD - QK_NOPE - QK_ROPE)))
    assert V_D == MLA_QK_PAD - QK_NOPE and w_ukv.shape[1] == n_hd * MLA_QK_PAD
    q_d, k_d, vt_d = mla_up_proj(cqn, ckvn, w_uq_p.reshape(Q_LORA, n_hd * MLA_QK_PAD).astype(BF16),
                                 w_ukv.astype(BF16), proj, col_cq + Q_LORA + KV_LORA, mla_tab[0], mla_tab[1], n_hd)
    ocs, ods = [], []
    for row0, batch, seq_len in groups:
        ocs.append(flash_attention(qc, kc, vtc, n_kv_heads=KV_C, g_heads=hq // KV_C, dqk=HEAD_DIM, dv=HEAD_DIM,
                                   row0=row0, batch=batch, seq_len=seq_len, tq=tiles[0], tk=tiles[1],
                                   nsplit=tiles[2], name="attn_gqa"))
        ods.append(flash_attention(q_d, k_d, vt_d, n_kv_heads=n_hd, g_heads=1, dqk=MLA_QK_PAD, dv=V_D,
                                   row0=row0, batch=batch, seq_len=seq_len, tq=tiles[3], tk=tiles[4],
                                   nsplit=tiles[5], name="attn_mla"))
    return jnp.concatenate(ocs, 0), jnp.concatenate(ods, 0)


def _conv_ffn(h, w_up, conv_w, w_down, x, gate, seqs, ff_mult):
    seq_starts, seq_ends = seqs
    dff = w_down.shape[0]
    dffp = -(-dff // ff_mult) * ff_mult
    pad = dffp - dff
    split_pad = lambda w: jnp.concatenate([jnp.pad(w[:, :dff], ((0, 0), (0, pad))),
                                           jnp.pad(w[:, dff:], ((0, 0), (0, pad)))], axis=1)
    act = ff_up_conv_gate(h, split_pad(w_up).astype(BF16), split_pad(conv_w), dffp, seq_starts, seq_ends)
    w_down_p = jnp.pad(w_down, ((0, pad), (0, 0))).astype(BF16)
    return matmul(act, w_down_p, out_dtype=F32, tn=1024, tk=2816, resid=x, gate=gate, seq_starts=seq_starts,
                  name="mm_ff_down")


def _trunk(x_prompt, x_sample, c_prompt, c_sample, rel_bias, norm_w, w_mod, b_mod, w_in_even,
           ret_log_1m_gamma, w_out_even, w_in_odd, c_qk_norm, mla_cq_norm, mla_ckv_norm, w_uq,
           w_ukv, w_out_odd, w_ff_up, conv_ff, w_ff_down, final_norm_w, *, attn_tiles, ff_mult):
    bp, tp, d = x_prompt.shape
    bs, ts, _ = x_sample.shape
    depth = norm_w.shape[0]
    seq_lens = (tp,) * bp + (ts,) * bs
    seq_starts = tuple(int(v) for v in np.cumsum((0,) + seq_lens[:-1]))
    seq_ends = tuple(int(v) for v in np.cumsum(seq_lens))
    seqs = (seq_starts, seq_ends)
    groups = ((0, bp, tp), (bp * tp, bs, ts))
    n_seq = len(seq_lens)

    x = jnp.concatenate([x_prompt.reshape(bp * tp, d), x_sample.reshape(bs * ts, d)], axis=0)
    c = jnp.concatenate([c_prompt, c_sample], axis=0)
    c_act = jnp.pad(jax.nn.silu(c), ((0, 8 - n_seq), (0, 0))).astype(BF16)
    ret_tab, ax_tab, mla_tab = _rope_tables(seq_lens)

    for layer in range(depth):
        mod = matmul(c_act, w_mod[layer], out_dtype=F32, tm=8, tn=2048, tk=1024,
                     bias=b_mod[layer].reshape(1, -1), name="mm_mod")
        mod6 = mod[:n_seq].reshape(n_seq * 6, 1, d)
        gate_of = lambda ci: mod6.reshape(n_seq, 6, d)[:, ci].reshape(n_seq, 1, d)
        h = norm_mod(x, norm_w[layer, 0], mod6, 1, 0, seq_starts)
        i = layer // 2
        if layer % 2 == 0:
            mix = _mix_even(h, w_in_even[i], ret_log_1m_gamma[i], rel_bias, ret_tab, seqs)
            w_out = w_out_even[i]
        else:
            mix = _mix_odd(h, w_in_odd[i], c_qk_norm[i], mla_cq_norm[i], mla_ckv_norm[i], w_uq[i], w_ukv[i],
                           ax_tab, mla_tab, groups, attn_tiles)
            w_out = w_out_odd[i]
        x = matmul_cat_resid(mix[0], mix[1], w_out.astype(BF16), x, gate_of(2), seq_starts)
        h = norm_mod(x, norm_w[layer, 1], mod6, 4, 3, seq_starts)
        x = _conv_ffn(h, w_ff_up[layer], conv_ff[layer], w_ff_down[layer], x, gate_of(5), seqs, ff_mult)
    y_p, y_s = final_norm(x, final_norm_w, bp * tp)
    return y_p.reshape(bp, tp, d), y_s.reshape(bs, ts, d)


def kernel(x_prompt, x_sample, c_prompt, c_sample, rel_bias, norm_w, w_mod, b_mod, w_in_even, ret_log_1m_gamma, w_out_even, w_in_odd, c_qk_norm, mla_cq_norm, mla_ckv_norm, w_uq, w_ukv, w_out_odd, w_ff_up, conv_ff, w_ff_down, final_norm_w):
    return _trunk(x_prompt, x_sample, c_prompt, c_sample, rel_bias, norm_w, w_mod, b_mod, w_in_even,
                  ret_log_1m_gamma, w_out_even, w_in_odd, c_qk_norm, mla_cq_norm, mla_ckv_norm, w_uq,
                  w_ukv, w_out_odd, w_ff_up, conv_ff, w_ff_down, final_norm_w,
                  attn_tiles=(512, 2048, 8, 2048, 2048, 8), ff_mult=1024)
```

```python
import functools
import math

import numpy as np
import jax
import jax.numpy as jnp
from jax import lax
from jax.experimental import pallas as pl
from jax.experimental.pallas import tpu as pltpu

F32 = jnp.float32
BF16 = jnp.bfloat16

HEAD_DIM = 128
GRID_W = 64
DIL_PATTERNS = ((128, 1), (512, 4), (2048, 16))
RET_CHUNK = 128
KV_C = 4
Q_LORA = 768
KV_LORA = 256
QK_NOPE = 128
QK_ROPE = 64
V_D = 128
MLA_QK_PAD = 256
N_BUCKETS = 32
MAX_DIST = 1024
ROPE_THETA = 10000.0
EPS = 1e-6
NEG = -1e30
LOG2E = math.log2(math.e)
FLASH_HEADROOM = 60.0

WIN_HALF = 64
WIN_TQ = 128
BF16_ROWS = 16
VMEM_LIMIT = 56 * 1024 * 1024


def _cparams(sem):
    return pltpu.CompilerParams(dimension_semantics=sem, vmem_limit_bytes=VMEM_LIMIT)


def _seq_id(row, seq_starts):
    s = 0
    for st in seq_starts[1:]:
        s = s + (row >= st).astype(jnp.int32)
    return s


def _is_any(row, marks):
    r = row == marks[0]
    for m in marks[1:]:
        r = jnp.logical_or(r, row == m)
    return r


def _largest_tile(n, cap):
    best = 128
    for t in range(128, cap + 1, 128):
        if n % t == 0:
            best = t
    return best


def _mm_body(*refs, nk, mode):
    a_ref, b_ref = refs[0], refs[1]
    if mode == "plain":
        extra, (o_ref, acc_ref) = (), refs[2:]
    elif mode == "bias":
        extra, (o_ref, acc_ref) = refs[2:3], refs[3:]
    else:
        extra, (o_ref, acc_ref) = refs[2:4], refs[4:]

    def epilogue(r):
        if mode == "bias":
            r = r + extra[0][...]
        elif mode == "resid":
            r = extra[0][...] + extra[1][0] * r
        o_ref[...] = r.astype(o_ref.dtype)

    def prod():
        return jnp.dot(a_ref[...].astype(BF16), b_ref[...].astype(BF16), preferred_element_type=F32)

    if nk == 1:
        epilogue(prod())
        return
    k = pl.program_id(2)

    @pl.when(k == 0)
    def _():
        acc_ref[...] = prod()

    @pl.when(jnp.logical_and(k > 0, k < nk - 1))
    def _():
        acc_ref[...] += prod()

    @pl.when(k == nk - 1)
    def _():
        epilogue(acc_ref[...] + prod())


def _layer_spec(w, layer, block, index_map):
    if w.ndim == len(block):
        return pl.BlockSpec(block, index_map)
    assert w.ndim == len(block) + 1
    return pl.BlockSpec((None,) + tuple(block), lambda *g: (layer,) + tuple(index_map(*g)))


def matmul(a, b, *, out_dtype, layer=None, tm=1024, tn=512, tk=4096, bias=None, resid=None, gate=None,
           seq_starts=None, name="mm"):
    m, kd = a.shape
    n = b.shape[-1]
    tm, tn, tk = min(tm, m), _largest_tile(n, tn), _largest_tile(kd, tk)
    assert m % tm == 0 and n % tn == 0 and kd % tk == 0 and b.shape[-2] == kd, (a.shape, b.shape, tm, tn, tk)
    nk = kd // tk
    in_specs = [pl.BlockSpec((tm, tk), lambda i, j, k: (i, k)),
                _layer_spec(b, layer, (tk, tn), lambda i, j, k: (k, j))]
    args = [a, b]
    if bias is not None:
        mode = "bias"
        in_specs.append(_layer_spec(bias, layer, (1, tn), lambda i, j, k: (0, j)))
        args.append(bias)
    elif resid is not None:
        mode = "resid"
        in_specs.append(pl.BlockSpec((tm, tn), lambda i, j, k: (i, j)))
        in_specs.append(pl.BlockSpec((1, 1, tn), lambda i, j, k: (_seq_id(i * tm, seq_starts), 0, j)))
        args += [resid, gate]
    else:
        mode = "plain"
    return pl.pallas_call(
        functools.partial(_mm_body, nk=nk, mode=mode),
        out_shape=jax.ShapeDtypeStruct((m, n), out_dtype),
        grid=(m // tm, n // tn, nk),
        in_specs=in_specs,
        out_specs=pl.BlockSpec((tm, tn), lambda i, j, k: (i, j)),
        scratch_shapes=[pltpu.VMEM((tm, tn) if nk > 1 else (8, 128), F32)],
        compiler_params=_cparams(("parallel", "parallel", "arbitrary")),
        name=name,
    )(*args)


def _mm_cat_body(a1_ref, a2_ref, b_ref, x_ref, g_ref, o_ref, *, k1):
    r = jnp.dot(a1_ref[...], b_ref[0:k1], preferred_element_type=F32)
    r = r + jnp.dot(a2_ref[...], b_ref[k1:], preferred_element_type=F32)
    o_ref[...] = (x_ref[...] + g_ref[0] * r).astype(o_ref.dtype)


def matmul_cat_resid(a1, a2, b, layer, resid, gate, seq_starts, tm=1024, tn=512, name="mm_out"):
    m, k1 = a1.shape
    k2 = a2.shape[1]
    n = b.shape[-1]
    tm, tn = min(tm, m), _largest_tile(n, tn)
    assert b.shape[-2] == k1 + k2 and m % tm == 0 and k1 % BF16_ROWS == 0
    return pl.pallas_call(
        functools.partial(_mm_cat_body, k1=k1),
        out_shape=jax.ShapeDtypeStruct((m, n), resid.dtype),
        grid=(m // tm, n // tn),
        in_specs=[pl.BlockSpec((tm, k1), lambda i, j: (i, 0)),
                  pl.BlockSpec((tm, k2), lambda i, j: (i, 0)),
                  _layer_spec(b, layer, (k1 + k2, tn), lambda i, j: (0, j)),
                  pl.BlockSpec((tm, tn), lambda i, j: (i, j)),
                  pl.BlockSpec((1, 1, tn), lambda i, j: (_seq_id(i * tm, seq_starts), 0, j))],
        out_specs=pl.BlockSpec((tm, tn), lambda i, j: (i, j)),
        compiler_params=_cparams(("parallel", "parallel")),
        name=name,
    )(a1, a2, b, resid, gate)


def _ffup_body(a_ref, ap_ref, an_ref, bg_ref, bv_ref, cwg_ref, cwv_ref, o_ref, ext_ref, *,
               tm, seq_starts, seq_ends):
    i = pl.program_id(0)
    halo = BF16_ROWS

    @pl.when(pl.program_id(1) == 0)
    def _():
        ext_ref[0:tm] = a_ref[...]
        hrow = lax.broadcasted_iota(jnp.int32, ap_ref.shape, 0)
        ext_ref[tm:] = jnp.where(hrow < halo // 2, an_ref[...], ap_ref[...])

    first = _is_any(i * tm, seq_starts)
    last = _is_any((i + 1) * tm, seq_ends)
    ext = ext_ref[...]
    rows = tm + halo
    row = lax.broadcasted_iota(jnp.int32, (tm, o_ref.shape[1]), 0)
    kill_prev = jnp.logical_and(row == 0, first)
    kill_next = jnp.logical_and(row == tm - 1, last)

    def conv(b_ref, cw_ref):
        u = jnp.dot(ext, b_ref[...], preferred_element_type=F32)
        u_prev = jnp.where(kill_prev, 0.0, pltpu.roll(u, 1, axis=0)[0:tm])
        u_next = jnp.where(kill_next, 0.0, pltpu.roll(u, rows - 1, axis=0)[0:tm])
        cw = cw_ref[...]
        return u_prev * cw[0:1] + u[0:tm] * cw[1:2] + u_next * cw[2:3]

    g = conv(bg_ref, cwg_ref)
    val = conv(bv_ref, cwv_ref)
    o_ref[...] = (g * jax.nn.sigmoid(g) * val).astype(o_ref.dtype)


def ff_up_conv_gate(h, w_up, conv_w, layer, dff, seq_starts, seq_ends, tm=1024, tn=512):
    n, d = h.shape
    tm, tn = min(tm, n), _largest_tile(dff, tn)
    nj = dff // tn
    hb = tm // BF16_ROWS
    nhb = n // BF16_ROWS
    prev = lambda i: jnp.maximum(i * hb - 1, 0)
    nxt = lambda i: jnp.minimum((i + 1) * hb, nhb - 1)
    return pl.pallas_call(
        functools.partial(_ffup_body, tm=tm, seq_starts=seq_starts, seq_ends=seq_ends),
        out_shape=jax.ShapeDtypeStruct((n, dff), BF16),
        grid=(n // tm, nj),
        in_specs=[pl.BlockSpec((tm, d), lambda i, j: (i, 0)),
                  pl.BlockSpec((BF16_ROWS, d), lambda i, j: (prev(i), 0)),
                  pl.BlockSpec((BF16_ROWS, d), lambda i, j: (nxt(i), 0)),
                  _layer_spec(w_up, layer, (d, tn), lambda i, j: (0, j)),
                  _layer_spec(w_up, layer, (d, tn), lambda i, j: (0, j + nj)),
                  _layer_spec(conv_w, layer, (3, tn), lambda i, j: (0, j)),
                  _layer_spec(conv_w, layer, (3, tn), lambda i, j: (0, j + nj))],
        out_specs=pl.BlockSpec((tm, tn), lambda i, j: (i, j)),
        scratch_shapes=[pltpu.VMEM((tm + BF16_ROWS, d), BF16)],
        compiler_params=_cparams(("parallel", "arbitrary")),
        name="ff_up_conv_gate",
    )(h, h, h, w_up, w_up, conv_w, conv_w)


def _normmod_body(x_ref, w_ref, sc_ref, sh_ref, o_ref):
    x = x_ref[...]
    y = x * lax.rsqrt(jnp.mean(x * x, axis=-1, keepdims=True) + EPS) * w_ref[...]
    o_ref[...] = (y * (1.0 + sc_ref[0]) + sh_ref[0]).astype(o_ref.dtype)


def norm_mod(x, w, mod6, scale_idx, shift_idx, seq_starts, tm=256):
    n, d = x.shape
    tm = min(tm, n)
    return pl.pallas_call(
        _normmod_body,
        out_shape=jax.ShapeDtypeStruct((n, d), BF16),
        grid=(n // tm,),
        in_specs=[pl.BlockSpec((tm, d), lambda i: (i, 0)),
                  pl.BlockSpec((1, d), lambda i: (0, 0)),
                  pl.BlockSpec((1, 1, d), lambda i: (_seq_id(i * tm, seq_starts) * 6 + scale_idx, 0, 0)),
                  pl.BlockSpec((1, 1, d), lambda i: (_seq_id(i * tm, seq_starts) * 6 + shift_idx, 0, 0))],
        out_specs=pl.BlockSpec((tm, d), lambda i: (i, 0)),
        compiler_params=_cparams(("parallel",)),
        name="norm_mod",
    )(x, w.reshape(1, d), mod6, mod6)


def _final_norm_body(x_ref, w_ref, oa_ref, ob_ref, *, na_tiles):
    x = x_ref[...]
    y = x * lax.rsqrt(jnp.mean(x * x, axis=-1, keepdims=True) + EPS) * w_ref[...]
    i = pl.program_id(0)

    @pl.when(i < na_tiles)
    def _():
        oa_ref[...] = y

    @pl.when(i >= na_tiles)
    def _():
        ob_ref[...] = y


def final_norm(x, w, n_first, tm=256):
    n, d = x.shape
    tm = min(tm, n_first, n - n_first)
    assert n_first % tm == 0 and n % tm == 0
    na = n_first // tm
    return pl.pallas_call(
        functools.partial(_final_norm_body, na_tiles=na),
        out_shape=[jax.ShapeDtypeStruct((n_first, d), F32), jax.ShapeDtypeStruct((n - n_first, d), F32)],
        grid=(n // tm,),
        in_specs=[pl.BlockSpec((tm, d), lambda i: (i, 0)), pl.BlockSpec((1, d), lambda i: (0, 0))],
        out_specs=[pl.BlockSpec((tm, d), lambda i: (jnp.minimum(i, na - 1), 0)),
                   pl.BlockSpec((tm, d), lambda i: (jnp.maximum(i - na, 0), 0))],
        compiler_params=_cparams(("arbitrary",)),
        name="final_norm",
    )(x, w.reshape(1, d))


def _flash_body(q_ref, k_ref, vt_ref, o_ref, m_ref, l_ref, acc_ref, pv_ref, mc_ref, *,
                g_heads, dqk, dv, nkv, tq, nsplit):
    j = pl.program_id(3)

    @pl.when(j == 0)
    def _():
        m_ref[...] = jnp.full(m_ref.shape, -jnp.inf, F32)
        l_ref[...] = jnp.zeros(l_ref.shape, F32)
        acc_ref[...] = jnp.zeros(acc_ref.shape, F32)
        mc_ref[...] = jnp.zeros(mc_ref.shape, F32)

    k = k_ref[...]
    vt1 = jnp.concatenate([vt_ref[...], jnp.ones((BF16_ROWS, vt_ref.shape[1]), BF16)], axis=0)
    if g_heads == 1:
        qcat = q_ref[...]
    else:
        qcat = jnp.concatenate([q_ref[:, g * dqk:(g + 1) * dqk] for g in range(g_heads)], axis=0)
    cs = (g_heads * tq) // nsplit
    cols = [slice(c * cs, (c + 1) * cs) for c in range(nsplit)]
    qk = lambda sl: lax.dot_general(k, qcat[sl], (((1,), (1,)), ((), ())), preferred_element_type=F32)

    @pl.when(j > 0)
    def _():
        ps = []
        for sl in cols:
            s = qk(sl)
            mc_ref[:, sl] = jnp.max(s, axis=0, keepdims=True)
            ps.append(jnp.exp2(s - m_ref[:, sl]).astype(BF16))
        for p, sl in zip(ps, cols):
            pv_ref[:, sl] = jnp.dot(vt1, p, preferred_element_type=F32)

    single_pass_ok = jnp.max(mc_ref[...] - m_ref[...]) <= FLASH_HEADROOM

    @pl.when(single_pass_ok)
    def _():
        m_old = m_ref[...]
        m_new = jnp.maximum(m_old, mc_ref[...])
        alpha = jnp.exp2(m_old - m_new)
        l_ref[...] = alpha * (l_ref[...] + pv_ref[dv:dv + 1, :])
        acc_ref[...] = alpha * (acc_ref[...] + pv_ref[0:dv, :])
        m_ref[...] = m_new

    @pl.when(jnp.logical_not(single_pass_ok))
    def _():
        scores = [qk(sl) for sl in cols]
        for s, sl in zip(scores, cols):
            m_prev = m_ref[:, sl]
            m_new = jnp.maximum(m_prev, jnp.max(s, axis=0, keepdims=True))
            alpha = jnp.exp2(m_prev - m_new)
            p = jnp.exp2(s - m_new)
            pv = jnp.dot(vt1, p.astype(BF16), preferred_element_type=F32)
            l_ref[:, sl] = alpha * l_ref[:, sl] + pv[dv:dv + 1]
            acc_ref[:, sl] = alpha * acc_ref[:, sl] + pv[:dv]
            m_ref[:, sl] = m_new

    @pl.when(j == nkv - 1)
    def _():
        o_t = acc_ref[...] / l_ref[...]
        for g in range(g_heads):
            o_ref[:, g * dv:(g + 1) * dv] = o_t[:, g * tq:(g + 1) * tq].T.astype(o_ref.dtype)


def flash_attention(q, k, vt, *, n_kv_heads, g_heads, dqk, dv, row0, batch, seq_len, tq, tk, nsplit, name):
    tq, tk = min(tq, seq_len), min(tk, seq_len)
    nq, nkv = seq_len // tq, seq_len // tk
    qb0, kb0 = row0 // tq, row0 // tk
    ncol = g_heads * tq
    nsplit = min(nsplit, ncol // 128)
    return pl.pallas_call(
        functools.partial(_flash_body, g_heads=g_heads, dqk=dqk, dv=dv, nkv=nkv, tq=tq, nsplit=nsplit),
        out_shape=jax.ShapeDtypeStruct((batch * seq_len, n_kv_heads * g_heads * dv), BF16),
        grid=(batch, n_kv_heads, nq, nkv),
        in_specs=[pl.BlockSpec((tq, g_heads * dqk), lambda b, h, i, j: (qb0 + b * nq + i, h)),
                  pl.BlockSpec((tk, dqk), lambda b, h, i, j: (kb0 + b * nkv + j, h)),
                  pl.BlockSpec((dv, tk), lambda b, h, i, j: (h, kb0 + b * nkv + j))],
        out_specs=pl.BlockSpec((tq, g_heads * dv), lambda b, h, i, j: (b * nq + i, h)),
        scratch_shapes=[pltpu.VMEM((1, ncol), F32), pltpu.VMEM((1, ncol), F32), pltpu.VMEM((dv, ncol), F32),
                        pltpu.VMEM((dv + BF16_ROWS, ncol), F32), pltpu.VMEM((1, ncol), F32)],
        compiler_params=_cparams(("parallel", "parallel", "parallel", "arbitrary")),
        name=name,
    )(q, k, vt)


def _win_body(q_ref, kp_ref, kc_ref, kn_ref, vp_ref, vc_ref, vn_ref, bias_ref, o_ref,
              kext_ref, vext_ref, *scr, tb, tq, dils, seq_starts, seq_ends):
    i = pl.program_id(0)
    first = _is_any(i * tb, seq_starts)
    last = _is_any((i + 1) * tb, seq_ends)
    hmax = WIN_HALF * max(dils)
    for ext_ref, p_ref, c_ref, n_ref in ((kext_ref, kp_ref, kc_ref, kn_ref), (vext_ref, vp_ref, vc_ref, vn_ref)):
        ext_ref[0:hmax] = p_ref[...]
        ext_ref[hmax:hmax + tb] = c_ref[...]
        ext_ref[hmax + tb:] = n_ref[...]
    nkeys = tq + 2 * WIN_HALF
    col = lax.broadcasted_iota(jnp.int32, (tq, nkeys), 1)
    bad_first = jnp.logical_and(col < WIN_HALF, first)
    bad_last = jnp.logical_and(col >= WIN_HALF + tq, last)
    scale = HEAD_DIM ** -0.5
    o_scr, l_scr = scr[:len(dils)], scr[len(dils):]
    for p, dil in enumerate(dils):
        nsub = tb // (tq * dil)
        bias = bias_ref[p]
        for sub in range(nsub):
            for r in range(dil):
                start = sub * tq * dil + r
                kstart = hmax - WIN_HALF * dil + start
                q = (q_ref[pl.ds(start, tq, stride=dil), :] * scale).astype(BF16)
                k = kext_ref[pl.ds(kstart, nkeys, stride=dil), :].astype(BF16)
                v = vext_ref[pl.ds(kstart, nkeys, stride=dil), :].astype(BF16)
                s = lax.dot_general(q, k, (((1,), (1,)), ((), ())), preferred_element_type=F32) + bias
                if sub == 0:
                    s = jnp.where(bad_first, NEG, s)
                if sub == nsub - 1:
                    s = jnp.where(bad_last, NEG, s)
                m = jnp.max(s, axis=-1, keepdims=True)
                e = jnp.exp(s - m)
                den = jnp.sum(e, axis=-1, keepdims=True)
                o = jnp.dot(e.astype(BF16), v, preferred_element_type=F32) / den
                o_scr[p][pl.ds(start, tq, stride=dil), :] = o
                l_scr[p][pl.ds(start, tq, stride=dil), :] = jnp.broadcast_to(m + jnp.log(den), (tq, HEAD_DIM))
    lses = [l[...] for l in l_scr]
    m = functools.reduce(jnp.maximum, lses)
    ws = [jnp.exp(l - m) for l in lses]
    num = functools.reduce(lambda a, b: a + b, [w * o[...] for w, o in zip(ws, o_scr)])
    o_ref[...] = (num / functools.reduce(lambda a, b: a + b, ws)).astype(o_ref.dtype)


def _t5_bucket(rel):
    nb = N_BUCKETS // 2
    max_exact = nb // 2
    n = jnp.abs(rel)
    log_ratio = jnp.log(jnp.maximum(n, 1).astype(F32) / max_exact) / math.log(MAX_DIST / max_exact)
    large = jnp.minimum(max_exact + (log_ratio * (nb - max_exact)).astype(jnp.int32), nb - 1)
    return jnp.where(rel > 0, nb, 0) + jnp.where(n < max_exact, n, large)


def _window_bias(rel_bias, dil, tq):
    kj = jnp.arange(tq + 2 * WIN_HALF)
    rel = kj[None, :] - WIN_HALF - jnp.arange(tq)[:, None]
    bias = rel_bias[_t5_bucket(rel * dil)].astype(F32)
    bias = jnp.where((jnp.abs(rel) <= WIN_HALF)[:, :, None], bias, NEG)
    return bias.transpose(2, 0, 1)


def window_attention(proj, rel_bias, dils, n_heads, seq_starts, seq_ends):
    n = proj.shape[0]
    tq = WIN_TQ
    hmax = WIN_HALF * max(dils)
    tb = tq * max(dils)
    assert all(s % tb == 0 for s in seq_starts + seq_ends) and tb % hmax == 0
    bias = jnp.stack([_window_bias(rel_bias, dil, tq) for dil in dils])
    hb = tb // hmax
    nhb = n // hmax
    prev = lambda i: jnp.maximum(i * hb - 1, 0)
    nxt = lambda i: jnp.minimum((i + 1) * hb, nhb - 1)
    cur_spec = lambda g: pl.BlockSpec((tb, HEAD_DIM), lambda i, h: (i, g * n_heads + h))
    prev_spec = lambda g: pl.BlockSpec((hmax, HEAD_DIM), lambda i, h: (prev(i), g * n_heads + h))
    next_spec = lambda g: pl.BlockSpec((hmax, HEAD_DIM), lambda i, h: (nxt(i), g * n_heads + h))
    np_ = len(dils)
    return pl.pallas_call(
        functools.partial(_win_body, tb=tb, tq=tq, dils=dils, seq_starts=seq_starts, seq_ends=seq_ends),
        out_shape=jax.ShapeDtypeStruct((n, n_heads * HEAD_DIM), BF16),
        grid=(n // tb, n_heads),
        in_specs=[cur_spec(0), prev_spec(1), cur_spec(1), next_spec(1),
                  prev_spec(2), cur_spec(2), next_spec(2),
                  pl.BlockSpec((np_, None, tq, tq + 2 * WIN_HALF), lambda i, h: (0, h, 0, 0))],
        out_specs=pl.BlockSpec((tb, HEAD_DIM), lambda i, h: (i, h)),
        scratch_shapes=[pltpu.VMEM((tb + 2 * hmax, HEAD_DIM), F32)] * 2 + [pltpu.VMEM((tb, HEAD_DIM), F32)] * (2 * np_),
        compiler_params=_cparams(("parallel", "parallel")),
        name="win_attn",
    )(proj, proj, proj, proj, proj, proj, proj, bias)


def _rope_half(x, cos, sin_signed):
    return x * cos + pltpu.roll(x, HEAD_DIM // 2, axis=1) * sin_signed


def _ret_bwd_body(lg_ref, q_ref, k_ref, v_ref, cos_ref, sin_ref, o_ref, r_ref, *,
                  n_heads, n_chunks, seq_ends):
    c = RET_CHUNK
    g = n_chunks - 1 - pl.program_id(0)

    @pl.when(_is_any((g + 1) * c, seq_ends))
    def _():
        r_ref[...] = jnp.zeros(r_ref.shape, F32)

    cos, sin = cos_ref[...], sin_ref[...]
    n = lax.broadcasted_iota(jnp.int32, (c, HEAD_DIM), 0).astype(F32)
    scale = HEAD_DIM ** -0.5
    for h in range(n_heads):
        sl = slice(h * HEAD_DIM, (h + 1) * HEAD_DIM)
        lg = lg_ref[1, h]
        q = _rope_half(q_ref[:, sl], cos, sin) * scale
        k = _rope_half(k_ref[:, sl], cos, sin)
        v = v_ref[:, sl]
        r = r_ref[h]
        qw = (q * jnp.exp(lg * (c - n))).astype(BF16)
        o_ref[:, sl] = jnp.dot(qw, r.astype(BF16), preferred_element_type=F32)
        kw_t = (k * jnp.exp(lg * n)).T.astype(BF16)
        chunk_decay = jnp.exp(jnp.full(r.shape, lg * c, F32))
        r_ref[h] = chunk_decay * r + jnp.dot(kw_t, v.astype(BF16), preferred_element_type=F32)


def _ret_fwd_body(lg_ref, q_ref, k_ref, v_ref, gate_ref, xb_ref, cos_ref, sin_ref, o_ref, r_ref, *,
                  n_heads, seq_starts):
    c = RET_CHUNK
    g = pl.program_id(0)

    @pl.when(_is_any(g * c, seq_starts))
    def _():
        r_ref[...] = jnp.zeros(r_ref.shape, F32)

    cos, sin = cos_ref[...], sin_ref[...]
    n = lax.broadcasted_iota(jnp.int32, (c, HEAD_DIM), 0).astype(F32)
    rel = (lax.broadcasted_iota(jnp.int32, (c, c), 0) - lax.broadcasted_iota(jnp.int32, (c, c), 1)).astype(F32)
    scale = HEAD_DIM ** -0.5
    for h in range(n_heads):
        sl = slice(h * HEAD_DIM, (h + 1) * HEAD_DIM)
        lg_f, lg_b = lg_ref[0, h], lg_ref[1, h]
        q = _rope_half(q_ref[:, sl], cos, sin) * scale
        k = _rope_half(k_ref[:, sl], cos, sin)
        v = v_ref[:, sl].astype(BF16)
        r = r_ref[h]
        decay = jnp.where(rel >= 0, jnp.exp(lg_f * jnp.maximum(rel, 0.0)), jnp.exp(lg_b * jnp.maximum(-rel, 0.0)))
        qk = lax.dot_general(q.astype(BF16), k.astype(BF16), (((1,), (1,)), ((), ())), preferred_element_type=F32)
        o = jnp.dot((qk * decay).astype(BF16), v, preferred_element_type=F32)
        qw = (q * jnp.exp(lg_f * (n + 1.0))).astype(BF16)
        o = o + jnp.dot(qw, r.astype(BF16), preferred_element_type=F32) + xb_ref[:, sl]
        mu = jnp.mean(o, axis=-1, keepdims=True)
        d = o - mu
        var = jnp.mean(d * d, axis=-1, keepdims=True)
        o = d * lax.rsqrt(var + EPS)
        gate = gate_ref[:, sl]
        o_ref[:, sl] = (gate * jax.nn.sigmoid(gate) * o).astype(o_ref.dtype)
        kw_t = (k * jnp.exp(lg_f * (c - 1.0 - n))).T.astype(BF16)
        chunk_decay = jnp.exp(jnp.full(r.shape, lg_f * c, F32))
        r_ref[h] = chunk_decay * r + jnp.dot(kw_t, v, preferred_element_type=F32)


def retention(proj, col0, log_gamma, cos, sin, n_heads, seq_starts, seq_ends):
    n, _ = proj.shape
    w = n_heads * HEAD_DIM
    cb0 = col0 // w
    c = RET_CHUNK
    n_chunks = n // c
    smem = pl.BlockSpec(memory_space=pltpu.SMEM)
    rev = lambda off: pl.BlockSpec((c, w), lambda s: (n_chunks - 1 - s, cb0 + off))
    rev_tab = pl.BlockSpec((c, HEAD_DIM), lambda s: (n_chunks - 1 - s, 0))
    cross_b = pl.pallas_call(
        functools.partial(_ret_bwd_body, n_heads=n_heads, n_chunks=n_chunks, seq_ends=seq_ends),
        out_shape=jax.ShapeDtypeStruct((n, w), F32),
        grid=(n_chunks,),
        in_specs=[smem, rev(0), rev(1), rev(2), rev_tab, rev_tab],
        out_specs=pl.BlockSpec((c, w), lambda s: (n_chunks - 1 - s, 0)),
        scratch_shapes=[pltpu.VMEM((n_heads, HEAD_DIM, HEAD_DIM), F32)],
        compiler_params=_cparams(("arbitrary",)),
        name="ret_bwd",
    )(log_gamma, proj, proj, proj, cos, sin)
    fwd = lambda off: pl.BlockSpec((c, w), lambda s: (s, cb0 + off))
    tab = pl.BlockSpec((c, HEAD_DIM), lambda s: (s, 0))
    return pl.pallas_call(
        functools.partial(_ret_fwd_body, n_heads=n_heads, seq_starts=seq_starts),
        out_shape=jax.ShapeDtypeStruct((n, w), BF16),
        grid=(n_chunks,),
        in_specs=[smem, fwd(0), fwd(1), fwd(2), fwd(3), pl.BlockSpec((c, w), lambda s: (s, 0)), tab, tab],
        out_specs=pl.BlockSpec((c, w), lambda s: (s, 0)),
        scratch_shapes=[pltpu.VMEM((n_heads, HEAD_DIM, HEAD_DIM), F32)],
        compiler_params=_cparams(("arbitrary",)),
        name="ret_fwd",
    )(log_gamma, proj, proj, proj, proj, cross_b, cos, sin)


def _rope_quarter(x, cos, sin_signed, low):
    partner = jnp.where(low, pltpu.roll(x, HEAD_DIM - 32, axis=1), pltpu.roll(x, 32, axis=1))
    return x * cos + partner * sin_signed


def _gqa_prep_body(q_ref, k_ref, v_ref, nw_ref, cos_ref, sin_ref, qo_ref, ko_ref, vto_ref, *, hq, hk):
    cos, sin = cos_ref[...], sin_ref[...]
    low = (lax.broadcasted_iota(jnp.int32, cos.shape, 1) % 64) < 32
    scale = HEAD_DIM ** -0.5 * LOG2E

    def norm_rope(x, w):
        y = x * lax.rsqrt(jnp.mean(x * x, axis=-1, keepdims=True) + EPS) * w
        return _rope_quarter(y, cos, sin, low)

    for h in range(hq):
        sl = slice(h * HEAD_DIM, (h + 1) * HEAD_DIM)
        qo_ref[:, sl] = (norm_rope(q_ref[:, sl], nw_ref[0:1, :]) * scale).astype(qo_ref.dtype)
    for h in range(hk):
        sl = slice(h * HEAD_DIM, (h + 1) * HEAD_DIM)
        ko_ref[:, sl] = norm_rope(k_ref[:, sl], nw_ref[1:2, :]).astype(ko_ref.dtype)
        vto_ref[sl, :] = v_ref[:, sl].T.astype(vto_ref.dtype)


def gqa_prep(proj, qk_norm, cos, sin, hq, hk, tm=256):
    n = proj.shape[0]
    tm = min(tm, n)
    wq, wk = hq * HEAD_DIM, hk * HEAD_DIM
    assert wq % wk == 0
    row = lambda width, blk: pl.BlockSpec((tm, width), lambda i: (i, blk))
    return pl.pallas_call(
        functools.partial(_gqa_prep_body, hq=hq, hk=hk),
        out_shape=[jax.ShapeDtypeStruct((n, wq), BF16), jax.ShapeDtypeStruct((n, wk), BF16),
                   jax.ShapeDtypeStruct((wk, n), BF16)],
        grid=(n // tm,),
        in_specs=[row(wq, 0), row(wk, wq // wk), row(wk, wq // wk + 1),
                  pl.BlockSpec((2, HEAD_DIM), lambda i: (0, 0)), row(HEAD_DIM, 0), row(HEAD_DIM, 0)],
        out_specs=[row(wq, 0), row(wk, 0), pl.BlockSpec((wk, tm), lambda i: (0, i))],
        compiler_params=_cparams(("parallel",)),
        name="gqa_prep",
    )(proj, proj, proj, qk_norm, cos, sin)


def _latent_norm_body(cq_ref, ckv_ref, wq_ref, wkv_ref, qo_ref, kvo_ref):
    def rms(x, w):
        return x * lax.rsqrt(jnp.mean(x * x, axis=-1, keepdims=True) + EPS) * w
    qo_ref[...] = rms(cq_ref[...], wq_ref[...]).astype(qo_ref.dtype)
    kvo_ref[...] = rms(ckv_ref[...], wkv_ref[...]).astype(kvo_ref.dtype)


def latent_norm(proj, col_cq, cq_norm, ckv_norm, tm=512):
    n = proj.shape[0]
    tm = min(tm, n)
    assert col_cq % Q_LORA == 0 and (col_cq + Q_LORA) % KV_LORA == 0
    return pl.pallas_call(
        _latent_norm_body,
        out_shape=[jax.ShapeDtypeStruct((n, Q_LORA), BF16), jax.ShapeDtypeStruct((n, KV_LORA), BF16)],
        grid=(n // tm,),
        in_specs=[pl.BlockSpec((tm, Q_LORA), lambda i: (i, col_cq // Q_LORA)),
                  pl.BlockSpec((tm, KV_LORA), lambda i: (i, (col_cq + Q_LORA) // KV_LORA)),
                  pl.BlockSpec((1, Q_LORA), lambda i: (0, 0)), pl.BlockSpec((1, KV_LORA), lambda i: (0, 0))],
        out_specs=[pl.BlockSpec((tm, Q_LORA), lambda i: (i, 0)), pl.BlockSpec((tm, KV_LORA), lambda i: (i, 0))],
        compiler_params=_cparams(("parallel",)),
        name="latent_norm",
    )(proj, proj, cq_norm.reshape(1, Q_LORA), ckv_norm.reshape(1, KV_LORA))


def _mla_q_body(a_ref, b_ref, cos_ref, sin_ref, qo_ref, *, heads):
    cos, sin = cos_ref[...], sin_ref[...]
    low = (lax.broadcasted_iota(jnp.int32, cos.shape, 1) % 64) < 32
    scale = (QK_NOPE + QK_ROPE) ** -0.5 * LOG2E
    r = jnp.dot(a_ref[...], b_ref[...], preferred_element_type=F32)
    for h in range(heads):
        a = slice(h * MLA_QK_PAD, h * MLA_QK_PAD + QK_NOPE)
        b = slice(h * MLA_QK_PAD + QK_NOPE, (h + 1) * MLA_QK_PAD)
        qo_ref[:, a] = (r[:, a] * scale).astype(qo_ref.dtype)
        qo_ref[:, b] = (_rope_quarter(r[:, b], cos, sin, low) * scale).astype(qo_ref.dtype)


def _mla_kv_body(a_ref, b_ref, kpe_ref, cos_ref, sin_ref, ko_ref, vto_ref, *, heads):
    cos, sin = cos_ref[...], sin_ref[...]
    low = (lax.broadcasted_iota(jnp.int32, cos.shape, 1) % 64) < 32
    k_rope = _rope_quarter(kpe_ref[...], cos, sin, low).astype(ko_ref.dtype)
    r = jnp.dot(a_ref[...], b_ref[...], preferred_element_type=F32)
    for h in range(heads):
        a = slice(h * MLA_QK_PAD, h * MLA_QK_PAD + QK_NOPE)
        b = slice(h * MLA_QK_PAD + QK_NOPE, (h + 1) * MLA_QK_PAD)
        ko_ref[:, a] = r[:, a].astype(ko_ref.dtype)
        ko_ref[:, b] = k_rope
        vto_ref[h * V_D:(h + 1) * V_D, :] = r[:, b].T.astype(vto_ref.dtype)


def mla_up_proj(cqn, ckvn, w_uq, w_ukv, layer, proj, col_kpe, cos, sin, n_heads, tm=512, heads_per_step=4):
    n = cqn.shape[0]
    tm = min(tm, n)
    hs = min(heads_per_step, n_heads)
    assert n_heads % hs == 0 and col_kpe % 128 == 0
    tn = hs * MLA_QK_PAD
    wd = n_heads * MLA_QK_PAD
    grid = (n // tm, n_heads // hs)
    tab = pl.BlockSpec((tm, 128), lambda i, j: (i, 0))
    q = pl.pallas_call(
        functools.partial(_mla_q_body, heads=hs),
        out_shape=jax.ShapeDtypeStruct((n, wd), BF16),
        grid=grid,
        in_specs=[pl.BlockSpec((tm, Q_LORA), lambda i, j: (i, 0)),
                  _layer_spec(w_uq, layer, (Q_LORA, tn), lambda i, j: (0, j)),
                  tab, tab],
        out_specs=pl.BlockSpec((tm, tn), lambda i, j: (i, j)),
        compiler_params=_cparams(("parallel", "parallel")),
        name="mla_q_proj",
    )(cqn, w_uq, cos, sin)
    k, vt = pl.pallas_call(
        functools.partial(_mla_kv_body, heads=hs),
        out_shape=[jax.ShapeDtypeStruct((n, wd), BF16), jax.ShapeDtypeStruct((n_heads * V_D, n), BF16)],
        grid=grid,
        in_specs=[pl.BlockSpec((tm, KV_LORA), lambda i, j: (i, 0)),
                  _layer_spec(w_ukv, layer, (KV_LORA, tn), lambda i, j: (0, j)),
                  pl.BlockSpec((tm, 128), lambda i, j: (i, col_kpe // 128)), tab, tab],
        out_specs=[pl.BlockSpec((tm, tn), lambda i, j: (i, j)),
                   pl.BlockSpec((hs * V_D, tm), lambda i, j: (j, i))],
        compiler_params=_cparams(("parallel", "parallel")),
        name="mla_kv_proj",
    )(ckvn, w_ukv, proj, cos, sin)
    return q, k, vt


def _inv_freq(dim):
    return (np.float32(ROPE_THETA) ** (-np.arange(0, dim, 2, dtype=np.float32) / np.float32(dim))).astype(np.float32)


def _angles(pos, dim):
    ang = pos.astype(F32)[:, None] * jnp.asarray(_inv_freq(dim))[None, :]
    return jnp.cos(ang), jnp.sin(ang)


def _rope_tables(seq_lens):
    tmax = max(seq_lens)
    pos = jnp.arange(tmax)
    per_token = lambda t: jnp.concatenate([t[:n] for n in seq_lens], axis=0)
    c, s = _angles(pos, HEAD_DIM)
    ret = (per_token(jnp.concatenate([c, c], -1)), per_token(jnp.concatenate([-s, s], -1)))
    rows = -(-tmax // GRID_W)
    cr, sr = (jnp.repeat(t, GRID_W, axis=0)[:tmax] for t in _angles(jnp.arange(rows), HEAD_DIM // 2))
    cc, sc = (jnp.tile(t, (rows, 1))[:tmax] for t in _angles(jnp.arange(GRID_W), HEAD_DIM // 2))
    axial = (per_token(jnp.concatenate([cr, cr, cc, cc], -1)), per_token(jnp.concatenate([-sr, sr, -sc, sc], -1)))
    c1, s1 = _angles(pos, QK_ROPE)
    one, zero = jnp.ones_like(c1), jnp.zeros_like(s1)
    mla = (per_token(jnp.concatenate([c1, c1, one, one], -1)), per_token(jnp.concatenate([-s1, s1, zero, zero], -1)))
    return ret, axial, mla


def _pad_last(w, width):
    return jnp.pad(w, ((0, 0),) * (w.ndim - 1) + ((0, width - w.shape[-1]),))


def _mix_even(h, w_in, i, log_1m_gamma, rel_bias, ret_tab, seqs):
    seq_starts, seq_ends = seqs
    n_heads = w_in.shape[-1] // (7 * HEAD_DIM)
    w = n_heads * HEAD_DIM
    proj = matmul(h, w_in, layer=i, out_dtype=F32, name="mm_in_even")
    assert all(window // (2 * dil) == WIN_HALF for window, dil in DIL_PATTERNS)
    dils = tuple(dil for _, dil in DIL_PATTERNS)
    oa = window_attention(proj, rel_bias, dils, n_heads, seq_starts, seq_ends)
    log_gamma = jnp.log1p(-jnp.exp(log_1m_gamma.astype(F32)))
    ob = retention(proj, 3 * w, log_gamma, ret_tab[0], ret_tab[1], n_heads, seq_starts, seq_ends)
    return oa, ob


def _mix_odd(h, w_in, in_odd, i, qk_norm, cq_norm, ckv_norm, w_uq, w_ukv, ax_tab, mla_tab, groups, tiles):
    wkv = KV_C * HEAD_DIM
    wq = in_odd - 2 * wkv - Q_LORA - KV_LORA - QK_ROPE
    hq = wq // HEAD_DIM
    proj = matmul(h, w_in, layer=i, out_dtype=F32, name="mm_in_odd")
    qc, kc, vtc = gqa_prep(proj, qk_norm, ax_tab[0], ax_tab[1], hq, KV_C)
    col_cq = wq + 2 * wkv
    cqn, ckvn = latent_norm(proj, col_cq, cq_norm, ckv_norm)
    n_hd = w_uq.shape[-1] // MLA_QK_PAD
    assert V_D == MLA_QK_PAD - QK_NOPE and w_ukv.shape[-1] == n_hd * MLA_QK_PAD
    q_d, k_d, vt_d = mla_up_proj(cqn, ckvn, w_uq, w_ukv, i, proj, col_cq + Q_LORA + KV_LORA,
                                 mla_tab[0], mla_tab[1], n_hd)
    ocs, ods = [], []
    for row0, batch, seq_len in groups:
        ocs.append(flash_attention(qc, kc, vtc, n_kv_heads=KV_C, g_heads=hq // KV_C, dqk=HEAD_DIM, dv=HEAD_DIM,
                                   row0=row0, batch=batch, seq_len=seq_len, tq=tiles[0], tk=tiles[1],
                                   nsplit=tiles[2], name="attn_gqa"))
        ods.append(flash_attention(q_d, k_d, vt_d, n_kv_heads=n_hd, g_heads=1, dqk=MLA_QK_PAD, dv=V_D,
                                   row0=row0, batch=batch, seq_len=seq_len, tq=tiles[3], tk=tiles[4],
                                   nsplit=tiles[5], name="attn_mla"))
    return jnp.concatenate(ocs, 0), jnp.concatenate(ods, 0)


def _conv_ffn(h, w_up, conv_w, w_down, layer, x, gate, seqs):
    seq_starts, seq_ends = seqs
    dffp = w_down.shape[-2]
    act = ff_up_conv_gate(h, w_up, conv_w, layer, dffp, seq_starts, seq_ends)
    return matmul(act, w_down, layer=layer, out_dtype=F32, tn=1024, tk=2816, resid=x, gate=gate,
                  seq_starts=seq_starts, name="mm_ff_down")


def _trunk(x_prompt, x_sample, c_prompt, c_sample, rel_bias, norm_w, w_mod, b_mod, w_in_even,
           ret_log_1m_gamma, w_out_even, w_in_odd, c_qk_norm, mla_cq_norm, mla_ckv_norm, w_uq,
           w_ukv, w_out_odd, w_ff_up, conv_ff, w_ff_down, final_norm_w, *, attn_tiles, ff_mult):
    bp, tp, d = x_prompt.shape
    bs, ts, _ = x_sample.shape
    depth = norm_w.shape[0]
    seq_lens = (tp,) * bp + (ts,) * bs
    seq_starts = tuple(int(v) for v in np.cumsum((0,) + seq_lens[:-1]))
    seq_ends = tuple(int(v) for v in np.cumsum(seq_lens))
    seqs = (seq_starts, seq_ends)
    groups = ((0, bp, tp), (bp * tp, bs, ts))
    n_seq = len(seq_lens)

    x = jnp.concatenate([x_prompt.reshape(bp * tp, d), x_sample.reshape(bs * ts, d)], axis=0)
    c = jnp.concatenate([c_prompt, c_sample], axis=0)
    c_act = jnp.pad(jax.nn.silu(c), ((0, 8 - n_seq), (0, 0))).astype(BF16)
    ret_tab, ax_tab, mla_tab = _rope_tables(seq_lens)

    in_odd = w_in_odd.shape[-1]
    w_in_even_b = w_in_even.astype(BF16)
    w_in_odd_b = _pad_last(w_in_odd, -(-in_odd // 512) * 512).astype(BF16)
    w_out_b = (w_out_even.astype(BF16), w_out_odd.astype(BF16))
    n_hd = w_uq.shape[-1] // (QK_NOPE + QK_ROPE)
    w_uq_b = _pad_last(w_uq.reshape(-1, Q_LORA, n_hd, QK_NOPE + QK_ROPE), MLA_QK_PAD)
    w_uq_b = w_uq_b.reshape(-1, Q_LORA, n_hd * MLA_QK_PAD).astype(BF16)
    w_ukv_b = w_ukv.astype(BF16)
    dff = w_ff_down.shape[-2]
    dffp = -(-dff // ff_mult) * ff_mult
    split_pad = lambda w: jnp.concatenate([_pad_last(w[..., :dff], dffp), _pad_last(w[..., dff:], dffp)], axis=-1)
    w_up_b = split_pad(w_ff_up).astype(BF16)
    conv_p = split_pad(conv_ff)
    w_down_b = jnp.pad(w_ff_down, ((0, 0), (0, dffp - dff), (0, 0))).astype(BF16)
    b_mod3 = b_mod.reshape(depth, 1, -1)

    for layer in range(depth):
        mod = matmul(c_act, w_mod, layer=layer, out_dtype=F32, tm=8, tn=2048, tk=1024, bias=b_mod3, name="mm_mod")
        mod6 = mod[:n_seq].reshape(n_seq * 6, 1, d)
        gate_of = lambda ci: mod6.reshape(n_seq, 6, d)[:, ci].reshape(n_seq, 1, d)
        h = norm_mod(x, norm_w[layer, 0], mod6, 1, 0, seq_starts)
        i = layer // 2
        if layer % 2 == 0:
            mix = _mix_even(h, w_in_even_b, i, ret_log_1m_gamma[i], rel_bias, ret_tab, seqs)
        else:
            mix = _mix_odd(h, w_in_odd_b, in_odd, i, c_qk_norm[i], mla_cq_norm[i], mla_ckv_norm[i], w_uq_b, w_ukv_b,
                           ax_tab, mla_tab, groups, attn_tiles)
        x = matmul_cat_resid(mix[0], mix[1], w_out_b[layer % 2], i, x, gate_of(2), seq_starts)
        h = norm_mod(x, norm_w[layer, 1], mod6, 4, 3, seq_starts)
        x = _conv_ffn(h, w_up_b, conv_p, w_down_b, layer, x, gate_of(5), seqs)
    y_p, y_s = final_norm(x, final_norm_w, bp * tp)
    return y_p.reshape(bp, tp, d), y_s.reshape(bs, ts, d)


def kernel(x_prompt, x_sample, c_prompt, c_sample, rel_bias, norm_w, w_mod, b_mod, w_in_even, ret_log_1m_gamma, w_out_even, w_in_odd, c_qk_norm, mla_cq_norm, mla_ckv_norm, w_uq, w_ukv, w_out_odd, w_ff_up, conv_ff, w_ff_down, final_norm_w):
    return _trunk(x_prompt, x_sample, c_prompt, c_sample, rel_bias, norm_w, w_mod, b_mod, w_in_even,
                  ret_log_1m_gamma, w_out_even, w_in_odd, c_qk_norm, mla_cq_norm, mla_ckv_norm, w_uq,
                  w_ukv, w_out_odd, w_ff_up, conv_ff, w_ff_down, final_norm_w,
                  attn_tiles=(512, 2048, 8, 2048, 2048, 8), ff_mult=1024)
```

```python
import functools
import math

import numpy as np
import jax
import jax.numpy as jnp
from jax import lax
from jax.experimental import pallas as pl
from jax.experimental.pallas import tpu as pltpu

F32 = jnp.float32
BF16 = jnp.bfloat16

HEAD_DIM = 128
GRID_W = 64
DIL_PATTERNS = ((128, 1), (512, 4), (2048, 16))
RET_CHUNK = 128
KV_C = 4
Q_LORA = 768
KV_LORA = 256
QK_NOPE = 128
QK_ROPE = 64
V_D = 128
MLA_QK_PAD = 256
N_BUCKETS = 32
MAX_DIST = 1024
ROPE_THETA = 10000.0
EPS = 1e-6
NEG = -1e30
LOG2E = math.log2(math.e)
FLASH_HEADROOM = 60.0
FLASH_SEED_KEYS = 256

WIN_HALF = 64
WIN_TQ = 128
BF16_ROWS = 16
VMEM_LIMIT = 56 * 1024 * 1024


def _cparams(sem):
    return pltpu.CompilerParams(dimension_semantics=sem, vmem_limit_bytes=VMEM_LIMIT)


def _seq_id(row, seq_starts):
    s = 0
    for st in seq_starts[1:]:
        s = s + (row >= st).astype(jnp.int32)
    return s


def _is_any(row, marks):
    r = row == marks[0]
    for m in marks[1:]:
        r = jnp.logical_or(r, row == m)
    return r


def _largest_tile(n, cap):
    best = 128
    for t in range(128, cap + 1, 128):
        if n % t == 0:
            best = t
    return best


def _mm_body(*refs, nk, mode):
    a_ref, b_ref = refs[0], refs[1]
    if mode == "plain":
        extra, (o_ref, acc_ref) = (), refs[2:]
    elif mode == "bias":
        extra, (o_ref, acc_ref) = refs[2:3], refs[3:]
    else:
        extra, (o_ref, acc_ref) = refs[2:4], refs[4:]

    def epilogue(r):
        if mode == "bias":
            r = r + extra[0][...]
        elif mode == "resid":
            r = extra[0][...] + extra[1][0] * r
        o_ref[...] = r.astype(o_ref.dtype)

    def prod():
        return jnp.dot(a_ref[...].astype(BF16), b_ref[...].astype(BF16), preferred_element_type=F32)

    if nk == 1:
        epilogue(prod())
        return
    k = pl.program_id(2)

    @pl.when(k == 0)
    def _():
        acc_ref[...] = prod()

    @pl.when(jnp.logical_and(k > 0, k < nk - 1))
    def _():
        acc_ref[...] += prod()

    @pl.when(k == nk - 1)
    def _():
        epilogue(acc_ref[...] + prod())


def _layer_spec(w, layer, block, index_map):
    if w.ndim == len(block):
        return pl.BlockSpec(block, index_map)
    assert w.ndim == len(block) + 1
    return pl.BlockSpec((None,) + tuple(block), lambda *g: (layer,) + tuple(index_map(*g)))


def matmul(a, b, *, out_dtype, layer=None, tm=1024, tn=512, tk=4096, bias=None, resid=None, gate=None,
           seq_starts=None, name="mm"):
    m, kd = a.shape
    n = b.shape[-1]
    tm, tn, tk = min(tm, m), _largest_tile(n, tn), _largest_tile(kd, tk)
    assert m % tm == 0 and n % tn == 0 and kd % tk == 0 and b.shape[-2] == kd, (a.shape, b.shape, tm, tn, tk)
    nk = kd // tk
    in_specs = [pl.BlockSpec((tm, tk), lambda i, j, k: (i, k)),
                _layer_spec(b, layer, (tk, tn), lambda i, j, k: (k, j))]
    args = [a, b]
    if bias is not None:
        mode = "bias"
        in_specs.append(_layer_spec(bias, layer, (1, tn), lambda i, j, k: (0, j)))
        args.append(bias)
    elif resid is not None:
        mode = "resid"
        in_specs.append(pl.BlockSpec((tm, tn), lambda i, j, k: (i, j)))
        in_specs.append(pl.BlockSpec((1, 1, tn), lambda i, j, k: (_seq_id(i * tm, seq_starts), 0, j)))
        args += [resid, gate]
    else:
        mode = "plain"
    return pl.pallas_call(
        functools.partial(_mm_body, nk=nk, mode=mode),
        out_shape=jax.ShapeDtypeStruct((m, n), out_dtype),
        grid=(m // tm, n // tn, nk),
        in_specs=in_specs,
        out_specs=pl.BlockSpec((tm, tn), lambda i, j, k: (i, j)),
        scratch_shapes=[pltpu.VMEM((tm, tn) if nk > 1 else (8, 128), F32)],
        compiler_params=_cparams(("parallel", "parallel", "arbitrary")),
        name=name,
    )(*args)


def _mm_cat_body(a1_ref, a2_ref, b_ref, x_ref, g_ref, o_ref, *, k1):
    r = jnp.dot(a1_ref[...], b_ref[0:k1], preferred_element_type=F32)
    r = r + jnp.dot(a2_ref[...], b_ref[k1:], preferred_element_type=F32)
    o_ref[...] = (x_ref[...] + g_ref[0] * r).astype(o_ref.dtype)


def matmul_cat_resid(a1, a2, b, layer, resid, gate, seq_starts, tm=1024, tn=512, name="mm_out"):
    m, k1 = a1.shape
    k2 = a2.shape[1]
    n = b.shape[-1]
    tm, tn = min(tm, m), _largest_tile(n, tn)
    assert b.shape[-2] == k1 + k2 and m % tm == 0 and k1 % BF16_ROWS == 0
    return pl.pallas_call(
        functools.partial(_mm_cat_body, k1=k1),
        out_shape=jax.ShapeDtypeStruct((m, n), resid.dtype),
        grid=(m // tm, n // tn),
        in_specs=[pl.BlockSpec((tm, k1), lambda i, j: (i, 0)),
                  pl.BlockSpec((tm, k2), lambda i, j: (i, 0)),
                  _layer_spec(b, layer, (k1 + k2, tn), lambda i, j: (0, j)),
                  pl.BlockSpec((tm, tn), lambda i, j: (i, j)),
                  pl.BlockSpec((1, 1, tn), lambda i, j: (_seq_id(i * tm, seq_starts), 0, j))],
        out_specs=pl.BlockSpec((tm, tn), lambda i, j: (i, j)),
        compiler_params=_cparams(("parallel", "parallel")),
        name=name,
    )(a1, a2, b, resid, gate)


def _ffup_body(a_ref, ap_ref, an_ref, bg_ref, bv_ref, cwg_ref, cwv_ref, o_ref, ext_ref, *,
               tm, seq_starts, seq_ends):
    i = pl.program_id(0)
    halo = BF16_ROWS

    @pl.when(pl.program_id(1) == 0)
    def _():
        ext_ref[0:tm] = a_ref[...]
        hrow = lax.broadcasted_iota(jnp.int32, ap_ref.shape, 0)
        ext_ref[tm:] = jnp.where(hrow < halo // 2, an_ref[...], ap_ref[...])

    first = _is_any(i * tm, seq_starts)
    last = _is_any((i + 1) * tm, seq_ends)
    ext = ext_ref[...]
    rows = tm + halo
    row = lax.broadcasted_iota(jnp.int32, (tm, o_ref.shape[1]), 0)
    kill_prev = jnp.logical_and(row == 0, first)
    kill_next = jnp.logical_and(row == tm - 1, last)

    def conv(b_ref, cw_ref):
        u = jnp.dot(ext, b_ref[...], preferred_element_type=F32)
        u_prev = jnp.where(kill_prev, 0.0, pltpu.roll(u, 1, axis=0)[0:tm])
        u_next = jnp.where(kill_next, 0.0, pltpu.roll(u, rows - 1, axis=0)[0:tm])
        cw = cw_ref[...]
        return u_prev * cw[0:1] + u[0:tm] * cw[1:2] + u_next * cw[2:3]

    g = conv(bg_ref, cwg_ref)
    val = conv(bv_ref, cwv_ref)
    o_ref[...] = (g * jax.nn.sigmoid(g) * val).astype(o_ref.dtype)


def ff_up_conv_gate(h, w_up, conv_w, layer, dff, seq_starts, seq_ends, tm=1024, tn=512):
    n, d = h.shape
    tm, tn = min(tm, n), _largest_tile(dff, tn)
    nj = dff // tn
    hb = tm // BF16_ROWS
    nhb = n // BF16_ROWS
    prev = lambda i: jnp.maximum(i * hb - 1, 0)
    nxt = lambda i: jnp.minimum((i + 1) * hb, nhb - 1)
    return pl.pallas_call(
        functools.partial(_ffup_body, tm=tm, seq_starts=seq_starts, seq_ends=seq_ends),
        out_shape=jax.ShapeDtypeStruct((n, dff), BF16),
        grid=(n // tm, nj),
        in_specs=[pl.BlockSpec((tm, d), lambda i, j: (i, 0)),
                  pl.BlockSpec((BF16_ROWS, d), lambda i, j: (prev(i), 0)),
                  pl.BlockSpec((BF16_ROWS, d), lambda i, j: (nxt(i), 0)),
                  _layer_spec(w_up, layer, (d, tn), lambda i, j: (0, j)),
                  _layer_spec(w_up, layer, (d, tn), lambda i, j: (0, j + nj)),
                  _layer_spec(conv_w, layer, (3, tn), lambda i, j: (0, j)),
                  _layer_spec(conv_w, layer, (3, tn), lambda i, j: (0, j + nj))],
        out_specs=pl.BlockSpec((tm, tn), lambda i, j: (i, j)),
        scratch_shapes=[pltpu.VMEM((tm + BF16_ROWS, d), BF16)],
        compiler_params=_cparams(("parallel", "arbitrary")),
        name="ff_up_conv_gate",
    )(h, h, h, w_up, w_up, conv_w, conv_w)


def _normmod_body(x_ref, w_ref, sc_ref, sh_ref, o_ref):
    x = x_ref[...]
    y = x * lax.rsqrt(jnp.mean(x * x, axis=-1, keepdims=True) + EPS) * w_ref[...]
    o_ref[...] = (y * (1.0 + sc_ref[0]) + sh_ref[0]).astype(o_ref.dtype)


def norm_mod(x, w, mod6, scale_idx, shift_idx, seq_starts, tm=256):
    n, d = x.shape
    tm = min(tm, n)
    return pl.pallas_call(
        _normmod_body,
        out_shape=jax.ShapeDtypeStruct((n, d), BF16),
        grid=(n // tm,),
        in_specs=[pl.BlockSpec((tm, d), lambda i: (i, 0)),
                  pl.BlockSpec((1, d), lambda i: (0, 0)),
                  pl.BlockSpec((1, 1, d), lambda i: (_seq_id(i * tm, seq_starts) * 6 + scale_idx, 0, 0)),
                  pl.BlockSpec((1, 1, d), lambda i: (_seq_id(i * tm, seq_starts) * 6 + shift_idx, 0, 0))],
        out_specs=pl.BlockSpec((tm, d), lambda i: (i, 0)),
        compiler_params=_cparams(("parallel",)),
        name="norm_mod",
    )(x, w.reshape(1, d), mod6, mod6)


def _final_norm_body(x_ref, w_ref, oa_ref, ob_ref, *, na_tiles):
    x = x_ref[...]
    y = x * lax.rsqrt(jnp.mean(x * x, axis=-1, keepdims=True) + EPS) * w_ref[...]
    i = pl.program_id(0)

    @pl.when(i < na_tiles)
    def _():
        oa_ref[...] = y

    @pl.when(i >= na_tiles)
    def _():
        ob_ref[...] = y


def final_norm(x, w, n_first, tm=256):
    n, d = x.shape
    tm = min(tm, n_first, n - n_first)
    assert n_first % tm == 0 and n % tm == 0
    na = n_first // tm
    return pl.pallas_call(
        functools.partial(_final_norm_body, na_tiles=na),
        out_shape=[jax.ShapeDtypeStruct((n_first, d), F32), jax.ShapeDtypeStruct((n - n_first, d), F32)],
        grid=(n // tm,),
        in_specs=[pl.BlockSpec((tm, d), lambda i: (i, 0)), pl.BlockSpec((1, d), lambda i: (0, 0))],
        out_specs=[pl.BlockSpec((tm, d), lambda i: (jnp.minimum(i, na - 1), 0)),
                   pl.BlockSpec((tm, d), lambda i: (jnp.maximum(i - na, 0), 0))],
        compiler_params=_cparams(("arbitrary",)),
        name="final_norm",
    )(x, w.reshape(1, d))


def _flash_body(q_ref, k_ref, vt_ref, o_ref, m_ref, l_ref, acc_ref, pv_ref, mc_ref, *,
                g_heads, dqk, dv, nkv, tq, nsplit):
    j = pl.program_id(3)
    cs = (g_heads * tq) // nsplit
    cols = [slice(c * cs, (c + 1) * cs) for c in range(nsplit)]
    if g_heads == 1:
        qcat = q_ref[...]
    else:
        qcat = jnp.concatenate([q_ref[:, g * dqk:(g + 1) * dqk] for g in range(g_heads)], axis=0)

    @pl.when(j == 0)
    def _():
        k_head = k_ref[0:FLASH_SEED_KEYS, :]
        for sl in cols:
            s0 = lax.dot_general(k_head, qcat[sl], (((1,), (1,)), ((), ())), preferred_element_type=F32)
            m_ref[:, sl] = jnp.max(s0, axis=0, keepdims=True)
        l_ref[...] = jnp.zeros(l_ref.shape, F32)
        acc_ref[...] = jnp.zeros(acc_ref.shape, F32)

    k = k_ref[...]
    vt1 = jnp.concatenate([vt_ref[...], jnp.ones((BF16_ROWS, vt_ref.shape[1]), BF16)], axis=0)
    qk = lambda sl: lax.dot_general(k, qcat[sl], (((1,), (1,)), ((), ())), preferred_element_type=F32)

    @pl.when(j < nkv)
    def _():
        ps = []
        for sl in cols:
            s = qk(sl)
            mc_ref[:, sl] = jnp.max(s, axis=0, keepdims=True)
            ps.append(jnp.exp2(s - m_ref[:, sl]).astype(BF16))
        for p, sl in zip(ps, cols):
            pv_ref[:, sl] = jnp.dot(vt1, p, preferred_element_type=F32)

    single_pass_ok = jnp.max(mc_ref[...] - m_ref[...]) <= FLASH_HEADROOM

    @pl.when(single_pass_ok)
    def _():
        m_old = m_ref[...]
        m_new = jnp.maximum(m_old, mc_ref[...])
        alpha = jnp.exp2(m_old - m_new)
        l_ref[...] = alpha * (l_ref[...] + pv_ref[dv:dv + 1, :])
        acc_ref[...] = alpha * (acc_ref[...] + pv_ref[0:dv, :])
        m_ref[...] = m_new

    @pl.when(jnp.logical_not(single_pass_ok))
    def _():
        scores = [qk(sl) for sl in cols]
        for s, sl in zip(scores, cols):
            m_prev = m_ref[:, sl]
            m_new = jnp.maximum(m_prev, jnp.max(s, axis=0, keepdims=True))
            alpha = jnp.exp2(m_prev - m_new)
            p = jnp.exp2(s - m_new)
            pv = jnp.dot(vt1, p.astype(BF16), preferred_element_type=F32)
            l_ref[:, sl] = alpha * l_ref[:, sl] + pv[dv:dv + 1]
            acc_ref[:, sl] = alpha * acc_ref[:, sl] + pv[:dv]
            m_ref[:, sl] = m_new

    @pl.when(j == nkv - 1)
    def _():
        o_t = acc_ref[...] / l_ref[...]
        for g in range(g_heads):
            o_ref[:, g * dv:(g + 1) * dv] = o_t[:, g * tq:(g + 1) * tq].T.astype(o_ref.dtype)


def flash_attention(q, k, vt, *, n_kv_heads, g_heads, dqk, dv, row0, batch, seq_len, tq, tk, nsplit, name):
    tq, tk = min(tq, seq_len), min(tk, seq_len)
    nq, nkv = seq_len // tq, seq_len // tk
    qb0, kb0 = row0 // tq, row0 // tk
    ncol = g_heads * tq
    nsplit = min(nsplit, ncol // 128)
    return pl.pallas_call(
        functools.partial(_flash_body, g_heads=g_heads, dqk=dqk, dv=dv, nkv=nkv, tq=tq, nsplit=nsplit),
        out_shape=jax.ShapeDtypeStruct((batch * seq_len, n_kv_heads * g_heads * dv), BF16),
        grid=(batch, n_kv_heads, nq, nkv),
        in_specs=[pl.BlockSpec((tq, g_heads * dqk), lambda b, h, i, j: (qb0 + b * nq + i, h)),
                  pl.BlockSpec((tk, dqk), lambda b, h, i, j: (kb0 + b * nkv + j, h)),
                  pl.BlockSpec((dv, tk), lambda b, h, i, j: (h, kb0 + b * nkv + j))],
        out_specs=pl.BlockSpec((tq, g_heads * dv), lambda b, h, i, j: (b * nq + i, h)),
        scratch_shapes=[pltpu.VMEM((1, ncol), F32), pltpu.VMEM((1, ncol), F32), pltpu.VMEM((dv, ncol), F32),
                        pltpu.VMEM((dv + BF16_ROWS, ncol), F32), pltpu.VMEM((1, ncol), F32)],
        compiler_params=_cparams(("parallel", "parallel", "parallel", "arbitrary")),
        name=name,
    )(q, k, vt)


def _win_body(q_ref, kp_ref, kc_ref, kn_ref, vp_ref, vc_ref, vn_ref, bias_ref, o_ref,
              kext_ref, vext_ref, *scr, tb, tq, dils, seq_starts, seq_ends):
    i = pl.program_id(0)
    first = _is_any(i * tb, seq_starts)
    last = _is_any((i + 1) * tb, seq_ends)
    hmax = WIN_HALF * max(dils)
    for ext_ref, p_ref, c_ref, n_ref in ((kext_ref, kp_ref, kc_ref, kn_ref), (vext_ref, vp_ref, vc_ref, vn_ref)):
        ext_ref[0:hmax] = p_ref[...]
        ext_ref[hmax:hmax + tb] = c_ref[...]
        ext_ref[hmax + tb:] = n_ref[...]
    nkeys = tq + 2 * WIN_HALF
    col = lax.broadcasted_iota(jnp.int32, (tq, nkeys), 1)
    bad_first = jnp.logical_and(col < WIN_HALF, first)
    bad_last = jnp.logical_and(col >= WIN_HALF + tq, last)
    scale = HEAD_DIM ** -0.5
    o_scr, l_scr = scr[:len(dils)], scr[len(dils):]
    for p, dil in enumerate(dils):
        nsub = tb // (tq * dil)
        bias = bias_ref[p]
        for sub in range(nsub):
            for r in range(dil):
                start = sub * tq * dil + r
                kstart = hmax - WIN_HALF * dil + start
                q = (q_ref[pl.ds(start, tq, stride=dil), :] * scale).astype(BF16)
                k = kext_ref[pl.ds(kstart, nkeys, stride=dil), :].astype(BF16)
                v = vext_ref[pl.ds(kstart, nkeys, stride=dil), :].astype(BF16)
                s = lax.dot_general(q, k, (((1,), (1,)), ((), ())), preferred_element_type=F32) + bias
                if sub == 0:
                    s = jnp.where(bad_first, NEG, s)
                if sub == nsub - 1:
                    s = jnp.where(bad_last, NEG, s)
                m = jnp.max(s, axis=-1, keepdims=True)
                e = jnp.exp(s - m)
                den = jnp.sum(e, axis=-1, keepdims=True)
                o = jnp.dot(e.astype(BF16), v, preferred_element_type=F32) / den
                o_scr[p][pl.ds(start, tq, stride=dil), :] = o
                l_scr[p][pl.ds(start, tq, stride=dil), :] = jnp.broadcast_to(m + jnp.log(den), (tq, HEAD_DIM))
    lses = [l[...] for l in l_scr]
    m = functools.reduce(jnp.maximum, lses)
    ws = [jnp.exp(l - m) for l in lses]
    num = functools.reduce(lambda a, b: a + b, [w * o[...] for w, o in zip(ws, o_scr)])
    o_ref[...] = (num / functools.reduce(lambda a, b: a + b, ws)).astype(o_ref.dtype)


def _t5_bucket(rel):
    nb = N_BUCKETS // 2
    max_exact = nb // 2
    n = jnp.abs(rel)
    log_ratio = jnp.log(jnp.maximum(n, 1).astype(F32) / max_exact) / math.log(MAX_DIST / max_exact)
    large = jnp.minimum(max_exact + (log_ratio * (nb - max_exact)).astype(jnp.int32), nb - 1)
    return jnp.where(rel > 0, nb, 0) + jnp.where(n < max_exact, n, large)


def _window_bias(rel_bias, dil, tq):
    kj = jnp.arange(tq + 2 * WIN_HALF)
    rel = kj[None, :] - WIN_HALF - jnp.arange(tq)[:, None]
    bias = rel_bias[_t5_bucket(rel * dil)].astype(F32)
    bias = jnp.where((jnp.abs(rel) <= WIN_HALF)[:, :, None], bias, NEG)
    return bias.transpose(2, 0, 1)


def window_attention(proj, rel_bias, dils, n_heads, seq_starts, seq_ends):
    n = proj.shape[0]
    tq = WIN_TQ
    hmax = WIN_HALF * max(dils)
    tb = tq * max(dils)
    assert all(s % tb == 0 for s in seq_starts + seq_ends) and tb % hmax == 0
    bias = jnp.stack([_window_bias(rel_bias, dil, tq) for dil in dils])
    hb = tb // hmax
    nhb = n // hmax
    prev = lambda i: jnp.maximum(i * hb - 1, 0)
    nxt = lambda i: jnp.minimum((i + 1) * hb, nhb - 1)
    cur_spec = lambda g: pl.BlockSpec((tb, HEAD_DIM), lambda i, h: (i, g * n_heads + h))
    prev_spec = lambda g: pl.BlockSpec((hmax, HEAD_DIM), lambda i, h: (prev(i), g * n_heads + h))
    next_spec = lambda g: pl.BlockSpec((hmax, HEAD_DIM), lambda i, h: (nxt(i), g * n_heads + h))
    np_ = len(dils)
    return pl.pallas_call(
        functools.partial(_win_body, tb=tb, tq=tq, dils=dils, seq_starts=seq_starts, seq_ends=seq_ends),
        out_shape=jax.ShapeDtypeStruct((n, n_heads * HEAD_DIM), BF16),
        grid=(n // tb, n_heads),
        in_specs=[cur_spec(0), prev_spec(1), cur_spec(1), next_spec(1),
                  prev_spec(2), cur_spec(2), next_spec(2),
                  pl.BlockSpec((np_, None, tq, tq + 2 * WIN_HALF), lambda i, h: (0, h, 0, 0))],
        out_specs=pl.BlockSpec((tb, HEAD_DIM), lambda i, h: (i, h)),
        scratch_shapes=[pltpu.VMEM((tb + 2 * hmax, HEAD_DIM), F32)] * 2 + [pltpu.VMEM((tb, HEAD_DIM), F32)] * (2 * np_),
        compiler_params=_cparams(("parallel", "parallel")),
        name="win_attn",
    )(proj, proj, proj, proj, proj, proj, proj, bias)


def _rope_half(x, cos, sin_signed):
    return x * cos + pltpu.roll(x, HEAD_DIM // 2, axis=1) * sin_signed


def _ret_bwd_body(lg_ref, q_ref, k_ref, v_ref, cos_ref, sin_ref, o_ref, r_ref, tab_ref, *,
                  n_heads, n_chunks, seq_ends):
    c = RET_CHUNK
    g = n_chunks - 1 - pl.program_id(0)

    @pl.when(pl.program_id(0) == 0)
    def _():
        n = lax.broadcasted_iota(jnp.int32, (c, HEAD_DIM), 0).astype(F32)
        for h in range(n_heads):
            lg = lg_ref[1, h]
            tab_ref[0, h] = jnp.exp(lg * (c - n))
            tab_ref[1, h] = jnp.exp(lg * n)
            tab_ref[2, h] = jnp.exp(jnp.full((HEAD_DIM, HEAD_DIM), lg * c, F32))

    @pl.when(_is_any((g + 1) * c, seq_ends))
    def _():
        r_ref[...] = jnp.zeros(r_ref.shape, F32)

    cos, sin = cos_ref[...], sin_ref[...]
    scale = HEAD_DIM ** -0.5
    for h in range(n_heads):
        sl = slice(h * HEAD_DIM, (h + 1) * HEAD_DIM)
        q = _rope_half(q_ref[:, sl], cos, sin) * scale
        k = _rope_half(k_ref[:, sl], cos, sin)
        v = v_ref[:, sl]
        r = r_ref[h]
        qw = (q * tab_ref[0, h]).astype(BF16)
        o_ref[:, sl] = jnp.dot(qw, r.astype(BF16), preferred_element_type=F32)
        kw_t = (k * tab_ref[1, h]).T.astype(BF16)
        r_ref[h] = tab_ref[2, h] * r + jnp.dot(kw_t, v.astype(BF16), preferred_element_type=F32)


def _ret_fwd_body(lg_ref, q_ref, k_ref, v_ref, gate_ref, xb_ref, cos_ref, sin_ref, o_ref, r_ref, tab_ref, *,
                  n_heads, seq_starts):
    c = RET_CHUNK
    g = pl.program_id(0)

    @pl.when(g == 0)
    def _():
        n = lax.broadcasted_iota(jnp.int32, (c, HEAD_DIM), 0).astype(F32)
        rel = (lax.broadcasted_iota(jnp.int32, (c, c), 0) - lax.broadcasted_iota(jnp.int32, (c, c), 1)).astype(F32)
        for h in range(n_heads):
            lg_f, lg_b = lg_ref[0, h], lg_ref[1, h]
            tab_ref[0, h] = jnp.where(rel >= 0, jnp.exp(lg_f * jnp.maximum(rel, 0.0)),
                                      jnp.exp(lg_b * jnp.maximum(-rel, 0.0)))
            tab_ref[1, h] = jnp.exp(lg_f * (n + 1.0))
            tab_ref[2, h] = jnp.exp(lg_f * (c - 1.0 - n))
            tab_ref[3, h] = jnp.exp(jnp.full((HEAD_DIM, HEAD_DIM), lg_f * c, F32))

    @pl.when(_is_any(g * c, seq_starts))
    def _():
        r_ref[...] = jnp.zeros(r_ref.shape, F32)

    cos, sin = cos_ref[...], sin_ref[...]
    scale = HEAD_DIM ** -0.5
    for h in range(n_heads):
        sl = slice(h * HEAD_DIM, (h + 1) * HEAD_DIM)
        q = _rope_half(q_ref[:, sl], cos, sin) * scale
        k = _rope_half(k_ref[:, sl], cos, sin)
        v = v_ref[:, sl].astype(BF16)
        r = r_ref[h]
        qk = lax.dot_general(q.astype(BF16), k.astype(BF16), (((1,), (1,)), ((), ())), preferred_element_type=F32)
        o = jnp.dot((qk * tab_ref[0, h]).astype(BF16), v, preferred_element_type=F32)
        qw = (q * tab_ref[1, h]).astype(BF16)
        o = o + jnp.dot(qw, r.astype(BF16), preferred_element_type=F32) + xb_ref[:, sl]
        mu = jnp.mean(o, axis=-1, keepdims=True)
        d = o - mu
        var = jnp.mean(d * d, axis=-1, keepdims=True)
        o = d * lax.rsqrt(var + EPS)
        gate = gate_ref[:, sl]
        o_ref[:, sl] = (gate * jax.nn.sigmoid(gate) * o).astype(o_ref.dtype)
        kw_t = (k * tab_ref[2, h]).T.astype(BF16)
        r_ref[h] = tab_ref[3, h] * r + jnp.dot(kw_t, v, preferred_element_type=F32)


def retention(proj, col0, log_gamma, cos, sin, n_heads, seq_starts, seq_ends):
    n, _ = proj.shape
    w = n_heads * HEAD_DIM
    cb0 = col0 // w
    c = RET_CHUNK
    assert c == HEAD_DIM
    n_chunks = n // c
    smem = pl.BlockSpec(memory_space=pltpu.SMEM)
    rev = lambda off: pl.BlockSpec((c, w), lambda s: (n_chunks - 1 - s, cb0 + off))
    rev_tab = pl.BlockSpec((c, HEAD_DIM), lambda s: (n_chunks - 1 - s, 0))
    cross_b = pl.pallas_call(
        functools.partial(_ret_bwd_body, n_heads=n_heads, n_chunks=n_chunks, seq_ends=seq_ends),
        out_shape=jax.ShapeDtypeStruct((n, w), F32),
        grid=(n_chunks,),
        in_specs=[smem, rev(0), rev(1), rev(2), rev_tab, rev_tab],
        out_specs=pl.BlockSpec((c, w), lambda s: (n_chunks - 1 - s, 0)),
        scratch_shapes=[pltpu.VMEM((n_heads, HEAD_DIM, HEAD_DIM), F32),
                        pltpu.VMEM((3, n_heads, c, HEAD_DIM), F32)],
        compiler_params=_cparams(("arbitrary",)),
        name="ret_bwd",
    )(log_gamma, proj, proj, proj, cos, sin)
    fwd = lambda off: pl.BlockSpec((c, w), lambda s: (s, cb0 + off))
    tab = pl.BlockSpec((c, HEAD_DIM), lambda s: (s, 0))
    return pl.pallas_call(
        functools.partial(_ret_fwd_body, n_heads=n_heads, seq_starts=seq_starts),
        out_shape=jax.ShapeDtypeStruct((n, w), BF16),
        grid=(n_chunks,),
        in_specs=[smem, fwd(0), fwd(1), fwd(2), fwd(3), pl.BlockSpec((c, w), lambda s: (s, 0)), tab, tab],
        out_specs=pl.BlockSpec((c, w), lambda s: (s, 0)),
        scratch_shapes=[pltpu.VMEM((n_heads, HEAD_DIM, HEAD_DIM), F32),
                        pltpu.VMEM((4, n_heads, c, HEAD_DIM), F32)],
        compiler_params=_cparams(("arbitrary",)),
        name="ret_fwd",
    )(log_gamma, proj, proj, proj, proj, cross_b, cos, sin)


def _rope_quarter(x, cos, sin_signed, low):
    partner = jnp.where(low, pltpu.roll(x, HEAD_DIM - 32, axis=1), pltpu.roll(x, 32, axis=1))
    return x * cos + partner * sin_signed


def _gqa_prep_body(q_ref, k_ref, v_ref, nw_ref, cos_ref, sin_ref, qo_ref, ko_ref, vto_ref, *, hq, hk):
    cos, sin = cos_ref[...], sin_ref[...]
    low = (lax.broadcasted_iota(jnp.int32, cos.shape, 1) % 64) < 32
    scale = HEAD_DIM ** -0.5 * LOG2E

    def norm_rope(x, w):
        y = x * lax.rsqrt(jnp.mean(x * x, axis=-1, keepdims=True) + EPS) * w
        return _rope_quarter(y, cos, sin, low)

    for h in range(hq):
        sl = slice(h * HEAD_DIM, (h + 1) * HEAD_DIM)
        qo_ref[:, sl] = (norm_rope(q_ref[:, sl], nw_ref[0:1, :]) * scale).astype(qo_ref.dtype)
    for h in range(hk):
        sl = slice(h * HEAD_DIM, (h + 1) * HEAD_DIM)
        ko_ref[:, sl] = norm_rope(k_ref[:, sl], nw_ref[1:2, :]).astype(ko_ref.dtype)
        vto_ref[sl, :] = v_ref[:, sl].T.astype(vto_ref.dtype)


def gqa_prep(proj, qk_norm, cos, sin, hq, hk, tm=256):
    n = proj.shape[0]
    tm = min(tm, n)
    wq, wk = hq * HEAD_DIM, hk * HEAD_DIM
    assert wq % wk == 0
    row = lambda width, blk: pl.BlockSpec((tm, width), lambda i: (i, blk))
    return pl.pallas_call(
        functools.partial(_gqa_prep_body, hq=hq, hk=hk),
        out_shape=[jax.ShapeDtypeStruct((n, wq), BF16), jax.ShapeDtypeStruct((n, wk), BF16),
                   jax.ShapeDtypeStruct((wk, n), BF16)],
        grid=(n // tm,),
        in_specs=[row(wq, 0), row(wk, wq // wk), row(wk, wq // wk + 1),
                  pl.BlockSpec((2, HEAD_DIM), lambda i: (0, 0)), row(HEAD_DIM, 0), row(HEAD_DIM, 0)],
        out_specs=[row(wq, 0), row(wk, 0), pl.BlockSpec((wk, tm), lambda i: (0, i))],
        compiler_params=_cparams(("parallel",)),
        name="gqa_prep",
    )(proj, proj, proj, qk_norm, cos, sin)


def _latent_norm_body(cq_ref, ckv_ref, wq_ref, wkv_ref, qo_ref, kvo_ref):
    def rms(x, w):
        return x * lax.rsqrt(jnp.mean(x * x, axis=-1, keepdims=True) + EPS) * w
    qo_ref[...] = rms(cq_ref[...], wq_ref[...]).astype(qo_ref.dtype)
    kvo_ref[...] = rms(ckv_ref[...], wkv_ref[...]).astype(kvo_ref.dtype)


def latent_norm(proj, col_cq, cq_norm, ckv_norm, tm=512):
    n = proj.shape[0]
    tm = min(tm, n)
    assert col_cq % Q_LORA == 0 and (col_cq + Q_LORA) % KV_LORA == 0
    return pl.pallas_call(
        _latent_norm_body,
        out_shape=[jax.ShapeDtypeStruct((n, Q_LORA), BF16), jax.ShapeDtypeStruct((n, KV_LORA), BF16)],
        grid=(n // tm,),
        in_specs=[pl.BlockSpec((tm, Q_LORA), lambda i: (i, col_cq // Q_LORA)),
                  pl.BlockSpec((tm, KV_LORA), lambda i: (i, (col_cq + Q_LORA) // KV_LORA)),
                  pl.BlockSpec((1, Q_LORA), lambda i: (0, 0)), pl.BlockSpec((1, KV_LORA), lambda i: (0, 0))],
        out_specs=[pl.BlockSpec((tm, Q_LORA), lambda i: (i, 0)), pl.BlockSpec((tm, KV_LORA), lambda i: (i, 0))],
        compiler_params=_cparams(("parallel",)),
        name="latent_norm",
    )(proj, proj, cq_norm.reshape(1, Q_LORA), ckv_norm.reshape(1, KV_LORA))


def _mla_q_body(a_ref, b_ref, cos_ref, sin_ref, qo_ref, *, heads):
    cos, sin = cos_ref[...], sin_ref[...]
    low = (lax.broadcasted_iota(jnp.int32, cos.shape, 1) % 64) < 32
    scale = (QK_NOPE + QK_ROPE) ** -0.5 * LOG2E
    r = jnp.dot(a_ref[...], b_ref[...], preferred_element_type=F32)
    for h in range(heads):
        a = slice(h * MLA_QK_PAD, h * MLA_QK_PAD + QK_NOPE)
        b = slice(h * MLA_QK_PAD + QK_NOPE, (h + 1) * MLA_QK_PAD)
        qo_ref[:, a] = (r[:, a] * scale).astype(qo_ref.dtype)
        qo_ref[:, b] = (_rope_quarter(r[:, b], cos, sin, low) * scale).astype(qo_ref.dtype)


def _mla_kv_body(a_ref, b_ref, kpe_ref, cos_ref, sin_ref, ko_ref, vto_ref, *, heads):
    cos, sin = cos_ref[...], sin_ref[...]
    low = (lax.broadcasted_iota(jnp.int32, cos.shape, 1) % 64) < 32
    k_rope = _rope_quarter(kpe_ref[...], cos, sin, low).astype(ko_ref.dtype)
    r = jnp.dot(a_ref[...], b_ref[...], preferred_element_type=F32)
    for h in range(heads):
        a = slice(h * MLA_QK_PAD, h * MLA_QK_PAD + QK_NOPE)
        b = slice(h * MLA_QK_PAD + QK_NOPE, (h + 1) * MLA_QK_PAD)
        ko_ref[:, a] = r[:, a].astype(ko_ref.dtype)
        ko_ref[:, b] = k_rope
        vto_ref[h * V_D:(h + 1) * V_D, :] = r[:, b].T.astype(vto_ref.dtype)


def mla_up_proj(cqn, ckvn, w_uq, w_ukv, layer, proj, col_kpe, cos, sin, n_heads, tm=512, heads_per_step=4):
    n = cqn.shape[0]
    tm = min(tm, n)
    hs = min(heads_per_step, n_heads)
    assert n_heads % hs == 0 and col_kpe % 128 == 0
    tn = hs * MLA_QK_PAD
    wd = n_heads * MLA_QK_PAD
    grid = (n // tm, n_heads // hs)
    tab = pl.BlockSpec((tm, 128), lambda i, j: (i, 0))
    q = pl.pallas_call(
        functools.partial(_mla_q_body, heads=hs),
        out_shape=jax.ShapeDtypeStruct((n, wd), BF16),
        grid=grid,
        in_specs=[pl.BlockSpec((tm, Q_LORA), lambda i, j: (i, 0)),
                  _layer_spec(w_uq, layer, (Q_LORA, tn), lambda i, j: (0, j)),
                  tab, tab],
        out_specs=pl.BlockSpec((tm, tn), lambda i, j: (i, j)),
        compiler_params=_cparams(("parallel", "parallel")),
        name="mla_q_proj",
    )(cqn, w_uq, cos, sin)
    k, vt = pl.pallas_call(
        functools.partial(_mla_kv_body, heads=hs),
        out_shape=[jax.ShapeDtypeStruct((n, wd), BF16), jax.ShapeDtypeStruct((n_heads * V_D, n), BF16)],
        grid=grid,
        in_specs=[pl.BlockSpec((tm, KV_LORA), lambda i, j: (i, 0)),
                  _layer_spec(w_ukv, layer, (KV_LORA, tn), lambda i, j: (0, j)),
                  pl.BlockSpec((tm, 128), lambda i, j: (i, col_kpe // 128)), tab, tab],
        out_specs=[pl.BlockSpec((tm, tn), lambda i, j: (i, j)),
                   pl.BlockSpec((hs * V_D, tm), lambda i, j: (j, i))],
        compiler_params=_cparams(("parallel", "parallel")),
        name="mla_kv_proj",
    )(ckvn, w_ukv, proj, cos, sin)
    return q, k, vt


def _inv_freq(dim):
    return (np.float32(ROPE_THETA) ** (-np.arange(0, dim, 2, dtype=np.float32) / np.float32(dim))).astype(np.float32)


def _angles(pos, dim):
    ang = pos.astype(F32)[:, None] * jnp.asarray(_inv_freq(dim))[None, :]
    return jnp.cos(ang), jnp.sin(ang)


def _rope_tables(seq_lens):
    tmax = max(seq_lens)
    pos = jnp.arange(tmax)
    per_token = lambda t: jnp.concatenate([t[:n] for n in seq_lens], axis=0)
    c, s = _angles(pos, HEAD_DIM)
    ret = (per_token(jnp.concatenate([c, c], -1)), per_token(jnp.concatenate([-s, s], -1)))
    rows = -(-tmax // GRID_W)
    cr, sr = (jnp.repeat(t, GRID_W, axis=0)[:tmax] for t in _angles(jnp.arange(rows), HEAD_DIM // 2))
    cc, sc = (jnp.tile(t, (rows, 1))[:tmax] for t in _angles(jnp.arange(GRID_W), HEAD_DIM // 2))
    axial = (per_token(jnp.concatenate([cr, cr, cc, cc], -1)), per_token(jnp.concatenate([-sr, sr, -sc, sc], -1)))
    c1, s1 = _angles(pos, QK_ROPE)
    one, zero = jnp.ones_like(c1), jnp.zeros_like(s1)
    mla = (per_token(jnp.concatenate([c1, c1, one, one], -1)), per_token(jnp.concatenate([-s1, s1, zero, zero], -1)))
    return ret, axial, mla


def _pad_last(w, width):
    return jnp.pad(w, ((0, 0),) * (w.ndim - 1) + ((0, width - w.shape[-1]),))


def _mix_even(h, w_in, i, log_1m_gamma, rel_bias, ret_tab, seqs):
    seq_starts, seq_ends = seqs
    n_heads = w_in.shape[-1] // (7 * HEAD_DIM)
    w = n_heads * HEAD_DIM
    proj = matmul(h, w_in, layer=i, out_dtype=F32, name="mm_in_even")
    assert all(window // (2 * dil) == WIN_HALF for window, dil in DIL_PATTERNS)
    dils = tuple(dil for _, dil in DIL_PATTERNS)
    oa = window_attention(proj, rel_bias, dils, n_heads, seq_starts, seq_ends)
    log_gamma = jnp.log1p(-jnp.exp(log_1m_gamma.astype(F32)))
    ob = retention(proj, 3 * w, log_gamma, ret_tab[0], ret_tab[1], n_heads, seq_starts, seq_ends)
    return oa, ob


def _mix_odd(h, w_in, in_odd, i, qk_norm, cq_norm, ckv_norm, w_uq, w_ukv, ax_tab, mla_tab, groups, tiles):
    wkv = KV_C * HEAD_DIM
    wq = in_odd - 2 * wkv - Q_LORA - KV_LORA - QK_ROPE
    hq = wq // HEAD_DIM
    proj = matmul(h, w_in, layer=i, out_dtype=F32, name="mm_in_odd")
    qc, kc, vtc = gqa_prep(proj, qk_norm, ax_tab[0], ax_tab[1], hq, KV_C)
    col_cq = wq + 2 * wkv
    cqn, ckvn = latent_norm(proj, col_cq, cq_norm, ckv_norm)
    n_hd = w_uq.shape[-1] // MLA_QK_PAD
    assert V_D == MLA_QK_PAD - QK_NOPE and w_ukv.shape[-1] == n_hd * MLA_QK_PAD
    q_d, k_d, vt_d = mla_up_proj(cqn, ckvn, w_uq, w_ukv, i, proj, col_cq + Q_LORA + KV_LORA,
                                 mla_tab[0], mla_tab[1], n_hd)
    ocs, ods = [], []
    for row0, batch, seq_len in groups:
        ocs.append(flash_attention(qc, kc, vtc, n_kv_heads=KV_C, g_heads=hq // KV_C, dqk=HEAD_DIM, dv=HEAD_DIM,
                                   row0=row0, batch=batch, seq_len=seq_len, tq=tiles[0], tk=tiles[1],
                                   nsplit=tiles[2], name="attn_gqa"))
        ods.append(flash_attention(q_d, k_d, vt_d, n_kv_heads=n_hd, g_heads=1, dqk=MLA_QK_PAD, dv=V_D,
                                   row0=row0, batch=batch, seq_len=seq_len, tq=tiles[3], tk=tiles[4],
                                   nsplit=tiles[5], name="attn_mla"))
    return jnp.concatenate(ocs, 0), jnp.concatenate(ods, 0)


def _conv_ffn(h, w_up, conv_w, w_down, layer, x, gate, seqs):
    seq_starts, seq_ends = seqs
    dffp = w_down.shape[-2]
    act = ff_up_conv_gate(h, w_up, conv_w, layer, dffp, seq_starts, seq_ends)
    return matmul(act, w_down, layer=layer, out_dtype=F32, tn=1024, tk=2816, resid=x, gate=gate,
                  seq_starts=seq_starts, name="mm_ff_down")


def _trunk(x_prompt, x_sample, c_prompt, c_sample, rel_bias, norm_w, w_mod, b_mod, w_in_even,
           ret_log_1m_gamma, w_out_even, w_in_odd, c_qk_norm, mla_cq_norm, mla_ckv_norm, w_uq,
           w_ukv, w_out_odd, w_ff_up, conv_ff, w_ff_down, final_norm_w, *, attn_tiles, ff_mult):
    bp, tp, d = x_prompt.shape
    bs, ts, _ = x_sample.shape
    depth = norm_w.shape[0]
    seq_lens = (tp,) * bp + (ts,) * bs
    seq_starts = tuple(int(v) for v in np.cumsum((0,) + seq_lens[:-1]))
    seq_ends = tuple(int(v) for v in np.cumsum(seq_lens))
    seqs = (seq_starts, seq_ends)
    groups = ((0, bp, tp), (bp * tp, bs, ts))
    n_seq = len(seq_lens)

    x = jnp.concatenate([x_prompt.reshape(bp * tp, d), x_sample.reshape(bs * ts, d)], axis=0)
    c = jnp.concatenate([c_prompt, c_sample], axis=0)
    c_act = jnp.pad(jax.nn.silu(c), ((0, 8 - n_seq), (0, 0))).astype(BF16)
    ret_tab, ax_tab, mla_tab = _rope_tables(seq_lens)

    in_odd = w_in_odd.shape[-1]
    w_in_even_b = w_in_even.astype(BF16)
    w_in_odd_b = _pad_last(w_in_odd, -(-in_odd // 512) * 512).astype(BF16)
    w_out_b = (w_out_even.astype(BF16), w_out_odd.astype(BF16))
    n_hd = w_uq.shape[-1] // (QK_NOPE + QK_ROPE)
    w_uq_b = _pad_last(w_uq.reshape(-1, Q_LORA, n_hd, QK_NOPE + QK_ROPE), MLA_QK_PAD)
    w_uq_b = w_uq_b.reshape(-1, Q_LORA, n_hd * MLA_QK_PAD).astype(BF16)
    w_ukv_b = w_ukv.astype(BF16)
    dff = w_ff_down.shape[-2]
    dffp = -(-dff // ff_mult) * ff_mult
    split_pad = lambda w: jnp.concatenate([_pad_last(w[..., :dff], dffp), _pad_last(w[..., dff:], dffp)], axis=-1)
    w_up_b = split_pad(w_ff_up).astype(BF16)
    conv_p = split_pad(conv_ff)
    w_down_b = jnp.pad(w_ff_down, ((0, 0), (0, dffp - dff), (0, 0))).astype(BF16)
    b_mod3 = b_mod.reshape(depth, 1, -1)

    for layer in range(depth):
        mod = matmul(c_act, w_mod, layer=layer, out_dtype=F32, tm=8, tn=2048, tk=1024, bias=b_mod3, name="mm_mod")
        mod6 = mod[:n_seq].reshape(n_seq * 6, 1, d)
        gate_of = lambda ci: mod6.reshape(n_seq, 6, d)[:, ci].reshape(n_seq, 1, d)
        h = norm_mod(x, norm_w[layer, 0], mod6, 1, 0, seq_starts)
        i = layer // 2
        if layer % 2 == 0:
            mix = _mix_even(h, w_in_even_b, i, ret_log_1m_gamma[i], rel_bias, ret_tab, seqs)
        else:
            mix = _mix_odd(h, w_in_odd_b, in_odd, i, c_qk_norm[i], mla_cq_norm[i], mla_ckv_norm[i], w_uq_b, w_ukv_b,
                           ax_tab, mla_tab, groups, attn_tiles)
        x = matmul_cat_resid(mix[0], mix[1], w_out_b[layer % 2], i, x, gate_of(2), seq_starts)
        h = norm_mod(x, norm_w[layer, 1], mod6, 4, 3, seq_starts)
        x = _conv_ffn(h, w_up_b, conv_p, w_down_b, layer, x, gate_of(5), seqs)
    y_p, y_s = final_norm(x, final_norm_w, bp * tp)
    return y_p.reshape(bp, tp, d), y_s.reshape(bs, ts, d)


def kernel(x_prompt, x_sample, c_prompt, c_sample, rel_bias, norm_w, w_mod, b_mod, w_in_even, ret_log_1m_gamma, w_out_even, w_in_odd, c_qk_norm, mla_cq_norm, mla_ckv_norm, w_uq, w_ukv, w_out_odd, w_ff_up, conv_ff, w_ff_down, final_norm_w):
    return _trunk(x_prompt, x_sample, c_prompt, c_sample, rel_bias, norm_w, w_mod, b_mod, w_in_even,
                  ret_log_1m_gamma, w_out_even, w_in_odd, c_qk_norm, mla_cq_norm, mla_ckv_norm, w_uq,
                  w_ukv, w_out_odd, w_ff_up, conv_ff, w_ff_down, final_norm_w,
                  attn_tiles=(512, 2048, 8, 2048, 2048, 8), ff_mult=1024)
```

```python
import functools
import math

import numpy as np
import jax
import jax.numpy as jnp
from jax import lax
from jax.experimental import pallas as pl
from jax.experimental.pallas import tpu as pltpu

F32 = jnp.float32
BF16 = jnp.bfloat16

HEAD_DIM = 128
GRID_W = 64
DIL_PATTERNS = ((128, 1), (512, 4), (2048, 16))
RET_CHUNK = 128
KV_C = 4
Q_LORA = 768
KV_LORA = 256
QK_NOPE = 128
QK_ROPE = 64
V_D = 128
MLA_QK_PAD = 256
N_BUCKETS = 32
MAX_DIST = 1024
ROPE_THETA = 10000.0
EPS = 1e-6
NEG = -1e30
LOG2E = math.log2(math.e)
FLASH_HEADROOM = 60.0
FLASH_SEED_KEYS = 256
FLASH_TWO_PASS_KEYS = 2048

WIN_HALF = 64
WIN_TQ = 128
BF16_ROWS = 16
FF_UP_TN = 512
VMEM_LIMIT = 56 * 1024 * 1024


def _cparams(sem):
    return pltpu.CompilerParams(dimension_semantics=sem, vmem_limit_bytes=VMEM_LIMIT)


def _seq_id(row, seq_starts):
    s = 0
    for st in seq_starts[1:]:
        s = s + (row >= st).astype(jnp.int32)
    return s


def _is_any(row, marks):
    r = row == marks[0]
    for m in marks[1:]:
        r = jnp.logical_or(r, row == m)
    return r


def _largest_tile(n, cap):
    best = 128
    for t in range(128, cap + 1, 128):
        if n % t == 0:
            best = t
    return best


def _mm_body(*refs, nk, mode):
    a_ref, b_ref = refs[0], refs[1]
    if mode == "plain":
        extra, (o_ref, acc_ref) = (), refs[2:]
    elif mode == "bias":
        extra, (o_ref, acc_ref) = refs[2:3], refs[3:]
    else:
        extra, (o_ref, acc_ref) = refs[2:4], refs[4:]

    def epilogue(r):
        if mode == "bias":
            r = r + extra[0][...]
        elif mode == "resid":
            r = extra[0][...] + extra[1][0] * r
        o_ref[...] = r.astype(o_ref.dtype)

    def prod():
        return jnp.dot(a_ref[...].astype(BF16), b_ref[...].astype(BF16), preferred_element_type=F32)

    if nk == 1:
        epilogue(prod())
        return
    k = pl.program_id(2)

    @pl.when(k == 0)
    def _():
        acc_ref[...] = prod()

    @pl.when(jnp.logical_and(k > 0, k < nk - 1))
    def _():
        acc_ref[...] += prod()

    @pl.when(k == nk - 1)
    def _():
        epilogue(acc_ref[...] + prod())


def _layer_spec(w, layer, block, index_map):
    if w.ndim == len(block):
        return pl.BlockSpec(block, index_map)
    assert w.ndim == len(block) + 1
    return pl.BlockSpec((None,) + tuple(block), lambda *g: (layer,) + tuple(index_map(*g)))


def matmul(a, b, *, out_dtype, layer=None, tm=1024, tn=512, tk=4096, bias=None, resid=None, gate=None,
           seq_starts=None, name="mm"):
    m, kd = a.shape
    n = b.shape[-1]
    tm, tn, tk = min(tm, m), _largest_tile(n, tn), _largest_tile(kd, tk)
    assert m % tm == 0 and n % tn == 0 and kd % tk == 0 and b.shape[-2] == kd, (a.shape, b.shape, tm, tn, tk)
    nk = kd // tk
    in_specs = [pl.BlockSpec((tm, tk), lambda i, j, k: (i, k)),
                _layer_spec(b, layer, (tk, tn), lambda i, j, k: (k, j))]
    args = [a, b]
    if bias is not None:
        mode = "bias"
        in_specs.append(_layer_spec(bias, layer, (1, tn), lambda i, j, k: (0, j)))
        args.append(bias)
    elif resid is not None:
        mode = "resid"
        in_specs.append(pl.BlockSpec((tm, tn), lambda i, j, k: (i, j)))
        in_specs.append(pl.BlockSpec((1, 1, tn), lambda i, j, k: (_seq_id(i * tm, seq_starts), 0, j)))
        args += [resid, gate]
    else:
        mode = "plain"
    return pl.pallas_call(
        functools.partial(_mm_body, nk=nk, mode=mode),
        out_shape=jax.ShapeDtypeStruct((m, n), out_dtype),
        grid=(m // tm, n // tn, nk),
        in_specs=in_specs,
        out_specs=pl.BlockSpec((tm, tn), lambda i, j, k: (i, j)),
        scratch_shapes=[pltpu.VMEM((tm, tn) if nk > 1 else (8, 128), F32)],
        compiler_params=_cparams(("parallel", "parallel", "arbitrary")),
        name=name,
    )(*args)


def _mm_cat_body(a1_ref, a2_ref, b_ref, x_ref, g_ref, o_ref, *, k1):
    r = jnp.dot(a1_ref[...], b_ref[0:k1], preferred_element_type=F32)
    r = r + jnp.dot(a2_ref[...], b_ref[k1:], preferred_element_type=F32)
    o_ref[...] = (x_ref[...] + g_ref[0] * r).astype(o_ref.dtype)


def matmul_cat_resid(a1, a2, b, layer, resid, gate, seq_starts, tm=1024, tn=512, name="mm_out"):
    m, k1 = a1.shape
    k2 = a2.shape[1]
    n = b.shape[-1]
    tm, tn = min(tm, m), _largest_tile(n, tn)
    assert b.shape[-2] == k1 + k2 and m % tm == 0 and k1 % BF16_ROWS == 0
    return pl.pallas_call(
        functools.partial(_mm_cat_body, k1=k1),
        out_shape=jax.ShapeDtypeStruct((m, n), resid.dtype),
        grid=(m // tm, n // tn),
        in_specs=[pl.BlockSpec((tm, k1), lambda i, j: (i, 0)),
                  pl.BlockSpec((tm, k2), lambda i, j: (i, 0)),
                  _layer_spec(b, layer, (k1 + k2, tn), lambda i, j: (0, j)),
                  pl.BlockSpec((tm, tn), lambda i, j: (i, j)),
                  pl.BlockSpec((1, 1, tn), lambda i, j: (_seq_id(i * tm, seq_starts), 0, j))],
        out_specs=pl.BlockSpec((tm, tn), lambda i, j: (i, j)),
        compiler_params=_cparams(("parallel", "parallel")),
        name=name,
    )(a1, a2, b, resid, gate)


def _ffup_body(a_ref, ap_ref, an_ref, b_ref, cw_ref, o_ref, ext_ref, *, tm, seq_starts, seq_ends):
    i = pl.program_id(0)
    halo = BF16_ROWS
    tn = o_ref.shape[1]

    @pl.when(pl.program_id(1) == 0)
    def _():
        ext_ref[0:tm] = a_ref[...]
        hrow = lax.broadcasted_iota(jnp.int32, ap_ref.shape, 0)
        ext_ref[tm:] = jnp.where(hrow < halo // 2, an_ref[...], ap_ref[...])

    first = _is_any(i * tm, seq_starts)
    last = _is_any((i + 1) * tm, seq_ends)
    ext = ext_ref[...]
    rows = tm + halo
    row = lax.broadcasted_iota(jnp.int32, (tm, o_ref.shape[1]), 0)
    kill_prev = jnp.logical_and(row == 0, first)
    kill_next = jnp.logical_and(row == tm - 1, last)

    def conv(cols):
        u = jnp.dot(ext, b_ref[:, cols], preferred_element_type=F32)
        u_prev = jnp.where(kill_prev, 0.0, pltpu.roll(u, 1, axis=0)[0:tm])
        u_next = jnp.where(kill_next, 0.0, pltpu.roll(u, rows - 1, axis=0)[0:tm])
        cw = cw_ref[:, cols]
        return u_prev * cw[0:1] + u[0:tm] * cw[1:2] + u_next * cw[2:3]

    g = conv(slice(0, tn))
    val = conv(slice(tn, 2 * tn))
    o_ref[...] = (g * jax.nn.sigmoid(g) * val).astype(o_ref.dtype)


def _ff_up_tile(dff):
    return _largest_tile(dff, FF_UP_TN)


def ff_interleave(w, dff, dffp):
    tn = _ff_up_tile(dffp)
    halves = [_pad_last(h, dffp).reshape(h.shape[:-1] + (dffp // tn, 1, tn)) for h in (w[..., :dff], w[..., dff:])]
    return jnp.concatenate(halves, axis=-2).reshape(w.shape[:-1] + (2 * dffp,))


def ff_up_conv_gate(h, w_up, conv_w, layer, dff, seq_starts, seq_ends, tm=1024):
    n, d = h.shape
    tm, tn = min(tm, n), _ff_up_tile(dff)
    nj = dff // tn
    hb = tm // BF16_ROWS
    nhb = n // BF16_ROWS
    prev = lambda i: jnp.maximum(i * hb - 1, 0)
    nxt = lambda i: jnp.minimum((i + 1) * hb, nhb - 1)
    return pl.pallas_call(
        functools.partial(_ffup_body, tm=tm, seq_starts=seq_starts, seq_ends=seq_ends),
        out_shape=jax.ShapeDtypeStruct((n, dff), BF16),
        grid=(n // tm, nj),
        in_specs=[pl.BlockSpec((tm, d), lambda i, j: (i, 0)),
                  pl.BlockSpec((BF16_ROWS, d), lambda i, j: (prev(i), 0)),
                  pl.BlockSpec((BF16_ROWS, d), lambda i, j: (nxt(i), 0)),
                  _layer_spec(w_up, layer, (d, 2 * tn), lambda i, j: (0, j)),
                  _layer_spec(conv_w, layer, (3, 2 * tn), lambda i, j: (0, j))],
        out_specs=pl.BlockSpec((tm, tn), lambda i, j: (i, j)),
        scratch_shapes=[pltpu.VMEM((tm + BF16_ROWS, d), BF16)],
        compiler_params=_cparams(("parallel", "arbitrary")),
        name="ff_up_conv_gate",
    )(h, h, h, w_up, conv_w)


def _normmod_body(x_ref, w_ref, sc_ref, sh_ref, o_ref):
    x = x_ref[...]
    y = x * lax.rsqrt(jnp.mean(x * x, axis=-1, keepdims=True) + EPS) * w_ref[...]
    o_ref[...] = (y * (1.0 + sc_ref[0]) + sh_ref[0]).astype(o_ref.dtype)


def norm_mod(x, w, mod6, scale_idx, shift_idx, seq_starts, tm=512):
    n, d = x.shape
    tm = min(tm, n)
    return pl.pallas_call(
        _normmod_body,
        out_shape=jax.ShapeDtypeStruct((n, d), BF16),
        grid=(n // tm,),
        in_specs=[pl.BlockSpec((tm, d), lambda i: (i, 0)),
                  pl.BlockSpec((1, d), lambda i: (0, 0)),
                  pl.BlockSpec((1, 1, d), lambda i: (_seq_id(i * tm, seq_starts) * 6 + scale_idx, 0, 0)),
                  pl.BlockSpec((1, 1, d), lambda i: (_seq_id(i * tm, seq_starts) * 6 + shift_idx, 0, 0))],
        out_specs=pl.BlockSpec((tm, d), lambda i: (i, 0)),
        compiler_params=_cparams(("parallel",)),
        name="norm_mod",
    )(x, w.reshape(1, d), mod6, mod6)


def _final_norm_body(x_ref, w_ref, oa_ref, ob_ref, *, na_tiles):
    x = x_ref[...]
    y = x * lax.rsqrt(jnp.mean(x * x, axis=-1, keepdims=True) + EPS) * w_ref[...]
    i = pl.program_id(0)

    @pl.when(i < na_tiles)
    def _():
        oa_ref[...] = y

    @pl.when(i >= na_tiles)
    def _():
        ob_ref[...] = y


def final_norm(x, w, n_first, tm=256):
    n, d = x.shape
    tm = min(tm, n_first, n - n_first)
    assert n_first % tm == 0 and n % tm == 0
    na = n_first // tm
    return pl.pallas_call(
        functools.partial(_final_norm_body, na_tiles=na),
        out_shape=[jax.ShapeDtypeStruct((n_first, d), F32), jax.ShapeDtypeStruct((n - n_first, d), F32)],
        grid=(n // tm,),
        in_specs=[pl.BlockSpec((tm, d), lambda i: (i, 0)), pl.BlockSpec((1, d), lambda i: (0, 0))],
        out_specs=[pl.BlockSpec((tm, d), lambda i: (jnp.minimum(i, na - 1), 0)),
                   pl.BlockSpec((tm, d), lambda i: (jnp.maximum(i - na, 0), 0))],
        compiler_params=_cparams(("arbitrary",)),
        name="final_norm",
    )(x, w.reshape(1, d))


def _flash_body(q_ref, k_ref, vt_ref, o_ref, m_ref, l_ref, acc_ref, pv_ref, mc_ref, *,
                g_heads, dqk, dv, nkv, tq, nsplit):
    j = pl.program_id(3)
    cs = (g_heads * tq) // nsplit
    cols = [slice(c * cs, (c + 1) * cs) for c in range(nsplit)]
    if g_heads == 1:
        qcat = q_ref[...]
    else:
        qcat = jnp.concatenate([q_ref[:, g * dqk:(g + 1) * dqk] for g in range(g_heads)], axis=0)

    @pl.when(j == 0)
    def _():
        k_head = k_ref[0:FLASH_SEED_KEYS, :]
        for sl in cols:
            s0 = lax.dot_general(k_head, qcat[sl], (((1,), (1,)), ((), ())), preferred_element_type=F32)
            m_ref[:, sl] = jnp.max(s0, axis=0, keepdims=True)
        l_ref[...] = jnp.zeros(l_ref.shape, F32)
        acc_ref[...] = jnp.zeros(acc_ref.shape, F32)

    k = k_ref[...]
    vt1 = jnp.concatenate([vt_ref[...], jnp.ones((BF16_ROWS, vt_ref.shape[1]), BF16)], axis=0)
    qk = lambda sl: lax.dot_general(k, qcat[sl], (((1,), (1,)), ((), ())), preferred_element_type=F32)

    @pl.when(j < nkv)
    def _():
        ps = []
        for sl in cols:
            s = qk(sl)
            mc_ref[:, sl] = jnp.max(s, axis=0, keepdims=True)
            ps.append(jnp.exp2(s - m_ref[:, sl]).astype(BF16))
        for p, sl in zip(ps, cols):
            pv_ref[:, sl] = jnp.dot(vt1, p, preferred_element_type=F32)

    single_pass_ok = jnp.max(mc_ref[...] - m_ref[...]) <= FLASH_HEADROOM

    @pl.when(single_pass_ok)
    def _():
        m_old = m_ref[...]
        m_new = jnp.maximum(m_old, mc_ref[...])
        alpha = jnp.exp2(m_old - m_new)
        l_ref[...] = alpha * (l_ref[...] + pv_ref[dv:dv + 1, :])
        acc_ref[...] = alpha * (acc_ref[...] + pv_ref[0:dv, :])
        m_ref[...] = m_new

    @pl.when(jnp.logical_not(single_pass_ok))
    def _():
        tk = k_ref.shape[0]
        step = min(tk, FLASH_TWO_PASS_KEYS)
        for k0 in range(0, tk, step):
            k_sub, vt_sub = k[k0:k0 + step], vt1[:, k0:k0 + step]
            scores = [lax.dot_general(k_sub, qcat[sl], (((1,), (1,)), ((), ())), preferred_element_type=F32)
                      for sl in cols]
            for s, sl in zip(scores, cols):
                m_prev = m_ref[:, sl]
                m_new = jnp.maximum(m_prev, jnp.max(s, axis=0, keepdims=True))
                alpha = jnp.exp2(m_prev - m_new)
                p = jnp.exp2(s - m_new)
                pv = jnp.dot(vt_sub, p.astype(BF16), preferred_element_type=F32)
                l_ref[:, sl] = alpha * l_ref[:, sl] + pv[dv:dv + 1]
                acc_ref[:, sl] = alpha * acc_ref[:, sl] + pv[:dv]
                m_ref[:, sl] = m_new

    @pl.when(j == nkv - 1)
    def _():
        o_t = acc_ref[...] / l_ref[...]
        for g in range(g_heads):
            o_ref[:, g * dv:(g + 1) * dv] = o_t[:, g * tq:(g + 1) * tq].T.astype(o_ref.dtype)


def flash_attention(q, k, vt, *, n_kv_heads, g_heads, dqk, dv, row0, batch, seq_len, tq, tk, nsplit, name):
    tq, tk = min(tq, seq_len), min(tk, seq_len)
    nq, nkv = seq_len // tq, seq_len // tk
    qb0, kb0 = row0 // tq, row0 // tk
    ncol = g_heads * tq
    nsplit = min(nsplit, ncol // 128)
    return pl.pallas_call(
        functools.partial(_flash_body, g_heads=g_heads, dqk=dqk, dv=dv, nkv=nkv, tq=tq, nsplit=nsplit),
        out_shape=jax.ShapeDtypeStruct((batch * seq_len, n_kv_heads * g_heads * dv), BF16),
        grid=(batch, n_kv_heads, nq, nkv),
        in_specs=[pl.BlockSpec((tq, g_heads * dqk), lambda b, h, i, j: (qb0 + b * nq + i, h)),
                  pl.BlockSpec((tk, dqk), lambda b, h, i, j: (kb0 + b * nkv + j, h)),
                  pl.BlockSpec((dv, tk), lambda b, h, i, j: (h, kb0 + b * nkv + j))],
        out_specs=pl.BlockSpec((tq, g_heads * dv), lambda b, h, i, j: (b * nq + i, h)),
        scratch_shapes=[pltpu.VMEM((1, ncol), F32), pltpu.VMEM((1, ncol), F32), pltpu.VMEM((dv, ncol), F32),
                        pltpu.VMEM((dv + BF16_ROWS, ncol), F32), pltpu.VMEM((1, ncol), F32)],
        compiler_params=_cparams(("parallel", "parallel", "parallel", "arbitrary")),
        name=name,
    )(q, k, vt)


def _win_body(q_ref, kp_ref, kc_ref, kn_ref, vp_ref, vc_ref, vn_ref, bias_ref, o_ref,
              kext_ref, vext_ref, *scr, tb, tq, dils, seq_starts, seq_ends):
    i = pl.program_id(0)
    first = _is_any(i * tb, seq_starts)
    last = _is_any((i + 1) * tb, seq_ends)
    hmax = WIN_HALF * max(dils)
    for ext_ref, p_ref, c_ref, n_ref in ((kext_ref, kp_ref, kc_ref, kn_ref), (vext_ref, vp_ref, vc_ref, vn_ref)):
        ext_ref[0:hmax] = p_ref[...]
        ext_ref[hmax:hmax + tb] = c_ref[...]
        ext_ref[hmax + tb:] = n_ref[...]
    nkeys = tq + 2 * WIN_HALF
    col = lax.broadcasted_iota(jnp.int32, (tq, nkeys), 1)
    bad_first = jnp.logical_and(col < WIN_HALF, first)
    bad_last = jnp.logical_and(col >= WIN_HALF + tq, last)
    scale = HEAD_DIM ** -0.5
    o_scr, l_scr = scr[:len(dils)], scr[len(dils):]
    for p, dil in enumerate(dils):
        nsub = tb // (tq * dil)
        bias = bias_ref[p]
        for sub in range(nsub):
            for r in range(dil):
                start = sub * tq * dil + r
                kstart = hmax - WIN_HALF * dil + start
                q = (q_ref[pl.ds(start, tq, stride=dil), :] * scale).astype(BF16)
                k = kext_ref[pl.ds(kstart, nkeys, stride=dil), :].astype(BF16)
                v = vext_ref[pl.ds(kstart, nkeys, stride=dil), :].astype(BF16)
                s = lax.dot_general(q, k, (((1,), (1,)), ((), ())), preferred_element_type=F32) + bias
                if sub == 0:
                    s = jnp.where(bad_first, NEG, s)
                if sub == nsub - 1:
                    s = jnp.where(bad_last, NEG, s)
                m = jnp.max(s, axis=-1, keepdims=True)
                e = jnp.exp(s - m)
                den = jnp.sum(e, axis=-1, keepdims=True)
                o = jnp.dot(e.astype(BF16), v, preferred_element_type=F32) / den
                o_scr[p][pl.ds(start, tq, stride=dil), :] = o
                l_scr[p][pl.ds(start, tq, stride=dil), :] = jnp.broadcast_to(m + jnp.log(den), (tq, HEAD_DIM))
    lses = [l[...] for l in l_scr]
    m = functools.reduce(jnp.maximum, lses)
    ws = [jnp.exp(l - m) for l in lses]
    num = functools.reduce(lambda a, b: a + b, [w * o[...] for w, o in zip(ws, o_scr)])
    o_ref[...] = (num / functools.reduce(lambda a, b: a + b, ws)).astype(o_ref.dtype)


def _t5_bucket(rel):
    nb = N_BUCKETS // 2
    max_exact = nb // 2
    n = jnp.abs(rel)
    log_ratio = jnp.log(jnp.maximum(n, 1).astype(F32) / max_exact) / math.log(MAX_DIST / max_exact)
    large = jnp.minimum(max_exact + (log_ratio * (nb - max_exact)).astype(jnp.int32), nb - 1)
    return jnp.where(rel > 0, nb, 0) + jnp.where(n < max_exact, n, large)


def _window_bias(rel_bias, dil, tq):
    kj = jnp.arange(tq + 2 * WIN_HALF)
    rel = kj[None, :] - WIN_HALF - jnp.arange(tq)[:, None]
    onehot = (_t5_bucket(rel * dil)[:, :, None] == jnp.arange(N_BUCKETS)).astype(F32)
    bias = jnp.einsum("qkb,bh->qkh", onehot, rel_bias.astype(F32), precision=lax.Precision.HIGHEST)
    bias = jnp.where((jnp.abs(rel) <= WIN_HALF)[:, :, None], bias, NEG)
    return bias.transpose(2, 0, 1)


def window_attention(proj, rel_bias, dils, n_heads, seq_starts, seq_ends):
    n = proj.shape[0]
    tq = WIN_TQ
    hmax = WIN_HALF * max(dils)
    tb = tq * max(dils)
    assert all(s % tb == 0 for s in seq_starts + seq_ends) and tb % hmax == 0
    bias = jnp.stack([_window_bias(rel_bias, dil, tq) for dil in dils])
    hb = tb // hmax
    nhb = n // hmax
    prev = lambda i: jnp.maximum(i * hb - 1, 0)
    nxt = lambda i: jnp.minimum((i + 1) * hb, nhb - 1)
    cur_spec = lambda g: pl.BlockSpec((tb, HEAD_DIM), lambda i, h: (i, g * n_heads + h))
    prev_spec = lambda g: pl.BlockSpec((hmax, HEAD_DIM), lambda i, h: (prev(i), g * n_heads + h))
    next_spec = lambda g: pl.BlockSpec((hmax, HEAD_DIM), lambda i, h: (nxt(i), g * n_heads + h))
    np_ = len(dils)
    return pl.pallas_call(
        functools.partial(_win_body, tb=tb, tq=tq, dils=dils, seq_starts=seq_starts, seq_ends=seq_ends),
        out_shape=jax.ShapeDtypeStruct((n, n_heads * HEAD_DIM), BF16),
        grid=(n // tb, n_heads),
        in_specs=[cur_spec(0), prev_spec(1), cur_spec(1), next_spec(1),
                  prev_spec(2), cur_spec(2), next_spec(2),
                  pl.BlockSpec((np_, None, tq, tq + 2 * WIN_HALF), lambda i, h: (0, h, 0, 0))],
        out_specs=pl.BlockSpec((tb, HEAD_DIM), lambda i, h: (i, h)),
        scratch_shapes=[pltpu.VMEM((tb + 2 * hmax, HEAD_DIM), F32)] * 2 + [pltpu.VMEM((tb, HEAD_DIM), F32)] * (2 * np_),
        compiler_params=_cparams(("parallel", "parallel")),
        name="win_attn",
    )(proj, proj, proj, proj, proj, proj, proj, bias)


def _rope_half(x, cos, sin_signed):
    return x * cos + pltpu.roll(x, HEAD_DIM // 2, axis=1) * sin_signed


def _ret_bwd_body(lg_ref, q_ref, k_ref, v_ref, cos_ref, sin_ref, o_ref, r_ref, tab_ref, *,
                  n_heads, n_chunks, seq_ends):
    c = RET_CHUNK
    g = n_chunks - 1 - pl.program_id(0)

    @pl.when(pl.program_id(0) == 0)
    def _():
        n = lax.broadcasted_iota(jnp.int32, (c, HEAD_DIM), 0).astype(F32)
        for h in range(n_heads):
            lg = lg_ref[1, h]
            tab_ref[0, h] = jnp.exp(lg * (c - n))
            tab_ref[1, h] = jnp.exp(lg * n)
            tab_ref[2, h] = jnp.exp(jnp.full((HEAD_DIM, HEAD_DIM), lg * c, F32))

    @pl.when(_is_any((g + 1) * c, seq_ends))
    def _():
        r_ref[...] = jnp.zeros(r_ref.shape, F32)

    cos, sin = cos_ref[...], sin_ref[...]
    scale = HEAD_DIM ** -0.5
    for h in range(n_heads):
        sl = slice(h * HEAD_DIM, (h + 1) * HEAD_DIM)
        q = _rope_half(q_ref[:, sl], cos, sin) * scale
        k = _rope_half(k_ref[:, sl], cos, sin)
        v = v_ref[:, sl]
        r = r_ref[h]
        qw = (q * tab_ref[0, h]).astype(BF16)
        o_ref[:, sl] = jnp.dot(qw, r.astype(BF16), preferred_element_type=F32)
        kw_t = (k * tab_ref[1, h]).T.astype(BF16)
        r_ref[h] = tab_ref[2, h] * r + jnp.dot(kw_t, v.astype(BF16), preferred_element_type=F32)


def _ret_fwd_body(lg_ref, q_ref, k_ref, v_ref, gate_ref, xb_ref, cos_ref, sin_ref, o_ref, r_ref, tab_ref, *,
                  n_heads, seq_starts):
    c = RET_CHUNK
    g = pl.program_id(0)

    @pl.when(g == 0)
    def _():
        n = lax.broadcasted_iota(jnp.int32, (c, HEAD_DIM), 0).astype(F32)
        rel = (lax.broadcasted_iota(jnp.int32, (c, c), 0) - lax.broadcasted_iota(jnp.int32, (c, c), 1)).astype(F32)
        for h in range(n_heads):
            lg_f, lg_b = lg_ref[0, h], lg_ref[1, h]
            tab_ref[0, h] = jnp.where(rel >= 0, jnp.exp(lg_f * jnp.maximum(rel, 0.0)),
                                      jnp.exp(lg_b * jnp.maximum(-rel, 0.0)))
            tab_ref[1, h] = jnp.exp(lg_f * (n + 1.0))
            tab_ref[2, h] = jnp.exp(lg_f * (c - 1.0 - n))
            tab_ref[3, h] = jnp.exp(jnp.full((HEAD_DIM, HEAD_DIM), lg_f * c, F32))

    @pl.when(_is_any(g * c, seq_starts))
    def _():
        r_ref[...] = jnp.zeros(r_ref.shape, F32)

    cos, sin = cos_ref[...], sin_ref[...]
    scale = HEAD_DIM ** -0.5
    for h in range(n_heads):
        sl = slice(h * HEAD_DIM, (h + 1) * HEAD_DIM)
        q = _rope_half(q_ref[:, sl], cos, sin) * scale
        k = _rope_half(k_ref[:, sl], cos, sin)
        v = v_ref[:, sl].astype(BF16)
        r = r_ref[h]
        qk = lax.dot_general(q.astype(BF16), k.astype(BF16), (((1,), (1,)), ((), ())), preferred_element_type=F32)
        o = jnp.dot((qk * tab_ref[0, h]).astype(BF16), v, preferred_element_type=F32)
        qw = (q * tab_ref[1, h]).astype(BF16)
        o = o + jnp.dot(qw, r.astype(BF16), preferred_element_type=F32) + xb_ref[:, sl]
        mu = jnp.mean(o, axis=-1, keepdims=True)
        d = o - mu
        var = jnp.mean(d * d, axis=-1, keepdims=True)
        o = d * lax.rsqrt(var + EPS)
        gate = gate_ref[:, sl]
        o_ref[:, sl] = (gate * jax.nn.sigmoid(gate) * o).astype(o_ref.dtype)
        kw_t = (k * tab_ref[2, h]).T.astype(BF16)
        r_ref[h] = tab_ref[3, h] * r + jnp.dot(kw_t, v, preferred_element_type=F32)


def retention(proj, col0, log_gamma, cos, sin, n_heads, seq_starts, seq_ends):
    n, _ = proj.shape
    w = n_heads * HEAD_DIM
    cb0 = col0 // w
    c = RET_CHUNK
    assert c == HEAD_DIM
    n_chunks = n // c
    smem = pl.BlockSpec(memory_space=pltpu.SMEM)
    rev = lambda off: pl.BlockSpec((c, w), lambda s: (n_chunks - 1 - s, cb0 + off))
    rev_tab = pl.BlockSpec((c, HEAD_DIM), lambda s: (n_chunks - 1 - s, 0))
    cross_b = pl.pallas_call(
        functools.partial(_ret_bwd_body, n_heads=n_heads, n_chunks=n_chunks, seq_ends=seq_ends),
        out_shape=jax.ShapeDtypeStruct((n, w), F32),
        grid=(n_chunks,),
        in_specs=[smem, rev(0), rev(1), rev(2), rev_tab, rev_tab],
        out_specs=pl.BlockSpec((c, w), lambda s: (n_chunks - 1 - s, 0)),
        scratch_shapes=[pltpu.VMEM((n_heads, HEAD_DIM, HEAD_DIM), F32),
                        pltpu.VMEM((3, n_heads, c, HEAD_DIM), F32)],
        compiler_params=_cparams(("arbitrary",)),
        name="ret_bwd",
    )(log_gamma, proj, proj, proj, cos, sin)
    fwd = lambda off: pl.BlockSpec((c, w), lambda s: (s, cb0 + off))
    tab = pl.BlockSpec((c, HEAD_DIM), lambda s: (s, 0))
    return pl.pallas_call(
        functools.partial(_ret_fwd_body, n_heads=n_heads, seq_starts=seq_starts),
        out_shape=jax.ShapeDtypeStruct((n, w), BF16),
        grid=(n_chunks,),
        in_specs=[smem, fwd(0), fwd(1), fwd(2), fwd(3), pl.BlockSpec((c, w), lambda s: (s, 0)), tab, tab],
        out_specs=pl.BlockSpec((c, w), lambda s: (s, 0)),
        scratch_shapes=[pltpu.VMEM((n_heads, HEAD_DIM, HEAD_DIM), F32),
                        pltpu.VMEM((4, n_heads, c, HEAD_DIM), F32)],
        compiler_params=_cparams(("arbitrary",)),
        name="ret_fwd",
    )(log_gamma, proj, proj, proj, proj, cross_b, cos, sin)


def _rope_quarter(x, cos, sin_signed, low):
    partner = jnp.where(low, pltpu.roll(x, HEAD_DIM - 32, axis=1), pltpu.roll(x, 32, axis=1))
    return x * cos + partner * sin_signed


def _gqa_prep_body(q_ref, k_ref, v_ref, nw_ref, cos_ref, sin_ref, qo_ref, ko_ref, vto_ref, *, hq, hk):
    cos, sin = cos_ref[...], sin_ref[...]
    low = (lax.broadcasted_iota(jnp.int32, cos.shape, 1) % 64) < 32
    scale = HEAD_DIM ** -0.5 * LOG2E

    def norm_rope(x, w):
        y = x * lax.rsqrt(jnp.mean(x * x, axis=-1, keepdims=True) + EPS) * w
        return _rope_quarter(y, cos, sin, low)

    for h in range(hq):
        sl = slice(h * HEAD_DIM, (h + 1) * HEAD_DIM)
        qo_ref[:, sl] = (norm_rope(q_ref[:, sl], nw_ref[0:1, :]) * scale).astype(qo_ref.dtype)
    for h in range(hk):
        sl = slice(h * HEAD_DIM, (h + 1) * HEAD_DIM)
        ko_ref[:, sl] = norm_rope(k_ref[:, sl], nw_ref[1:2, :]).astype(ko_ref.dtype)
        vto_ref[sl, :] = v_ref[:, sl].T.astype(vto_ref.dtype)


def gqa_prep(proj, qk_norm, cos, sin, hq, hk, tm=256):
    n = proj.shape[0]
    tm = min(tm, n)
    wq, wk = hq * HEAD_DIM, hk * HEAD_DIM
    assert wq % wk == 0
    row = lambda width, blk: pl.BlockSpec((tm, width), lambda i: (i, blk))
    return pl.pallas_call(
        functools.partial(_gqa_prep_body, hq=hq, hk=hk),
        out_shape=[jax.ShapeDtypeStruct((n, wq), BF16), jax.ShapeDtypeStruct((n, wk), BF16),
                   jax.ShapeDtypeStruct((wk, n), BF16)],
        grid=(n // tm,),
        in_specs=[row(wq, 0), row(wk, wq // wk), row(wk, wq // wk + 1),
                  pl.BlockSpec((2, HEAD_DIM), lambda i: (0, 0)), row(HEAD_DIM, 0), row(HEAD_DIM, 0)],
        out_specs=[row(wq, 0), row(wk, 0), pl.BlockSpec((wk, tm), lambda i: (0, i))],
        compiler_params=_cparams(("parallel",)),
        name="gqa_prep",
    )(proj, proj, proj, qk_norm, cos, sin)


def _latent_norm_body(cq_ref, ckv_ref, wq_ref, wkv_ref, qo_ref, kvo_ref):
    def rms(x, w):
        return x * lax.rsqrt(jnp.mean(x * x, axis=-1, keepdims=True) + EPS) * w
    qo_ref[...] = rms(cq_ref[...], wq_ref[...]).astype(qo_ref.dtype)
    kvo_ref[...] = rms(ckv_ref[...], wkv_ref[...]).astype(kvo_ref.dtype)


def latent_norm(proj, col_cq, cq_norm, ckv_norm, tm=512):
    n = proj.shape[0]
    tm = min(tm, n)
    assert col_cq % Q_LORA == 0 and (col_cq + Q_LORA) % KV_LORA == 0
    return pl.pallas_call(
        _latent_norm_body,
        out_shape=[jax.ShapeDtypeStruct((n, Q_LORA), BF16), jax.ShapeDtypeStruct((n, KV_LORA), BF16)],
        grid=(n // tm,),
        in_specs=[pl.BlockSpec((tm, Q_LORA), lambda i: (i, col_cq // Q_LORA)),
                  pl.BlockSpec((tm, KV_LORA), lambda i: (i, (col_cq + Q_LORA) // KV_LORA)),
                  pl.BlockSpec((1, Q_LORA), lambda i: (0, 0)), pl.BlockSpec((1, KV_LORA), lambda i: (0, 0))],
        out_specs=[pl.BlockSpec((tm, Q_LORA), lambda i: (i, 0)), pl.BlockSpec((tm, KV_LORA), lambda i: (i, 0))],
        compiler_params=_cparams(("parallel",)),
        name="latent_norm",
    )(proj, proj, cq_norm.reshape(1, Q_LORA), ckv_norm.reshape(1, KV_LORA))


def _mla_q_body(a_ref, b_ref, cos_ref, sin_ref, qo_ref, *, heads):
    cos, sin = cos_ref[...], sin_ref[...]
    low = (lax.broadcasted_iota(jnp.int32, cos.shape, 1) % 64) < 32
    scale = (QK_NOPE + QK_ROPE) ** -0.5 * LOG2E
    r = jnp.dot(a_ref[...], b_ref[...], preferred_element_type=F32)
    for h in range(heads):
        a = slice(h * MLA_QK_PAD, h * MLA_QK_PAD + QK_NOPE)
        b = slice(h * MLA_QK_PAD + QK_NOPE, (h + 1) * MLA_QK_PAD)
        qo_ref[:, a] = (r[:, a] * scale).astype(qo_ref.dtype)
        qo_ref[:, b] = (_rope_quarter(r[:, b], cos, sin, low) * scale).astype(qo_ref.dtype)


def _mla_kv_body(a_ref, b_ref, kpe_ref, cos_ref, sin_ref, ko_ref, vto_ref, *, heads):
    cos, sin = cos_ref[...], sin_ref[...]
    low = (lax.broadcasted_iota(jnp.int32, cos.shape, 1) % 64) < 32
    k_rope = _rope_quarter(kpe_ref[...], cos, sin, low).astype(ko_ref.dtype)
    r = jnp.dot(a_ref[...], b_ref[...], preferred_element_type=F32)
    for h in range(heads):
        a = slice(h * MLA_QK_PAD, h * MLA_QK_PAD + QK_NOPE)
        b = slice(h * MLA_QK_PAD + QK_NOPE, (h + 1) * MLA_QK_PAD)
        ko_ref[:, a] = r[:, a].astype(ko_ref.dtype)
        ko_ref[:, b] = k_rope
        vto_ref[h * V_D:(h + 1) * V_D, :] = r[:, b].T.astype(vto_ref.dtype)


def mla_up_proj(cqn, ckvn, w_uq, w_ukv, layer, proj, col_kpe, cos, sin, n_heads, tm=512, heads_per_step=4):
    n = cqn.shape[0]
    tm = min(tm, n)
    hs = min(heads_per_step, n_heads)
    assert n_heads % hs == 0 and col_kpe % 128 == 0
    tn = hs * MLA_QK_PAD
    wd = n_heads * MLA_QK_PAD
    grid = (n // tm, n_heads // hs)
    tab = pl.BlockSpec((tm, 128), lambda i, j: (i, 0))
    q = pl.pallas_call(
        functools.partial(_mla_q_body, heads=hs),
        out_shape=jax.ShapeDtypeStruct((n, wd), BF16),
        grid=grid,
        in_specs=[pl.BlockSpec((tm, Q_LORA), lambda i, j: (i, 0)),
                  _layer_spec(w_uq, layer, (Q_LORA, tn), lambda i, j: (0, j)),
                  tab, tab],
        out_specs=pl.BlockSpec((tm, tn), lambda i, j: (i, j)),
        compiler_params=_cparams(("parallel", "parallel")),
        name="mla_q_proj",
    )(cqn, w_uq, cos, sin)
    k, vt = pl.pallas_call(
        functools.partial(_mla_kv_body, heads=hs),
        out_shape=[jax.ShapeDtypeStruct((n, wd), BF16), jax.ShapeDtypeStruct((n_heads * V_D, n), BF16)],
        grid=grid,
        in_specs=[pl.BlockSpec((tm, KV_LORA), lambda i, j: (i, 0)),
                  _layer_spec(w_ukv, layer, (KV_LORA, tn), lambda i, j: (0, j)),
                  pl.BlockSpec((tm, 128), lambda i, j: (i, col_kpe // 128)), tab, tab],
        out_specs=[pl.BlockSpec((tm, tn), lambda i, j: (i, j)),
                   pl.BlockSpec((hs * V_D, tm), lambda i, j: (j, i))],
        compiler_params=_cparams(("parallel", "parallel")),
        name="mla_kv_proj",
    )(ckvn, w_ukv, proj, cos, sin)
    return q, k, vt


def _inv_freq(dim):
    return (np.float32(ROPE_THETA) ** (-np.arange(0, dim, 2, dtype=np.float32) / np.float32(dim))).astype(np.float32)


def _angles(pos, dim):
    ang = pos.astype(F32)[:, None] * jnp.asarray(_inv_freq(dim))[None, :]
    return jnp.cos(ang), jnp.sin(ang)


def _rope_tables(seq_lens):
    tmax = max(seq_lens)
    pos = jnp.arange(tmax)
    per_token = lambda t: jnp.concatenate([t[:n] for n in seq_lens], axis=0)
    c, s = _angles(pos, HEAD_DIM)
    ret = (per_token(jnp.concatenate([c, c], -1)), per_token(jnp.concatenate([-s, s], -1)))
    rows = -(-tmax // GRID_W)
    cr, sr = (jnp.repeat(t, GRID_W, axis=0)[:tmax] for t in _angles(jnp.arange(rows), HEAD_DIM // 2))
    cc, sc = (jnp.tile(t, (rows, 1))[:tmax] for t in _angles(jnp.arange(GRID_W), HEAD_DIM // 2))
    axial = (per_token(jnp.concatenate([cr, cr, cc, cc], -1)), per_token(jnp.concatenate([-sr, sr, -sc, sc], -1)))
    c1, s1 = _angles(pos, QK_ROPE)
    one, zero = jnp.ones_like(c1), jnp.zeros_like(s1)
    mla = (per_token(jnp.concatenate([c1, c1, one, one], -1)), per_token(jnp.concatenate([-s1, s1, zero, zero], -1)))
    return ret, axial, mla


def _pad_last(w, width):
    return jnp.pad(w, ((0, 0),) * (w.ndim - 1) + ((0, width - w.shape[-1]),))


def _mix_even(h, w_in, i, log_1m_gamma, rel_bias, ret_tab, seqs):
    seq_starts, seq_ends = seqs
    n_heads = w_in.shape[-1] // (7 * HEAD_DIM)
    w = n_heads * HEAD_DIM
    proj = matmul(h, w_in, layer=i, out_dtype=F32, name="mm_in_even")
    assert all(window // (2 * dil) == WIN_HALF for window, dil in DIL_PATTERNS)
    dils = tuple(dil for _, dil in DIL_PATTERNS)
    oa = window_attention(proj, rel_bias, dils, n_heads, seq_starts, seq_ends)
    log_gamma = jnp.log1p(-jnp.exp(log_1m_gamma.astype(F32)))
    ob = retention(proj, 3 * w, log_gamma, ret_tab[0], ret_tab[1], n_heads, seq_starts, seq_ends)
    return oa, ob


def _mix_odd(h, w_in, in_odd, i, qk_norm, cq_norm, ckv_norm, w_uq, w_ukv, ax_tab, mla_tab, groups, tiles):
    wkv = KV_C * HEAD_DIM
    wq = in_odd - 2 * wkv - Q_LORA - KV_LORA - QK_ROPE
    hq = wq // HEAD_DIM
    proj = matmul(h, w_in, layer=i, out_dtype=F32, name="mm_in_odd")
    qc, kc, vtc = gqa_prep(proj, qk_norm, ax_tab[0], ax_tab[1], hq, KV_C)
    col_cq = wq + 2 * wkv
    cqn, ckvn = latent_norm(proj, col_cq, cq_norm, ckv_norm)
    n_hd = w_uq.shape[-1] // MLA_QK_PAD
    assert V_D == MLA_QK_PAD - QK_NOPE and w_ukv.shape[-1] == n_hd * MLA_QK_PAD
    q_d, k_d, vt_d = mla_up_proj(cqn, ckvn, w_uq, w_ukv, i, proj, col_cq + Q_LORA + KV_LORA,
                                 mla_tab[0], mla_tab[1], n_hd)
    ocs, ods = [], []
    for row0, batch, seq_len in groups:
        ocs.append(flash_attention(qc, kc, vtc, n_kv_heads=KV_C, g_heads=hq // KV_C, dqk=HEAD_DIM, dv=HEAD_DIM,
                                   row0=row0, batch=batch, seq_len=seq_len, tq=tiles[0], tk=tiles[1],
                                   nsplit=tiles[2], name="attn_gqa"))
        ods.append(flash_attention(q_d, k_d, vt_d, n_kv_heads=n_hd, g_heads=1, dqk=MLA_QK_PAD, dv=V_D,
                                   row0=row0, batch=batch, seq_len=seq_len, tq=tiles[3], tk=tiles[4],
                                   nsplit=tiles[5], name="attn_mla"))
    return jnp.concatenate(ocs, 0), jnp.concatenate(ods, 0)


def _conv_ffn(h, w_up, conv_w, w_down, layer, x, gate, seqs):
    seq_starts, seq_ends = seqs
    dffp = w_down.shape[-2]
    act = ff_up_conv_gate(h, w_up, conv_w, layer, dffp, seq_starts, seq_ends)
    return matmul(act, w_down, layer=layer, out_dtype=F32, tn=1024, tk=2816, resid=x, gate=gate,
                  seq_starts=seq_starts, name="mm_ff_down")


def _trunk(x_prompt, x_sample, c_prompt, c_sample, rel_bias, norm_w, w_mod, b_mod, w_in_even,
           ret_log_1m_gamma, w_out_even, w_in_odd, c_qk_norm, mla_cq_norm, mla_ckv_norm, w_uq,
           w_ukv, w_out_odd, w_ff_up, conv_ff, w_ff_down, final_norm_w, *, attn_tiles, ff_mult):
    bp, tp, d = x_prompt.shape
    bs, ts, _ = x_sample.shape
    depth = norm_w.shape[0]
    seq_lens = (tp,) * bp + (ts,) * bs
    seq_starts = tuple(int(v) for v in np.cumsum((0,) + seq_lens[:-1]))
    seq_ends = tuple(int(v) for v in np.cumsum(seq_lens))
    seqs = (seq_starts, seq_ends)
    groups = ((0, bp, tp), (bp * tp, bs, ts))
    n_seq = len(seq_lens)

    x = jnp.concatenate([x_prompt.reshape(bp * tp, d), x_sample.reshape(bs * ts, d)], axis=0)
    c = jnp.concatenate([c_prompt, c_sample], axis=0)
    c_act = jnp.pad(jax.nn.silu(c), ((0, 8 - n_seq), (0, 0))).astype(BF16)
    ret_tab, ax_tab, mla_tab = _rope_tables(seq_lens)

    in_odd = w_in_odd.shape[-1]
    w_in_even_b = w_in_even.astype(BF16)
    w_in_odd_b = _pad_last(w_in_odd, -(-in_odd // 512) * 512).astype(BF16)
    w_out_b = (w_out_even.astype(BF16), w_out_odd.astype(BF16))
    n_hd = w_uq.shape[-1] // (QK_NOPE + QK_ROPE)
    w_uq_b = _pad_last(w_uq.reshape(-1, Q_LORA, n_hd, QK_NOPE + QK_ROPE), MLA_QK_PAD)
    w_uq_b = w_uq_b.reshape(-1, Q_LORA, n_hd * MLA_QK_PAD).astype(BF16)
    w_ukv_b = w_ukv.astype(BF16)
    dff = w_ff_down.shape[-2]
    dffp = -(-dff // ff_mult) * ff_mult
    w_up_b = ff_interleave(w_ff_up, dff, dffp).astype(BF16)
    conv_p = ff_interleave(conv_ff, dff, dffp)
    w_down_b = jnp.pad(w_ff_down, ((0, 0), (0, dffp - dff), (0, 0))).astype(BF16)
    b_mod3 = b_mod.reshape(depth, 1, -1)

    for layer in range(depth):
        mod = matmul(c_act, w_mod, layer=layer, out_dtype=F32, tm=8, tn=2048, tk=1024, bias=b_mod3, name="mm_mod")
        mod6 = mod[:n_seq].reshape(n_seq * 6, 1, d)
        gate_of = lambda ci: mod6.reshape(n_seq, 6, d)[:, ci].reshape(n_seq, 1, d)
        h = norm_mod(x, norm_w[layer, 0], mod6, 1, 0, seq_starts)
        i = layer // 2
        if layer % 2 == 0:
            mix = _mix_even(h, w_in_even_b, i, ret_log_1m_gamma[i], rel_bias, ret_tab, seqs)
        else:
            mix = _mix_odd(h, w_in_odd_b, in_odd, i, c_qk_norm[i], mla_cq_norm[i], mla_ckv_norm[i], w_uq_b, w_ukv_b,
                           ax_tab, mla_tab, groups, attn_tiles)
        x = matmul_cat_resid(mix[0], mix[1], w_out_b[layer % 2], i, x, gate_of(2), seq_starts)
        h = norm_mod(x, norm_w[layer, 1], mod6, 4, 3, seq_starts)
        x = _conv_ffn(h, w_up_b, conv_p, w_down_b, layer, x, gate_of(5), seqs)
    y_p, y_s = final_norm(x, final_norm_w, bp * tp)
    return y_p.reshape(bp, tp, d), y_s.reshape(bs, ts, d)


def kernel(x_prompt, x_sample, c_prompt, c_sample, rel_bias, norm_w, w_mod, b_mod, w_in_even, ret_log_1m_gamma, w_out_even, w_in_odd, c_qk_norm, mla_cq_norm, mla_ckv_norm, w_uq, w_ukv, w_out_odd, w_ff_up, conv_ff, w_ff_down, final_norm_w):
    return _trunk(x_prompt, x_sample, c_prompt, c_sample, rel_bias, norm_w, w_mod, b_mod, w_in_even,
                  ret_log_1m_gamma, w_out_even, w_in_odd, c_qk_norm, mla_cq_norm, mla_ckv_norm, w_uq,
                  w_ukv, w_out_odd, w_ff_up, conv_ff, w_ff_down, final_norm_w,
                  attn_tiles=(512, 4096, 8, 2048, 4096, 8), ff_mult=1024)
```

```python
import functools
import math

import numpy as np
import jax
import jax.numpy as jnp
from jax import lax
from jax.experimental import pallas as pl
from jax.experimental.pallas import tpu as pltpu

F32 = jnp.float32
BF16 = jnp.bfloat16

HEAD_DIM = 128
GRID_W = 64
DIL_PATTERNS = ((128, 1), (512, 4), (2048, 16))
RET_CHUNK = 128
KV_C = 4
Q_LORA = 768
KV_LORA = 256
QK_NOPE = 128
QK_ROPE = 64
V_D = 128
MLA_QK_PAD = 256
N_BUCKETS = 32
MAX_DIST = 1024
ROPE_THETA = 10000.0
EPS = 1e-6
NEG = -1e30
LOG2E = math.log2(math.e)
FLASH_HEADROOM = 60.0
FLASH_SEED_KEYS = 256
FLASH_TWO_PASS_KEYS = 2048

WIN_HALF = 64
WIN_TQ = 128
BF16_ROWS = 16
VMEM_LIMIT = 56 * 1024 * 1024


def _cparams(sem):
    return pltpu.CompilerParams(dimension_semantics=sem, vmem_limit_bytes=VMEM_LIMIT)


def _seq_id(row, seq_starts):
    s = 0
    for st in seq_starts[1:]:
        s = s + (row >= st).astype(jnp.int32)
    return s


def _is_any(row, marks):
    r = row == marks[0]
    for m in marks[1:]:
        r = jnp.logical_or(r, row == m)
    return r


def _largest_tile(n, cap):
    best = 128
    for t in range(128, cap + 1, 128):
        if n % t == 0:
            best = t
    return best


def _mm_body(*refs, nk, mode):
    a_ref, b_ref = refs[0], refs[1]
    if mode == "plain":
        extra, (o_ref, acc_ref) = (), refs[2:]
    elif mode == "bias":
        extra, (o_ref, acc_ref) = refs[2:3], refs[3:]
    else:
        extra, (o_ref, acc_ref) = refs[2:4], refs[4:]

    def epilogue(r):
        if mode == "bias":
            r = r + extra[0][...]
        elif mode == "resid":
            r = extra[0][...] + extra[1][0] * r
        o_ref[...] = r.astype(o_ref.dtype)

    def prod():
        return jnp.dot(a_ref[...].astype(BF16), b_ref[...].astype(BF16), preferred_element_type=F32)

    if nk == 1:
        epilogue(prod())
        return
    k = pl.program_id(2)

    @pl.when(k == 0)
    def _():
        acc_ref[...] = prod()

    @pl.when(jnp.logical_and(k > 0, k < nk - 1))
    def _():
        acc_ref[...] += prod()

    @pl.when(k == nk - 1)
    def _():
        epilogue(acc_ref[...] + prod())


def _layer_spec(w, layer, block, index_map):
    if w.ndim == len(block):
        return pl.BlockSpec(block, index_map)
    assert w.ndim == len(block) + 1
    return pl.BlockSpec((None,) + tuple(block), lambda *g: (layer,) + tuple(index_map(*g)))


def matmul(a, b, *, out_dtype, layer=None, tm=1024, tn=512, tk=4096, bias=None, resid=None, gate=None,
           seq_starts=None, name="mm"):
    m, kd = a.shape
    n = b.shape[-1]
    tm, tn, tk = min(tm, m), _largest_tile(n, tn), _largest_tile(kd, tk)
    assert m % tm == 0 and n % tn == 0 and kd % tk == 0 and b.shape[-2] == kd, (a.shape, b.shape, tm, tn, tk)
    nk = kd // tk
    in_specs = [pl.BlockSpec((tm, tk), lambda i, j, k: (i, k)),
                _layer_spec(b, layer, (tk, tn), lambda i, j, k: (k, j))]
    args = [a, b]
    if bias is not None:
        mode = "bias"
        in_specs.append(_layer_spec(bias, layer, (1, tn), lambda i, j, k: (0, j)))
        args.append(bias)
    elif resid is not None:
        mode = "resid"
        in_specs.append(pl.BlockSpec((tm, tn), lambda i, j, k: (i, j)))
        in_specs.append(pl.BlockSpec((1, 1, tn), lambda i, j, k: (_seq_id(i * tm, seq_starts), 0, j)))
        args += [resid, gate]
    else:
        mode = "plain"
    return pl.pallas_call(
        functools.partial(_mm_body, nk=nk, mode=mode),
        out_shape=jax.ShapeDtypeStruct((m, n), out_dtype),
        grid=(m // tm, n // tn, nk),
        in_specs=in_specs,
        out_specs=pl.BlockSpec((tm, tn), lambda i, j, k: (i, j)),
        scratch_shapes=[pltpu.VMEM((tm, tn) if nk > 1 else (8, 128), F32)],
        compiler_params=_cparams(("parallel", "parallel", "arbitrary")),
        name=name,
    )(*args)


def _mm_cat_body(a1_ref, a2_ref, b_ref, x_ref, g_ref, o_ref, *, k1):
    r = jnp.dot(a1_ref[...], b_ref[0:k1], preferred_element_type=F32)
    r = r + jnp.dot(a2_ref[...], b_ref[k1:], preferred_element_type=F32)
    o_ref[...] = (x_ref[...] + g_ref[0] * r).astype(o_ref.dtype)


def matmul_cat_resid(a1, a2, b, layer, resid, gate, seq_starts, tm=1024, tn=512, name="mm_out"):
    m, k1 = a1.shape
    k2 = a2.shape[1]
    n = b.shape[-1]
    tm, tn = min(tm, m), _largest_tile(n, tn)
    assert b.shape[-2] == k1 + k2 and m % tm == 0 and k1 % BF16_ROWS == 0
    return pl.pallas_call(
        functools.partial(_mm_cat_body, k1=k1),
        out_shape=jax.ShapeDtypeStruct((m, n), resid.dtype),
        grid=(m // tm, n // tn),
        in_specs=[pl.BlockSpec((tm, k1), lambda i, j: (i, 0)),
                  pl.BlockSpec((tm, k2), lambda i, j: (i, 0)),
                  _layer_spec(b, layer, (k1 + k2, tn), lambda i, j: (0, j)),
                  pl.BlockSpec((tm, tn), lambda i, j: (i, j)),
                  pl.BlockSpec((1, 1, tn), lambda i, j: (_seq_id(i * tm, seq_starts), 0, j))],
        out_specs=pl.BlockSpec((tm, tn), lambda i, j: (i, j)),
        compiler_params=_cparams(("parallel", "parallel")),
        name=name,
    )(a1, a2, b, resid, gate)


def _ffup_body(a_ref, ap_ref, an_ref, bg_ref, bv_ref, cwg_ref, cwv_ref, o_ref, ext_ref, *,
               tm, seq_starts, seq_ends):
    i = pl.program_id(0)
    halo = BF16_ROWS

    @pl.when(pl.program_id(1) == 0)
    def _():
        ext_ref[0:tm] = a_ref[...]
        hrow = lax.broadcasted_iota(jnp.int32, ap_ref.shape, 0)
        ext_ref[tm:] = jnp.where(hrow < halo // 2, an_ref[...], ap_ref[...])

    first = _is_any(i * tm, seq_starts)
    last = _is_any((i + 1) * tm, seq_ends)
    ext = ext_ref[...]
    rows = tm + halo
    row = lax.broadcasted_iota(jnp.int32, (tm, o_ref.shape[1]), 0)
    kill_prev = jnp.logical_and(row == 0, first)
    kill_next = jnp.logical_and(row == tm - 1, last)

    def conv(b_ref, cw_ref):
        u = jnp.dot(ext, b_ref[...], preferred_element_type=F32)
        u_prev = jnp.where(kill_prev, 0.0, pltpu.roll(u, 1, axis=0)[0:tm])
        u_next = jnp.where(kill_next, 0.0, pltpu.roll(u, rows - 1, axis=0)[0:tm])
        cw = cw_ref[...]
        return u_prev * cw[0:1] + u[0:tm] * cw[1:2] + u_next * cw[2:3]

    g = conv(bg_ref, cwg_ref)
    val = conv(bv_ref, cwv_ref)
    o_ref[...] = (g * jax.nn.sigmoid(g) * val).astype(o_ref.dtype)


def ff_up_conv_gate(h, w_up, conv_w, layer, dff, seq_starts, seq_ends, tm=1024, tn=512):
    n, d = h.shape
    tm, tn = min(tm, n), _largest_tile(dff, tn)
    nj = dff // tn
    hb = tm // BF16_ROWS
    nhb = n // BF16_ROWS
    prev = lambda i: jnp.maximum(i * hb - 1, 0)
    nxt = lambda i: jnp.minimum((i + 1) * hb, nhb - 1)
    return pl.pallas_call(
        functools.partial(_ffup_body, tm=tm, seq_starts=seq_starts, seq_ends=seq_ends),
        out_shape=jax.ShapeDtypeStruct((n, dff), BF16),
        grid=(n // tm, nj),
        in_specs=[pl.BlockSpec((tm, d), lambda i, j: (i, 0)),
                  pl.BlockSpec((BF16_ROWS, d), lambda i, j: (prev(i), 0)),
                  pl.BlockSpec((BF16_ROWS, d), lambda i, j: (nxt(i), 0)),
                  _layer_spec(w_up, layer, (d, tn), lambda i, j: (0, j)),
                  _layer_spec(w_up, layer, (d, tn), lambda i, j: (0, j + nj)),
                  _layer_spec(conv_w, layer, (3, tn), lambda i, j: (0, j)),
                  _layer_spec(conv_w, layer, (3, tn), lambda i, j: (0, j + nj))],
        out_specs=pl.BlockSpec((tm, tn), lambda i, j: (i, j)),
        scratch_shapes=[pltpu.VMEM((tm + BF16_ROWS, d), BF16)],
        compiler_params=_cparams(("parallel", "arbitrary")),
        name="ff_up_conv_gate",
    )(h, h, h, w_up, w_up, conv_w, conv_w)


def _normmod_body(x_ref, w_ref, sc_ref, sh_ref, o_ref):
    x = x_ref[...]
    y = x * lax.rsqrt(jnp.mean(x * x, axis=-1, keepdims=True) + EPS) * w_ref[...]
    o_ref[...] = (y * (1.0 + sc_ref[0]) + sh_ref[0]).astype(o_ref.dtype)


def norm_mod(x, w, mod6, scale_idx, shift_idx, seq_starts, tm=512):
    n, d = x.shape
    tm = min(tm, n)
    return pl.pallas_call(
        _normmod_body,
        out_shape=jax.ShapeDtypeStruct((n, d), BF16),
        grid=(n // tm,),
        in_specs=[pl.BlockSpec((tm, d), lambda i: (i, 0)),
                  pl.BlockSpec((1, d), lambda i: (0, 0)),
                  pl.BlockSpec((1, 1, d), lambda i: (_seq_id(i * tm, seq_starts) * 6 + scale_idx, 0, 0)),
                  pl.BlockSpec((1, 1, d), lambda i: (_seq_id(i * tm, seq_starts) * 6 + shift_idx, 0, 0))],
        out_specs=pl.BlockSpec((tm, d), lambda i: (i, 0)),
        compiler_params=_cparams(("parallel",)),
        name="norm_mod",
    )(x, w.reshape(1, d), mod6, mod6)


def _final_norm_body(x_ref, w_ref, oa_ref, ob_ref, *, na_tiles):
    x = x_ref[...]
    y = x * lax.rsqrt(jnp.mean(x * x, axis=-1, keepdims=True) + EPS) * w_ref[...]
    i = pl.program_id(0)

    @pl.when(i < na_tiles)
    def _():
        oa_ref[...] = y

    @pl.when(i >= na_tiles)
    def _():
        ob_ref[...] = y


def final_norm(x, w, n_first, tm=256):
    n, d = x.shape
    tm = min(tm, n_first, n - n_first)
    assert n_first % tm == 0 and n % tm == 0
    na = n_first // tm
    return pl.pallas_call(
        functools.partial(_final_norm_body, na_tiles=na),
        out_shape=[jax.ShapeDtypeStruct((n_first, d), F32), jax.ShapeDtypeStruct((n - n_first, d), F32)],
        grid=(n // tm,),
        in_specs=[pl.BlockSpec((tm, d), lambda i: (i, 0)), pl.BlockSpec((1, d), lambda i: (0, 0))],
        out_specs=[pl.BlockSpec((tm, d), lambda i: (jnp.minimum(i, na - 1), 0)),
                   pl.BlockSpec((tm, d), lambda i: (jnp.maximum(i - na, 0), 0))],
        compiler_params=_cparams(("arbitrary",)),
        name="final_norm",
    )(x, w.reshape(1, d))


def _flash_body(q_ref, k_ref, vt_ref, o_ref, m_ref, l_ref, acc_ref, pv_ref, mc_ref, *,
                g_heads, dqk, dv, nkv, tq, nsplit):
    j = pl.program_id(3)
    cs = (g_heads * tq) // nsplit
    cols = [slice(c * cs, (c + 1) * cs) for c in range(nsplit)]
    if g_heads == 1:
        qcat = q_ref[...]
    else:
        qcat = jnp.concatenate([q_ref[:, g * dqk:(g + 1) * dqk] for g in range(g_heads)], axis=0)

    @pl.when(j == 0)
    def _():
        k_head = k_ref[0:FLASH_SEED_KEYS, :]
        for sl in cols:
            s0 = lax.dot_general(k_head, qcat[sl], (((1,), (1,)), ((), ())), preferred_element_type=F32)
            m_ref[:, sl] = jnp.max(s0, axis=0, keepdims=True)
        l_ref[...] = jnp.zeros(l_ref.shape, F32)
        acc_ref[...] = jnp.zeros(acc_ref.shape, F32)

    k = k_ref[...]
    vt1 = jnp.concatenate([vt_ref[...], jnp.ones((BF16_ROWS, vt_ref.shape[1]), BF16)], axis=0)
    qk = lambda sl: lax.dot_general(k, qcat[sl], (((1,), (1,)), ((), ())), preferred_element_type=F32)

    @pl.when(j < nkv)
    def _():
        ps = []
        for sl in cols:
            s = qk(sl)
            mc_ref[:, sl] = jnp.max(s, axis=0, keepdims=True)
            ps.append(jnp.exp2(s - m_ref[:, sl]).astype(BF16))
        for p, sl in zip(ps, cols):
            pv_ref[:, sl] = jnp.dot(vt1, p, preferred_element_type=F32)

    single_pass_ok = jnp.max(mc_ref[...] - m_ref[...]) <= FLASH_HEADROOM

    @pl.when(single_pass_ok)
    def _():
        m_old = m_ref[...]
        m_new = jnp.maximum(m_old, mc_ref[...])
        alpha = jnp.exp2(m_old - m_new)
        l_ref[...] = alpha * (l_ref[...] + pv_ref[dv:dv + 1, :])
        acc_ref[...] = alpha * (acc_ref[...] + pv_ref[0:dv, :])
        m_ref[...] = m_new

    @pl.when(jnp.logical_not(single_pass_ok))
    def _():
        tk = k_ref.shape[0]
        step = min(tk, FLASH_TWO_PASS_KEYS)
        for k0 in range(0, tk, step):
            k_sub, vt_sub = k[k0:k0 + step], vt1[:, k0:k0 + step]
            scores = [lax.dot_general(k_sub, qcat[sl], (((1,), (1,)), ((), ())), preferred_element_type=F32)
                      for sl in cols]
            for s, sl in zip(scores, cols):
                m_prev = m_ref[:, sl]
                m_new = jnp.maximum(m_prev, jnp.max(s, axis=0, keepdims=True))
                alpha = jnp.exp2(m_prev - m_new)
                p = jnp.exp2(s - m_new)
                pv = jnp.dot(vt_sub, p.astype(BF16), preferred_element_type=F32)
                l_ref[:, sl] = alpha * l_ref[:, sl] + pv[dv:dv + 1]
                acc_ref[:, sl] = alpha * acc_ref[:, sl] + pv[:dv]
                m_ref[:, sl] = m_new

    @pl.when(j == nkv - 1)
    def _():
        o_t = acc_ref[...] / l_ref[...]
        for g in range(g_heads):
            o_ref[:, g * dv:(g + 1) * dv] = o_t[:, g * tq:(g + 1) * tq].T.astype(o_ref.dtype)


def flash_attention(q, k, vt, *, n_kv_heads, g_heads, dqk, dv, row0, batch, seq_len, tq, tk, nsplit, name):
    tq, tk = min(tq, seq_len), min(tk, seq_len)
    nq, nkv = seq_len // tq, seq_len // tk
    qb0, kb0 = row0 // tq, row0 // tk
    ncol = g_heads * tq
    nsplit = min(nsplit, ncol // 128)
    return pl.pallas_call(
        functools.partial(_flash_body, g_heads=g_heads, dqk=dqk, dv=dv, nkv=nkv, tq=tq, nsplit=nsplit),
        out_shape=jax.ShapeDtypeStruct((batch * seq_len, n_kv_heads * g_heads * dv), BF16),
        grid=(batch, n_kv_heads, nq, nkv),
        in_specs=[pl.BlockSpec((tq, g_heads * dqk), lambda b, h, i, j: (qb0 + b * nq + i, h)),
                  pl.BlockSpec((tk, dqk), lambda b, h, i, j: (kb0 + b * nkv + j, h)),
                  pl.BlockSpec((dv, tk), lambda b, h, i, j: (h, kb0 + b * nkv + j))],
        out_specs=pl.BlockSpec((tq, g_heads * dv), lambda b, h, i, j: (b * nq + i, h)),
        scratch_shapes=[pltpu.VMEM((1, ncol), F32), pltpu.VMEM((1, ncol), F32), pltpu.VMEM((dv, ncol), F32),
                        pltpu.VMEM((dv + BF16_ROWS, ncol), F32), pltpu.VMEM((1, ncol), F32)],
        compiler_params=_cparams(("parallel", "parallel", "parallel", "arbitrary")),
        name=name,
    )(q, k, vt)


def _win_body(q_ref, kp_ref, kc_ref, kn_ref, vp_ref, vc_ref, vn_ref, bias_ref, o_ref,
              kext_ref, vext_ref, *scr, tb, tq, dils, seq_starts, seq_ends):
    i = pl.program_id(0)
    first = _is_any(i * tb, seq_starts)
    last = _is_any((i + 1) * tb, seq_ends)
    hmax = WIN_HALF * max(dils)
    for ext_ref, p_ref, c_ref, n_ref in ((kext_ref, kp_ref, kc_ref, kn_ref), (vext_ref, vp_ref, vc_ref, vn_ref)):
        ext_ref[0:hmax] = p_ref[...]
        ext_ref[hmax:hmax + tb] = c_ref[...]
        ext_ref[hmax + tb:] = n_ref[...]
    nkeys = tq + 2 * WIN_HALF
    col = lax.broadcasted_iota(jnp.int32, (tq, nkeys), 1)
    bad_first = jnp.logical_and(col < WIN_HALF, first)
    bad_last = jnp.logical_and(col >= WIN_HALF + tq, last)
    scale = HEAD_DIM ** -0.5
    o_scr, l_scr = scr[:len(dils)], scr[len(dils):]
    for p, dil in enumerate(dils):
        nsub = tb // (tq * dil)
        bias = bias_ref[p]
        for sub in range(nsub):
            for r in range(dil):
                start = sub * tq * dil + r
                kstart = hmax - WIN_HALF * dil + start
                q = (q_ref[pl.ds(start, tq, stride=dil), :] * scale).astype(BF16)
                k = kext_ref[pl.ds(kstart, nkeys, stride=dil), :].astype(BF16)
                v = vext_ref[pl.ds(kstart, nkeys, stride=dil), :].astype(BF16)
                s = lax.dot_general(q, k, (((1,), (1,)), ((), ())), preferred_element_type=F32) + bias
                if sub == 0:
                    s = jnp.where(bad_first, NEG, s)
                if sub == nsub - 1:
                    s = jnp.where(bad_last, NEG, s)
                m = jnp.max(s, axis=-1, keepdims=True)
                e = jnp.exp(s - m)
                den = jnp.sum(e, axis=-1, keepdims=True)
                o = jnp.dot(e.astype(BF16), v, preferred_element_type=F32) / den
                o_scr[p][pl.ds(start, tq, stride=dil), :] = o
                l_scr[p][pl.ds(start, tq, stride=dil), :] = jnp.broadcast_to(m + jnp.log(den), (tq, HEAD_DIM))
    lses = [l[...] for l in l_scr]
    m = functools.reduce(jnp.maximum, lses)
    ws = [jnp.exp(l - m) for l in lses]
    num = functools.reduce(lambda a, b: a + b, [w * o[...] for w, o in zip(ws, o_scr)])
    o_ref[...] = (num / functools.reduce(lambda a, b: a + b, ws)).astype(o_ref.dtype)


def _t5_bucket(rel):
    nb = N_BUCKETS // 2
    max_exact = nb // 2
    n = jnp.abs(rel)
    log_ratio = jnp.log(jnp.maximum(n, 1).astype(F32) / max_exact) / math.log(MAX_DIST / max_exact)
    large = jnp.minimum(max_exact + (log_ratio * (nb - max_exact)).astype(jnp.int32), nb - 1)
    return jnp.where(rel > 0, nb, 0) + jnp.where(n < max_exact, n, large)


def _window_bias(rel_bias, dil, tq):
    kj = jnp.arange(tq + 2 * WIN_HALF)
    rel = kj[None, :] - WIN_HALF - jnp.arange(tq)[:, None]
    onehot = (_t5_bucket(rel * dil)[:, :, None] == jnp.arange(N_BUCKETS)).astype(F32)
    bias = jnp.einsum("qkb,bh->qkh", onehot, rel_bias.astype(F32), precision=lax.Precision.HIGHEST)
    bias = jnp.where((jnp.abs(rel) <= WIN_HALF)[:, :, None], bias, NEG)
    return bias.transpose(2, 0, 1)


def window_attention(proj, rel_bias, dils, n_heads, seq_starts, seq_ends):
    n = proj.shape[0]
    tq = WIN_TQ
    hmax = WIN_HALF * max(dils)
    tb = tq * max(dils)
    assert all(s % tb == 0 for s in seq_starts + seq_ends) and tb % hmax == 0
    bias = jnp.stack([_window_bias(rel_bias, dil, tq) for dil in dils])
    hb = tb // hmax
    nhb = n // hmax
    prev = lambda i: jnp.maximum(i * hb - 1, 0)
    nxt = lambda i: jnp.minimum((i + 1) * hb, nhb - 1)
    cur_spec = lambda g: pl.BlockSpec((tb, HEAD_DIM), lambda i, h: (i, g * n_heads + h))
    prev_spec = lambda g: pl.BlockSpec((hmax, HEAD_DIM), lambda i, h: (prev(i), g * n_heads + h))
    next_spec = lambda g: pl.BlockSpec((hmax, HEAD_DIM), lambda i, h: (nxt(i), g * n_heads + h))
    np_ = len(dils)
    return pl.pallas_call(
        functools.partial(_win_body, tb=tb, tq=tq, dils=dils, seq_starts=seq_starts, seq_ends=seq_ends),
        out_shape=jax.ShapeDtypeStruct((n, n_heads * HEAD_DIM), BF16),
        grid=(n // tb, n_heads),
        in_specs=[cur_spec(0), prev_spec(1), cur_spec(1), next_spec(1),
                  prev_spec(2), cur_spec(2), next_spec(2),
                  pl.BlockSpec((np_, None, tq, tq + 2 * WIN_HALF), lambda i, h: (0, h, 0, 0))],
        out_specs=pl.BlockSpec((tb, HEAD_DIM), lambda i, h: (i, h)),
        scratch_shapes=[pltpu.VMEM((tb + 2 * hmax, HEAD_DIM), F32)] * 2 + [pltpu.VMEM((tb, HEAD_DIM), F32)] * (2 * np_),
        compiler_params=_cparams(("parallel", "parallel")),
        name="win_attn",
    )(proj, proj, proj, proj, proj, proj, proj, bias)


def _rope_half(x, cos, sin_signed):
    return x * cos + pltpu.roll(x, HEAD_DIM // 2, axis=1) * sin_signed


def _ret_bwd_body(lg_ref, q_ref, k_ref, v_ref, cos_ref, sin_ref, o_ref, r_ref, tab_ref, *,
                  n_heads, n_chunks, seq_ends):
    c = RET_CHUNK
    g = n_chunks - 1 - pl.program_id(0)

    @pl.when(pl.program_id(0) == 0)
    def _():
        n = lax.broadcasted_iota(jnp.int32, (c, HEAD_DIM), 0).astype(F32)
        for h in range(n_heads):
            lg = lg_ref[1, h]
            tab_ref[0, h] = jnp.exp(lg * (c - n))
            tab_ref[1, h] = jnp.exp(lg * n)
            tab_ref[2, h] = jnp.exp(jnp.full((HEAD_DIM, HEAD_DIM), lg * c, F32))

    @pl.when(_is_any((g + 1) * c, seq_ends))
    def _():
        r_ref[...] = jnp.zeros(r_ref.shape, F32)

    cos, sin = cos_ref[...], sin_ref[...]
    scale = HEAD_DIM ** -0.5
    for h in range(n_heads):
        sl = slice(h * HEAD_DIM, (h + 1) * HEAD_DIM)
        q = _rope_half(q_ref[:, sl], cos, sin) * scale
        k = _rope_half(k_ref[:, sl], cos, sin)
        v = v_ref[:, sl]
        r = r_ref[h]
        qw = (q * tab_ref[0, h]).astype(BF16)
        o_ref[:, sl] = jnp.dot(qw, r.astype(BF16), preferred_element_type=F32)
        kw_t = (k * tab_ref[1, h]).T.astype(BF16)
        r_ref[h] = tab_ref[2, h] * r + jnp.dot(kw_t, v.astype(BF16), preferred_element_type=F32)


def _ret_fwd_body(lg_ref, q_ref, k_ref, v_ref, gate_ref, xb_ref, cos_ref, sin_ref, o_ref, r_ref, tab_ref, *,
                  n_heads, seq_starts):
    c = RET_CHUNK
    g = pl.program_id(0)

    @pl.when(g == 0)
    def _():
        n = lax.broadcasted_iota(jnp.int32, (c, HEAD_DIM), 0).astype(F32)
        rel = (lax.broadcasted_iota(jnp.int32, (c, c), 0) - lax.broadcasted_iota(jnp.int32, (c, c), 1)).astype(F32)
        for h in range(n_heads):
            lg_f, lg_b = lg_ref[0, h], lg_ref[1, h]
            tab_ref[0, h] = jnp.where(rel >= 0, jnp.exp(lg_f * jnp.maximum(rel, 0.0)),
                                      jnp.exp(lg_b * jnp.maximum(-rel, 0.0)))
            tab_ref[1, h] = jnp.exp(lg_f * (n + 1.0))
            tab_ref[2, h] = jnp.exp(lg_f * (c - 1.0 - n))
            tab_ref[3, h] = jnp.exp(jnp.full((HEAD_DIM, HEAD_DIM), lg_f * c, F32))

    @pl.when(_is_any(g * c, seq_starts))
    def _():
        r_ref[...] = jnp.zeros(r_ref.shape, F32)

    cos, sin = cos_ref[...], sin_ref[...]
    scale = HEAD_DIM ** -0.5
    for h in range(n_heads):
        sl = slice(h * HEAD_DIM, (h + 1) * HEAD_DIM)
        q = _rope_half(q_ref[:, sl], cos, sin) * scale
        k = _rope_half(k_ref[:, sl], cos, sin)
        v = v_ref[:, sl].astype(BF16)
        r = r_ref[h]
        qk = lax.dot_general(q.astype(BF16), k.astype(BF16), (((1,), (1,)), ((), ())), preferred_element_type=F32)
        o = jnp.dot((qk * tab_ref[0, h]).astype(BF16), v, preferred_element_type=F32)
        qw = (q * tab_ref[1, h]).astype(BF16)
        o = o + jnp.dot(qw, r.astype(BF16), preferred_element_type=F32) + xb_ref[:, sl]
        mu = jnp.mean(o, axis=-1, keepdims=True)
        d = o - mu
        var = jnp.mean(d * d, axis=-1, keepdims=True)
        o = d * lax.rsqrt(var + EPS)
        gate = gate_ref[:, sl]
        o_ref[:, sl] = (gate * jax.nn.sigmoid(gate) * o).astype(o_ref.dtype)
        kw_t = (k * tab_ref[2, h]).T.astype(BF16)
        r_ref[h] = tab_ref[3, h] * r + jnp.dot(kw_t, v, preferred_element_type=F32)


def retention(proj, col0, log_gamma, cos, sin, n_heads, seq_starts, seq_ends):
    n, _ = proj.shape
    w = n_heads * HEAD_DIM
    cb0 = col0 // w
    c = RET_CHUNK
    assert c == HEAD_DIM
    n_chunks = n // c
    smem = pl.BlockSpec(memory_space=pltpu.SMEM)
    rev = lambda off: pl.BlockSpec((c, w), lambda s: (n_chunks - 1 - s, cb0 + off))
    rev_tab = pl.BlockSpec((c, HEAD_DIM), lambda s: (n_chunks - 1 - s, 0))
    cross_b = pl.pallas_call(
        functools.partial(_ret_bwd_body, n_heads=n_heads, n_chunks=n_chunks, seq_ends=seq_ends),
        out_shape=jax.ShapeDtypeStruct((n, w), F32),
        grid=(n_chunks,),
        in_specs=[smem, rev(0), rev(1), rev(2), rev_tab, rev_tab],
        out_specs=pl.BlockSpec((c, w), lambda s: (n_chunks - 1 - s, 0)),
        scratch_shapes=[pltpu.VMEM((n_heads, HEAD_DIM, HEAD_DIM), F32),
                        pltpu.VMEM((3, n_heads, c, HEAD_DIM), F32)],
        compiler_params=_cparams(("arbitrary",)),
        name="ret_bwd",
    )(log_gamma, proj, proj, proj, cos, sin)
    fwd = lambda off: pl.BlockSpec((c, w), lambda s: (s, cb0 + off))
    tab = pl.BlockSpec((c, HEAD_DIM), lambda s: (s, 0))
    return pl.pallas_call(
        functools.partial(_ret_fwd_body, n_heads=n_heads, seq_starts=seq_starts),
        out_shape=jax.ShapeDtypeStruct((n, w), BF16),
        grid=(n_chunks,),
        in_specs=[smem, fwd(0), fwd(1), fwd(2), fwd(3), pl.BlockSpec((c, w), lambda s: (s, 0)), tab, tab],
        out_specs=pl.BlockSpec((c, w), lambda s: (s, 0)),
        scratch_shapes=[pltpu.VMEM((n_heads, HEAD_DIM, HEAD_DIM), F32),
                        pltpu.VMEM((4, n_heads, c, HEAD_DIM), F32)],
        compiler_params=_cparams(("arbitrary",)),
        name="ret_fwd",
    )(log_gamma, proj, proj, proj, proj, cross_b, cos, sin)


def _rope_quarter(x, cos, sin_signed, low):
    partner = jnp.where(low, pltpu.roll(x, HEAD_DIM - 32, axis=1), pltpu.roll(x, 32, axis=1))
    return x * cos + partner * sin_signed


def _gqa_prep_body(q_ref, k_ref, v_ref, nw_ref, cos_ref, sin_ref, qo_ref, ko_ref, vto_ref, *, hq, hk):
    cos, sin = cos_ref[...], sin_ref[...]
    low = (lax.broadcasted_iota(jnp.int32, cos.shape, 1) % 64) < 32
    scale = HEAD_DIM ** -0.5 * LOG2E

    def norm_rope(x, w):
        y = x * lax.rsqrt(jnp.mean(x * x, axis=-1, keepdims=True) + EPS) * w
        return _rope_quarter(y, cos, sin, low)

    for h in range(hq):
        sl = slice(h * HEAD_DIM, (h + 1) * HEAD_DIM)
        qo_ref[:, sl] = (norm_rope(q_ref[:, sl], nw_ref[0:1, :]) * scale).astype(qo_ref.dtype)
    for h in range(hk):
        sl = slice(h * HEAD_DIM, (h + 1) * HEAD_DIM)
        ko_ref[:, sl] = norm_rope(k_ref[:, sl], nw_ref[1:2, :]).astype(ko_ref.dtype)
        vto_ref[sl, :] = v_ref[:, sl].T.astype(vto_ref.dtype)


def gqa_prep(proj, qk_norm, cos, sin, hq, hk, tm=256):
    n = proj.shape[0]
    tm = min(tm, n)
    wq, wk = hq * HEAD_DIM, hk * HEAD_DIM
    assert wq % wk == 0
    row = lambda width, blk: pl.BlockSpec((tm, width), lambda i: (i, blk))
    return pl.pallas_call(
        functools.partial(_gqa_prep_body, hq=hq, hk=hk),
        out_shape=[jax.ShapeDtypeStruct((n, wq), BF16), jax.ShapeDtypeStruct((n, wk), BF16),
                   jax.ShapeDtypeStruct((wk, n), BF16)],
        grid=(n // tm,),
        in_specs=[row(wq, 0), row(wk, wq // wk), row(wk, wq // wk + 1),
                  pl.BlockSpec((2, HEAD_DIM), lambda i: (0, 0)), row(HEAD_DIM, 0), row(HEAD_DIM, 0)],
        out_specs=[row(wq, 0), row(wk, 0), pl.BlockSpec((wk, tm), lambda i: (0, i))],
        compiler_params=_cparams(("parallel",)),
        name="gqa_prep",
    )(proj, proj, proj, qk_norm, cos, sin)


def _latent_norm_body(cq_ref, ckv_ref, wq_ref, wkv_ref, qo_ref, kvo_ref):
    def rms(x, w):
        return x * lax.rsqrt(jnp.mean(x * x, axis=-1, keepdims=True) + EPS) * w
    qo_ref[...] = rms(cq_ref[...], wq_ref[...]).astype(qo_ref.dtype)
    kvo_ref[...] = rms(ckv_ref[...], wkv_ref[...]).astype(kvo_ref.dtype)


def latent_norm(proj, col_cq, cq_norm, ckv_norm, tm=512):
    n = proj.shape[0]
    tm = min(tm, n)
    assert col_cq % Q_LORA == 0 and (col_cq + Q_LORA) % KV_LORA == 0
    return pl.pallas_call(
        _latent_norm_body,
        out_shape=[jax.ShapeDtypeStruct((n, Q_LORA), BF16), jax.ShapeDtypeStruct((n, KV_LORA), BF16)],
        grid=(n // tm,),
        in_specs=[pl.BlockSpec((tm, Q_LORA), lambda i: (i, col_cq // Q_LORA)),
                  pl.BlockSpec((tm, KV_LORA), lambda i: (i, (col_cq + Q_LORA) // KV_LORA)),
                  pl.BlockSpec((1, Q_LORA), lambda i: (0, 0)), pl.BlockSpec((1, KV_LORA), lambda i: (0, 0))],
        out_specs=[pl.BlockSpec((tm, Q_LORA), lambda i: (i, 0)), pl.BlockSpec((tm, KV_LORA), lambda i: (i, 0))],
        compiler_params=_cparams(("parallel",)),
        name="latent_norm",
    )(proj, proj, cq_norm.reshape(1, Q_LORA), ckv_norm.reshape(1, KV_LORA))


def _mla_q_body(a_ref, b_ref, cos_ref, sin_ref, qo_ref, *, heads):
    cos, sin = cos_ref[...], sin_ref[...]
    low = (lax.broadcasted_iota(jnp.int32, cos.shape, 1) % 64) < 32
    scale = (QK_NOPE + QK_ROPE) ** -0.5 * LOG2E
    r = jnp.dot(a_ref[...], b_ref[...], preferred_element_type=F32)
    for h in range(heads):
        a = slice(h * MLA_QK_PAD, h * MLA_QK_PAD + QK_NOPE)
        b = slice(h * MLA_QK_PAD + QK_NOPE, (h + 1) * MLA_QK_PAD)
        qo_ref[:, a] = (r[:, a] * scale).astype(qo_ref.dtype)
        qo_ref[:, b] = (_rope_quarter(r[:, b], cos, sin, low) * scale).astype(qo_ref.dtype)


def _mla_kv_body(a_ref, b_ref, kpe_ref, cos_ref, sin_ref, ko_ref, vto_ref, *, heads):
    cos, sin = cos_ref[...], sin_ref[...]
    low = (lax.broadcasted_iota(jnp.int32, cos.shape, 1) % 64) < 32
    k_rope = _rope_quarter(kpe_ref[...], cos, sin, low).astype(ko_ref.dtype)
    r = jnp.dot(a_ref[...], b_ref[...], preferred_element_type=F32)
    for h in range(heads):
        a = slice(h * MLA_QK_PAD, h * MLA_QK_PAD + QK_NOPE)
        b = slice(h * MLA_QK_PAD + QK_NOPE, (h + 1) * MLA_QK_PAD)
        ko_ref[:, a] = r[:, a].astype(ko_ref.dtype)
        ko_ref[:, b] = k_rope
        vto_ref[h * V_D:(h + 1) * V_D, :] = r[:, b].T.astype(vto_ref.dtype)


def mla_up_proj(cqn, ckvn, w_uq, w_ukv, layer, proj, col_kpe, cos, sin, n_heads, tm=512, heads_per_step=4):
    n = cqn.shape[0]
    tm = min(tm, n)
    hs = min(heads_per_step, n_heads)
    assert n_heads % hs == 0 and col_kpe % 128 == 0
    tn = hs * MLA_QK_PAD
    wd = n_heads * MLA_QK_PAD
    grid = (n // tm, n_heads // hs)
    tab = pl.BlockSpec((tm, 128), lambda i, j: (i, 0))
    q = pl.pallas_call(
        functools.partial(_mla_q_body, heads=hs),
        out_shape=jax.ShapeDtypeStruct((n, wd), BF16),
        grid=grid,
        in_specs=[pl.BlockSpec((tm, Q_LORA), lambda i, j: (i, 0)),
                  _layer_spec(w_uq, layer, (Q_LORA, tn), lambda i, j: (0, j)),
                  tab, tab],
        out_specs=pl.BlockSpec((tm, tn), lambda i, j: (i, j)),
        compiler_params=_cparams(("parallel", "parallel")),
        name="mla_q_proj",
    )(cqn, w_uq, cos, sin)
    k, vt = pl.pallas_call(
        functools.partial(_mla_kv_body, heads=hs),
        out_shape=[jax.ShapeDtypeStruct((n, wd), BF16), jax.ShapeDtypeStruct((n_heads * V_D, n), BF16)],
        grid=grid,
        in_specs=[pl.BlockSpec((tm, KV_LORA), lambda i, j: (i, 0)),
                  _layer_spec(w_ukv, layer, (KV_LORA, tn), lambda i, j: (0, j)),
                  pl.BlockSpec((tm, 128), lambda i, j: (i, col_kpe // 128)), tab, tab],
        out_specs=[pl.BlockSpec((tm, tn), lambda i, j: (i, j)),
                   pl.BlockSpec((hs * V_D, tm), lambda i, j: (j, i))],
        compiler_params=_cparams(("parallel", "parallel")),
        name="mla_kv_proj",
    )(ckvn, w_ukv, proj, cos, sin)
    return q, k, vt


def _inv_freq(dim):
    return (np.float32(ROPE_THETA) ** (-np.arange(0, dim, 2, dtype=np.float32) / np.float32(dim))).astype(np.float32)


def _angles(pos, dim):
    ang = pos.astype(F32)[:, None] * jnp.asarray(_inv_freq(dim))[None, :]
    return jnp.cos(ang), jnp.sin(ang)


def _rope_tables(seq_lens):
    tmax = max(seq_lens)
    pos = jnp.arange(tmax)
    per_token = lambda t: jnp.concatenate([t[:n] for n in seq_lens], axis=0)
    c, s = _angles(pos, HEAD_DIM)
    ret = (per_token(jnp.concatenate([c, c], -1)), per_token(jnp.concatenate([-s, s], -1)))
    rows = -(-tmax // GRID_W)
    cr, sr = (jnp.repeat(t, GRID_W, axis=0)[:tmax] for t in _angles(jnp.arange(rows), HEAD_DIM // 2))
    cc, sc = (jnp.tile(t, (rows, 1))[:tmax] for t in _angles(jnp.arange(GRID_W), HEAD_DIM // 2))
    axial = (per_token(jnp.concatenate([cr, cr, cc, cc], -1)), per_token(jnp.concatenate([-sr, sr, -sc, sc], -1)))
    c1, s1 = _angles(pos, QK_ROPE)
    one, zero = jnp.ones_like(c1), jnp.zeros_like(s1)
    mla = (per_token(jnp.concatenate([c1, c1, one, one], -1)), per_token(jnp.concatenate([-s1, s1, zero, zero], -1)))
    return ret, axial, mla


def _pad_last(w, width):
    return jnp.pad(w, ((0, 0),) * (w.ndim - 1) + ((0, width - w.shape[-1]),))


def _mix_even(h, w_in, i, log_1m_gamma, rel_bias, ret_tab, seqs):
    seq_starts, seq_ends = seqs
    n_heads = w_in.shape[-1] // (7 * HEAD_DIM)
    w = n_heads * HEAD_DIM
    proj = matmul(h, w_in, layer=i, out_dtype=F32, name="mm_in_even")
    assert all(window // (2 * dil) == WIN_HALF for window, dil in DIL_PATTERNS)
    dils = tuple(dil for _, dil in DIL_PATTERNS)
    oa = window_attention(proj, rel_bias, dils, n_heads, seq_starts, seq_ends)
    log_gamma = jnp.log1p(-jnp.exp(log_1m_gamma.astype(F32)))
    ob = retention(proj, 3 * w, log_gamma, ret_tab[0], ret_tab[1], n_heads, seq_starts, seq_ends)
    return oa, ob


def _mix_odd(h, w_in, in_odd, i, qk_norm, cq_norm, ckv_norm, w_uq, w_ukv, ax_tab, mla_tab, groups, tiles):
    wkv = KV_C * HEAD_DIM
    wq = in_odd - 2 * wkv - Q_LORA - KV_LORA - QK_ROPE
    hq = wq // HEAD_DIM
    proj = matmul(h, w_in, layer=i, out_dtype=F32, name="mm_in_odd")
    qc, kc, vtc = gqa_prep(proj, qk_norm, ax_tab[0], ax_tab[1], hq, KV_C)
    col_cq = wq + 2 * wkv
    cqn, ckvn = latent_norm(proj, col_cq, cq_norm, ckv_norm)
    n_hd = w_uq.shape[-1] // MLA_QK_PAD
    assert V_D == MLA_QK_PAD - QK_NOPE and w_ukv.shape[-1] == n_hd * MLA_QK_PAD
    q_d, k_d, vt_d = mla_up_proj(cqn, ckvn, w_uq, w_ukv, i, proj, col_cq + Q_LORA + KV_LORA,
                                 mla_tab[0], mla_tab[1], n_hd)
    ocs, ods = [], []
    for row0, batch, seq_len in groups:
        ocs.append(flash_attention(qc, kc, vtc, n_kv_heads=KV_C, g_heads=hq // KV_C, dqk=HEAD_DIM, dv=HEAD_DIM,
                                   row0=row0, batch=batch, seq_len=seq_len, tq=tiles[0], tk=tiles[1],
                                   nsplit=tiles[2], name="attn_gqa"))
        ods.append(flash_attention(q_d, k_d, vt_d, n_kv_heads=n_hd, g_heads=1, dqk=MLA_QK_PAD, dv=V_D,
                                   row0=row0, batch=batch, seq_len=seq_len, tq=tiles[3], tk=tiles[4],
                                   nsplit=tiles[5], name="attn_mla"))
    return jnp.concatenate(ocs, 0), jnp.concatenate(ods, 0)


def _conv_ffn(h, w_up, conv_w, w_down, layer, x, gate, seqs):
    seq_starts, seq_ends = seqs
    dffp = w_down.shape[-2]
    act = ff_up_conv_gate(h, w_up, conv_w, layer, dffp, seq_starts, seq_ends)
    return matmul(act, w_down, layer=layer, out_dtype=F32, tn=1024, tk=2816, resid=x, gate=gate,
                  seq_starts=seq_starts, name="mm_ff_down")


def _trunk(x_prompt, x_sample, c_prompt, c_sample, rel_bias, norm_w, w_mod, b_mod, w_in_even,
           ret_log_1m_gamma, w_out_even, w_in_odd, c_qk_norm, mla_cq_norm, mla_ckv_norm, w_uq,
           w_ukv, w_out_odd, w_ff_up, conv_ff, w_ff_down, final_norm_w, *, attn_tiles, ff_mult):
    bp, tp, d = x_prompt.shape
    bs, ts, _ = x_sample.shape
    depth = norm_w.shape[0]
    seq_lens = (tp,) * bp + (ts,) * bs
    seq_starts = tuple(int(v) for v in np.cumsum((0,) + seq_lens[:-1]))
    seq_ends = tuple(int(v) for v in np.cumsum(seq_lens))
    seqs = (seq_starts, seq_ends)
    groups = ((0, bp, tp), (bp * tp, bs, ts))
    n_seq = len(seq_lens)

    x = jnp.concatenate([x_prompt.reshape(bp * tp, d), x_sample.reshape(bs * ts, d)], axis=0)
    c = jnp.concatenate([c_prompt, c_sample], axis=0)
    c_act = jnp.pad(jax.nn.silu(c), ((0, 8 - n_seq), (0, 0))).astype(BF16)
    ret_tab, ax_tab, mla_tab = _rope_tables(seq_lens)

    in_odd = w_in_odd.shape[-1]
    w_in_even_b = w_in_even.astype(BF16)
    w_in_odd_b = _pad_last(w_in_odd, -(-in_odd // 512) * 512).astype(BF16)
    w_out_b = (w_out_even.astype(BF16), w_out_odd.astype(BF16))
    n_hd = w_uq.shape[-1] // (QK_NOPE + QK_ROPE)
    w_uq_b = _pad_last(w_uq.reshape(-1, Q_LORA, n_hd, QK_NOPE + QK_ROPE), MLA_QK_PAD)
    w_uq_b = w_uq_b.reshape(-1, Q_LORA, n_hd * MLA_QK_PAD).astype(BF16)
    w_ukv_b = w_ukv.astype(BF16)
    dff = w_ff_down.shape[-2]
    dffp = -(-dff // ff_mult) * ff_mult
    split_pad = lambda w: jnp.concatenate([_pad_last(w[..., :dff], dffp), _pad_last(w[..., dff:], dffp)], axis=-1)
    w_up_b = split_pad(w_ff_up).astype(BF16)
    conv_p = split_pad(conv_ff)
    w_down_b = jnp.pad(w_ff_down, ((0, 0), (0, dffp - dff), (0, 0))).astype(BF16)
    b_mod3 = b_mod.reshape(depth, 1, -1)

    for layer in range(depth):
        mod = matmul(c_act, w_mod, layer=layer, out_dtype=F32, tm=8, tn=2048, tk=1024, bias=b_mod3, name="mm_mod")
        mod6 = mod[:n_seq].reshape(n_seq * 6, 1, d)
        gate_of = lambda ci: mod6.reshape(n_seq, 6, d)[:, ci].reshape(n_seq, 1, d)
        h = norm_mod(x, norm_w[layer, 0], mod6, 1, 0, seq_starts)
        i = layer // 2
        if layer % 2 == 0:
            mix = _mix_even(h, w_in_even_b, i, ret_log_1m_gamma[i], rel_bias, ret_tab, seqs)
        else:
            mix = _mix_odd(h, w_in_odd_b, in_odd, i, c_qk_norm[i], mla_cq_norm[i], mla_ckv_norm[i], w_uq_b, w_ukv_b,
                           ax_tab, mla_tab, groups, attn_tiles)
        x = matmul_cat_resid(mix[0], mix[1], w_out_b[layer % 2], i, x, gate_of(2), seq_starts)
        h = norm_mod(x, norm_w[layer, 1], mod6, 4, 3, seq_starts)
        x = _conv_ffn(h, w_up_b, conv_p, w_down_b, layer, x, gate_of(5), seqs)
    y_p, y_s = final_norm(x, final_norm_w, bp * tp)
    return y_p.reshape(bp, tp, d), y_s.reshape(bs, ts, d)


def kernel(x_prompt, x_sample, c_prompt, c_sample, rel_bias, norm_w, w_mod, b_mod, w_in_even, ret_log_1m_gamma, w_out_even, w_in_odd, c_qk_norm, mla_cq_norm, mla_ckv_norm, w_uq, w_ukv, w_out_odd, w_ff_up, conv_ff, w_ff_down, final_norm_w):
    return _trunk(x_prompt, x_sample, c_prompt, c_sample, rel_bias, norm_w, w_mod, b_mod, w_in_even,
                  ret_log_1m_gamma, w_out_even, w_in_odd, c_qk_norm, mla_cq_norm, mla_ckv_norm, w_uq,
                  w_ukv, w_out_odd, w_ff_up, conv_ff, w_ff_down, final_norm_w,
                  attn_tiles=(512, 4096, 8, 2048, 4096, 8), ff_mult=1024)
```

```python
import functools
import math

import numpy as np
import jax
import jax.numpy as jnp
from jax import lax
from jax.experimental import pallas as pl
from jax.experimental.pallas import tpu as pltpu

F32 = jnp.float32
BF16 = jnp.bfloat16

HEAD_DIM = 128
GRID_W = 64
DIL_PATTERNS = ((128, 1), (512, 4), (2048, 16))
RET_CHUNK = 128
KV_C = 4
Q_LORA = 768
KV_LORA = 256
QK_NOPE = 128
QK_ROPE = 64
V_D = 128
MLA_QK_PAD = 256
N_BUCKETS = 32
MAX_DIST = 1024
ROPE_THETA = 10000.0
EPS = 1e-6
NEG = -1e30
LOG2E = math.log2(math.e)
FLASH_HEADROOM = 60.0
FLASH_SEED_KEYS = 256
FLASH_TWO_PASS_KEYS = 2048

WIN_HALF = 64
WIN_TQ = 128
BF16_ROWS = 16
VMEM_LIMIT = 56 * 1024 * 1024


def _cparams(sem):
    return pltpu.CompilerParams(dimension_semantics=sem, vmem_limit_bytes=VMEM_LIMIT)


def _seq_id(row, seq_starts):
    s = 0
    for st in seq_starts[1:]:
        s = s + (row >= st).astype(jnp.int32)
    return s


def _is_any(row, marks):
    r = row == marks[0]
    for m in marks[1:]:
        r = jnp.logical_or(r, row == m)
    return r


def _largest_tile(n, cap):
    best = 128
    for t in range(128, cap + 1, 128):
        if n % t == 0:
            best = t
    return best


def _mm_body(*refs, nk, mode):
    a_ref, b_ref = refs[0], refs[1]
    if mode == "plain":
        extra, (o_ref, acc_ref) = (), refs[2:]
    elif mode == "bias":
        extra, (o_ref, acc_ref) = refs[2:3], refs[3:]
    else:
        extra, (o_ref, acc_ref) = refs[2:4], refs[4:]

    def epilogue(r):
        if mode == "bias":
            r = r + extra[0][...]
        elif mode == "resid":
            r = extra[0][...] + extra[1][0] * r
        o_ref[...] = r.astype(o_ref.dtype)

    def prod():
        return jnp.dot(a_ref[...].astype(BF16), b_ref[...].astype(BF16), preferred_element_type=F32)

    if nk == 1:
        epilogue(prod())
        return
    k = pl.program_id(2)

    @pl.when(k == 0)
    def _():
        acc_ref[...] = prod()

    @pl.when(jnp.logical_and(k > 0, k < nk - 1))
    def _():
        acc_ref[...] += prod()

    @pl.when(k == nk - 1)
    def _():
        epilogue(acc_ref[...] + prod())


def _layer_spec(w, layer, block, index_map):
    if w.ndim == len(block):
        return pl.BlockSpec(block, index_map)
    assert w.ndim == len(block) + 1
    return pl.BlockSpec((None,) + tuple(block), lambda *g: (layer,) + tuple(index_map(*g)))


def matmul(a, b, *, out_dtype, layer=None, tm=1024, tn=512, tk=4096, bias=None, resid=None, gate=None,
           seq_starts=None, name="mm"):
    m, kd = a.shape
    n = b.shape[-1]
    tm, tn, tk = min(tm, m), _largest_tile(n, tn), _largest_tile(kd, tk)
    assert m % tm == 0 and n % tn == 0 and kd % tk == 0 and b.shape[-2] == kd, (a.shape, b.shape, tm, tn, tk)
    nk = kd // tk
    in_specs = [pl.BlockSpec((tm, tk), lambda i, j, k: (i, k)),
                _layer_spec(b, layer, (tk, tn), lambda i, j, k: (k, j))]
    args = [a, b]
    if bias is not None:
        mode = "bias"
        in_specs.append(_layer_spec(bias, layer, (1, tn), lambda i, j, k: (0, j)))
        args.append(bias)
    elif resid is not None:
        mode = "resid"
        in_specs.append(pl.BlockSpec((tm, tn), lambda i, j, k: (i, j)))
        in_specs.append(pl.BlockSpec((1, 1, tn), lambda i, j, k: (_seq_id(i * tm, seq_starts), 0, j)))
        args += [resid, gate]
    else:
        mode = "plain"
    return pl.pallas_call(
        functools.partial(_mm_body, nk=nk, mode=mode),
        out_shape=jax.ShapeDtypeStruct((m, n), out_dtype),
        grid=(m // tm, n // tn, nk),
        in_specs=in_specs,
        out_specs=pl.BlockSpec((tm, tn), lambda i, j, k: (i, j)),
        scratch_shapes=[pltpu.VMEM((tm, tn) if nk > 1 else (8, 128), F32)],
        compiler_params=_cparams(("parallel", "parallel", "arbitrary")),
        name=name,
    )(*args)


def _mm_cat_body(*refs, k1, tile0):
    ng = len(tile0) - 1
    a1_refs, a2_refs = refs[:ng], refs[ng:2 * ng]
    b_ref, x_ref, g_ref, o_ref = refs[2 * ng:]
    i = pl.program_id(0)
    for g in range(ng):
        @pl.when(jnp.logical_and(i >= tile0[g], i < tile0[g + 1]))
        def _():
            r = jnp.dot(a1_refs[g][...], b_ref[0:k1], preferred_element_type=F32)
            r = r + jnp.dot(a2_refs[g][...], b_ref[k1:], preferred_element_type=F32)
            o_ref[...] = (x_ref[...] + g_ref[0] * r).astype(o_ref.dtype)


def matmul_cat_resid(a1, a2, b, layer, resid, gate, seq_starts, tm=1024, tn=512, name="mm_out"):
    k1, k2 = a1[0].shape[1], a2[0].shape[1]
    rows = [a.shape[0] for a in a1]
    assert rows == [a.shape[0] for a in a2]
    m, n = sum(rows), b.shape[-1]
    tm, tn = min([tm] + rows), _largest_tile(n, tn)
    assert b.shape[-2] == k1 + k2 and all(r % tm == 0 for r in rows) and k1 % BF16_ROWS == 0
    tile0 = tuple(int(v) for v in np.cumsum([0] + [r // tm for r in rows]))
    group_spec = lambda g, k: pl.BlockSpec(
        (tm, k), lambda i, j: (jnp.clip(i - tile0[g], 0, tile0[g + 1] - tile0[g] - 1), 0))
    return pl.pallas_call(
        functools.partial(_mm_cat_body, k1=k1, tile0=tile0),
        out_shape=jax.ShapeDtypeStruct((m, n), resid.dtype),
        grid=(m // tm, n // tn),
        in_specs=[group_spec(g, k1) for g in range(len(rows))] + [group_spec(g, k2) for g in range(len(rows))]
                 + [_layer_spec(b, layer, (k1 + k2, tn), lambda i, j: (0, j)),
                    pl.BlockSpec((tm, tn), lambda i, j: (i, j)),
                    pl.BlockSpec((1, 1, tn), lambda i, j: (_seq_id(i * tm, seq_starts), 0, j))],
        out_specs=pl.BlockSpec((tm, tn), lambda i, j: (i, j)),
        compiler_params=_cparams(("parallel", "parallel")),
        name=name,
    )(*a1, *a2, b, resid, gate)


def _ffup_body(a_ref, ap_ref, an_ref, bg_ref, bv_ref, cwg_ref, cwv_ref, o_ref, ext_ref, *,
               tm, seq_starts, seq_ends):
    i = pl.program_id(0)
    halo = BF16_ROWS

    @pl.when(pl.program_id(1) == 0)
    def _():
        ext_ref[0:tm] = a_ref[...]
        hrow = lax.broadcasted_iota(jnp.int32, ap_ref.shape, 0)
        ext_ref[tm:] = jnp.where(hrow < halo // 2, an_ref[...], ap_ref[...])

    first = _is_any(i * tm, seq_starts)
    last = _is_any((i + 1) * tm, seq_ends)
    ext = ext_ref[...]
    rows = tm + halo
    row = lax.broadcasted_iota(jnp.int32, (tm, o_ref.shape[1]), 0)
    kill_prev = jnp.logical_and(row == 0, first)
    kill_next = jnp.logical_and(row == tm - 1, last)

    def conv(b_ref, cw_ref):
        u = jnp.dot(ext, b_ref[...], preferred_element_type=F32)
        u_prev = jnp.where(kill_prev, 0.0, pltpu.roll(u, 1, axis=0)[0:tm])
        u_next = jnp.where(kill_next, 0.0, pltpu.roll(u, rows - 1, axis=0)[0:tm])
        cw = cw_ref[...]
        return u_prev * cw[0:1] + u[0:tm] * cw[1:2] + u_next * cw[2:3]

    g = conv(bg_ref, cwg_ref)
    val = conv(bv_ref, cwv_ref)
    o_ref[...] = (g * jax.nn.sigmoid(g) * val).astype(o_ref.dtype)


def ff_up_conv_gate(h, w_up, conv_w, layer, dff, seq_starts, seq_ends, tm=1024, tn=512):
    n, d = h.shape
    tm, tn = min(tm, n), _largest_tile(dff, tn)
    nj = dff // tn
    hb = tm // BF16_ROWS
    nhb = n // BF16_ROWS
    prev = lambda i: jnp.maximum(i * hb - 1, 0)
    nxt = lambda i: jnp.minimum((i + 1) * hb, nhb - 1)
    return pl.pallas_call(
        functools.partial(_ffup_body, tm=tm, seq_starts=seq_starts, seq_ends=seq_ends),
        out_shape=jax.ShapeDtypeStruct((n, dff), BF16),
        grid=(n // tm, nj),
        in_specs=[pl.BlockSpec((tm, d), lambda i, j: (i, 0)),
                  pl.BlockSpec((BF16_ROWS, d), lambda i, j: (prev(i), 0)),
                  pl.BlockSpec((BF16_ROWS, d), lambda i, j: (nxt(i), 0)),
                  _layer_spec(w_up, layer, (d, tn), lambda i, j: (0, j)),
                  _layer_spec(w_up, layer, (d, tn), lambda i, j: (0, j + nj)),
                  _layer_spec(conv_w, layer, (3, tn), lambda i, j: (0, j)),
                  _layer_spec(conv_w, layer, (3, tn), lambda i, j: (0, j + nj))],
        out_specs=pl.BlockSpec((tm, tn), lambda i, j: (i, j)),
        scratch_shapes=[pltpu.VMEM((tm + BF16_ROWS, d), BF16)],
        compiler_params=_cparams(("parallel", "arbitrary")),
        name="ff_up_conv_gate",
    )(h, h, h, w_up, w_up, conv_w, conv_w)


def _normmod_body(x_ref, w_ref, sc_ref, sh_ref, o_ref):
    x = x_ref[...]
    y = x * lax.rsqrt(jnp.mean(x * x, axis=-1, keepdims=True) + EPS) * w_ref[...]
    o_ref[...] = (y * (1.0 + sc_ref[0]) + sh_ref[0]).astype(o_ref.dtype)


def norm_mod(x, w, mod6, scale_idx, shift_idx, seq_starts, tm=512):
    n, d = x.shape
    tm = min(tm, n)
    return pl.pallas_call(
        _normmod_body,
        out_shape=jax.ShapeDtypeStruct((n, d), BF16),
        grid=(n // tm,),
        in_specs=[pl.BlockSpec((tm, d), lambda i: (i, 0)),
                  pl.BlockSpec((1, d), lambda i: (0, 0)),
                  pl.BlockSpec((1, 1, d), lambda i: (_seq_id(i * tm, seq_starts) * 6 + scale_idx, 0, 0)),
                  pl.BlockSpec((1, 1, d), lambda i: (_seq_id(i * tm, seq_starts) * 6 + shift_idx, 0, 0))],
        out_specs=pl.BlockSpec((tm, d), lambda i: (i, 0)),
        compiler_params=_cparams(("parallel",)),
        name="norm_mod",
    )(x, w.reshape(1, d), mod6, mod6)


def _final_norm_body(x_ref, w_ref, oa_ref, ob_ref, *, na_tiles):
    x = x_ref[...]
    y = x * lax.rsqrt(jnp.mean(x * x, axis=-1, keepdims=True) + EPS) * w_ref[...]
    i = pl.program_id(0)

    @pl.when(i < na_tiles)
    def _():
        oa_ref[...] = y

    @pl.when(i >= na_tiles)
    def _():
        ob_ref[...] = y


def final_norm(x, w, n_first, tm=256):
    n, d = x.shape
    tm = min(tm, n_first, n - n_first)
    assert n_first % tm == 0 and n % tm == 0
    na = n_first // tm
    return pl.pallas_call(
        functools.partial(_final_norm_body, na_tiles=na),
        out_shape=[jax.ShapeDtypeStruct((n_first, d), F32), jax.ShapeDtypeStruct((n - n_first, d), F32)],
        grid=(n // tm,),
        in_specs=[pl.BlockSpec((tm, d), lambda i: (i, 0)), pl.BlockSpec((1, d), lambda i: (0, 0))],
        out_specs=[pl.BlockSpec((tm, d), lambda i: (jnp.minimum(i, na - 1), 0)),
                   pl.BlockSpec((tm, d), lambda i: (jnp.maximum(i - na, 0), 0))],
        compiler_params=_cparams(("arbitrary",)),
        name="final_norm",
    )(x, w.reshape(1, d))


def _flash_body(q_ref, k_ref, vt_ref, o_ref, m_ref, l_ref, acc_ref, pv_ref, mc_ref, *,
                g_heads, dqk, dv, nkv, tq, nsplit):
    j = pl.program_id(3)
    cs = (g_heads * tq) // nsplit
    cols = [slice(c * cs, (c + 1) * cs) for c in range(nsplit)]
    if g_heads == 1:
        qcat = q_ref[...]
    else:
        qcat = jnp.concatenate([q_ref[:, g * dqk:(g + 1) * dqk] for g in range(g_heads)], axis=0)

    @pl.when(j == 0)
    def _():
        k_head = k_ref[0:FLASH_SEED_KEYS, :]
        for sl in cols:
            s0 = lax.dot_general(k_head, qcat[sl], (((1,), (1,)), ((), ())), preferred_element_type=F32)
            m_ref[:, sl] = jnp.max(s0, axis=0, keepdims=True)
        l_ref[...] = jnp.zeros(l_ref.shape, F32)
        acc_ref[...] = jnp.zeros(acc_ref.shape, F32)

    k = k_ref[...]
    vt1 = jnp.concatenate([vt_ref[...], jnp.ones((BF16_ROWS, vt_ref.shape[1]), BF16)], axis=0)
    qk = lambda sl: lax.dot_general(k, qcat[sl], (((1,), (1,)), ((), ())), preferred_element_type=F32)

    @pl.when(j < nkv)
    def _():
        ps = []
        for sl in cols:
            s = qk(sl)
            mc_ref[:, sl] = jnp.max(s, axis=0, keepdims=True)
            ps.append(jnp.exp2(s - m_ref[:, sl]).astype(BF16))
        for p, sl in zip(ps, cols):
            pv_ref[:, sl] = jnp.dot(vt1, p, preferred_element_type=F32)

    single_pass_ok = jnp.max(mc_ref[...] - m_ref[...]) <= FLASH_HEADROOM

    @pl.when(single_pass_ok)
    def _():
        m_old = m_ref[...]
        m_new = jnp.maximum(m_old, mc_ref[...])
        alpha = jnp.exp2(m_old - m_new)
        l_ref[...] = alpha * (l_ref[...] + pv_ref[dv:dv + 1, :])
        acc_ref[...] = alpha * (acc_ref[...] + pv_ref[0:dv, :])
        m_ref[...] = m_new

    @pl.when(jnp.logical_not(single_pass_ok))
    def _():
        tk = k_ref.shape[0]
        step = min(tk, FLASH_TWO_PASS_KEYS)
        for k0 in range(0, tk, step):
            k_sub, vt_sub = k[k0:k0 + step], vt1[:, k0:k0 + step]
            scores = [lax.dot_general(k_sub, qcat[sl], (((1,), (1,)), ((), ())), preferred_element_type=F32)
                      for sl in cols]
            for s, sl in zip(scores, cols):
                m_prev = m_ref[:, sl]
                m_new = jnp.maximum(m_prev, jnp.max(s, axis=0, keepdims=True))
                alpha = jnp.exp2(m_prev - m_new)
                p = jnp.exp2(s - m_new)
                pv = jnp.dot(vt_sub, p.astype(BF16), preferred_element_type=F32)
                l_ref[:, sl] = alpha * l_ref[:, sl] + pv[dv:dv + 1]
                acc_ref[:, sl] = alpha * acc_ref[:, sl] + pv[:dv]
                m_ref[:, sl] = m_new

    @pl.when(j == nkv - 1)
    def _():
        o_t = acc_ref[...] / l_ref[...]
        for g in range(g_heads):
            o_ref[:, g * dv:(g + 1) * dv] = o_t[:, g * tq:(g + 1) * tq].T.astype(o_ref.dtype)


def flash_attention(q, k, vt, *, n_kv_heads, g_heads, dqk, dv, row0, batch, seq_len, tq, tk, nsplit, name):
    tq, tk = min(tq, seq_len), min(tk, seq_len)
    nq, nkv = seq_len // tq, seq_len // tk
    qb0, kb0 = row0 // tq, row0 // tk
    ncol = g_heads * tq
    nsplit = min(nsplit, ncol // 128)
    return pl.pallas_call(
        functools.partial(_flash_body, g_heads=g_heads, dqk=dqk, dv=dv, nkv=nkv, tq=tq, nsplit=nsplit),
        out_shape=jax.ShapeDtypeStruct((batch * seq_len, n_kv_heads * g_heads * dv), BF16),
        grid=(batch, n_kv_heads, nq, nkv),
        in_specs=[pl.BlockSpec((tq, g_heads * dqk), lambda b, h, i, j: (qb0 + b * nq + i, h)),
                  pl.BlockSpec((tk, dqk), lambda b, h, i, j: (kb0 + b * nkv + j, h)),
                  pl.BlockSpec((dv, tk), lambda b, h, i, j: (h, kb0 + b * nkv + j))],
        out_specs=pl.BlockSpec((tq, g_heads * dv), lambda b, h, i, j: (b * nq + i, h)),
        scratch_shapes=[pltpu.VMEM((1, ncol), F32), pltpu.VMEM((1, ncol), F32), pltpu.VMEM((dv, ncol), F32),
                        pltpu.VMEM((dv + BF16_ROWS, ncol), F32), pltpu.VMEM((1, ncol), F32)],
        compiler_params=_cparams(("parallel", "parallel", "parallel", "arbitrary")),
        name=name,
    )(q, k, vt)


def _win_body(q_ref, kp_ref, kc_ref, kn_ref, vp_ref, vc_ref, vn_ref, bias_ref, o_ref,
              kext_ref, vext_ref, *scr, tb, tq, dils, seq_starts, seq_ends):
    i = pl.program_id(0)
    first = _is_any(i * tb, seq_starts)
    last = _is_any((i + 1) * tb, seq_ends)
    hmax = WIN_HALF * max(dils)
    for ext_ref, p_ref, c_ref, n_ref in ((kext_ref, kp_ref, kc_ref, kn_ref), (vext_ref, vp_ref, vc_ref, vn_ref)):
        ext_ref[0:hmax] = p_ref[...]
        ext_ref[hmax:hmax + tb] = c_ref[...]
        ext_ref[hmax + tb:] = n_ref[...]
    nkeys = tq + 2 * WIN_HALF
    col = lax.broadcasted_iota(jnp.int32, (tq, nkeys), 1)
    bad_first = jnp.logical_and(col < WIN_HALF, first)
    bad_last = jnp.logical_and(col >= WIN_HALF + tq, last)
    scale = HEAD_DIM ** -0.5
    o_scr, l_scr = scr[:len(dils)], scr[len(dils):]
    for p, dil in enumerate(dils):
        nsub = tb // (tq * dil)
        bias = bias_ref[p]
        for sub in range(nsub):
            for r in range(dil):
                start = sub * tq * dil + r
                kstart = hmax - WIN_HALF * dil + start
                q = (q_ref[pl.ds(start, tq, stride=dil), :] * scale).astype(BF16)
                k = kext_ref[pl.ds(kstart, nkeys, stride=dil), :].astype(BF16)
                v = vext_ref[pl.ds(kstart, nkeys, stride=dil), :].astype(BF16)
                s = lax.dot_general(q, k, (((1,), (1,)), ((), ())), preferred_element_type=F32) + bias
                if sub == 0:
                    s = jnp.where(bad_first, NEG, s)
                if sub == nsub - 1:
                    s = jnp.where(bad_last, NEG, s)
                m = jnp.max(s, axis=-1, keepdims=True)
                e = jnp.exp(s - m)
                den = jnp.sum(e, axis=-1, keepdims=True)
                o = jnp.dot(e.astype(BF16), v, preferred_element_type=F32) / den
                o_scr[p][pl.ds(start, tq, stride=dil), :] = o
                l_scr[p][pl.ds(start, tq, stride=dil), :] = jnp.broadcast_to(m + jnp.log(den), (tq, HEAD_DIM))
    lses = [l[...] for l in l_scr]
    m = functools.reduce(jnp.maximum, lses)
    ws = [jnp.exp(l - m) for l in lses]
    num = functools.reduce(lambda a, b: a + b, [w * o[...] for w, o in zip(ws, o_scr)])
    o_ref[...] = (num / functools.reduce(lambda a, b: a + b, ws)).astype(o_ref.dtype)


def _t5_bucket(rel):
    nb = N_BUCKETS // 2
    max_exact = nb // 2
    n = jnp.abs(rel)
    log_ratio = jnp.log(jnp.maximum(n, 1).astype(F32) / max_exact) / math.log(MAX_DIST / max_exact)
    large = jnp.minimum(max_exact + (log_ratio * (nb - max_exact)).astype(jnp.int32), nb - 1)
    return jnp.where(rel > 0, nb, 0) + jnp.where(n < max_exact, n, large)


def _window_bias(rel_bias, dil, tq):
    kj = jnp.arange(tq + 2 * WIN_HALF)
    rel = kj[None, :] - WIN_HALF - jnp.arange(tq)[:, None]
    onehot = (_t5_bucket(rel * dil)[:, :, None] == jnp.arange(N_BUCKETS)).astype(F32)
    bias = jnp.einsum("qkb,bh->qkh", onehot, rel_bias.astype(F32), precision=lax.Precision.HIGHEST)
    bias = jnp.where((jnp.abs(rel) <= WIN_HALF)[:, :, None], bias, NEG)
    return bias.transpose(2, 0, 1)


def window_attention(proj, rel_bias, dils, n_heads, seq_starts, seq_ends):
    n = proj.shape[0]
    tq = WIN_TQ
    hmax = WIN_HALF * max(dils)
    tb = tq * max(dils)
    assert all(s % tb == 0 for s in seq_starts + seq_ends) and tb % hmax == 0
    bias = jnp.stack([_window_bias(rel_bias, dil, tq) for dil in dils])
    hb = tb // hmax
    nhb = n // hmax
    prev = lambda i: jnp.maximum(i * hb - 1, 0)
    nxt = lambda i: jnp.minimum((i + 1) * hb, nhb - 1)
    cur_spec = lambda g: pl.BlockSpec((tb, HEAD_DIM), lambda i, h: (i, g * n_heads + h))
    prev_spec = lambda g: pl.BlockSpec((hmax, HEAD_DIM), lambda i, h: (prev(i), g * n_heads + h))
    next_spec = lambda g: pl.BlockSpec((hmax, HEAD_DIM), lambda i, h: (nxt(i), g * n_heads + h))
    np_ = len(dils)
    return pl.pallas_call(
        functools.partial(_win_body, tb=tb, tq=tq, dils=dils, seq_starts=seq_starts, seq_ends=seq_ends),
        out_shape=jax.ShapeDtypeStruct((n, n_heads * HEAD_DIM), BF16),
        grid=(n // tb, n_heads),
        in_specs=[cur_spec(0), prev_spec(1), cur_spec(1), next_spec(1),
                  prev_spec(2), cur_spec(2), next_spec(2),
                  pl.BlockSpec((np_, None, tq, tq + 2 * WIN_HALF), lambda i, h: (0, h, 0, 0))],
        out_specs=pl.BlockSpec((tb, HEAD_DIM), lambda i, h: (i, h)),
        scratch_shapes=[pltpu.VMEM((tb + 2 * hmax, HEAD_DIM), F32)] * 2 + [pltpu.VMEM((tb, HEAD_DIM), F32)] * (2 * np_),
        compiler_params=_cparams(("parallel", "parallel")),
        name="win_attn",
    )(proj, proj, proj, proj, proj, proj, proj, bias)


def _rope_half(x, cos, sin_signed):
    return x * cos + pltpu.roll(x, HEAD_DIM // 2, axis=1) * sin_signed


def _ret_bwd_body(lg_ref, q_ref, k_ref, v_ref, cos_ref, sin_ref, o_ref, r_ref, tab_ref, *,
                  n_heads, n_chunks, seq_ends):
    c = RET_CHUNK
    g = n_chunks - 1 - pl.program_id(0)

    @pl.when(pl.program_id(0) == 0)
    def _():
        n = lax.broadcasted_iota(jnp.int32, (c, HEAD_DIM), 0).astype(F32)
        for h in range(n_heads):
            lg = lg_ref[1, h]
            tab_ref[0, h] = jnp.exp(lg * (c - n))
            tab_ref[1, h] = jnp.exp(lg * n)
            tab_ref[2, h] = jnp.exp(jnp.full((HEAD_DIM, HEAD_DIM), lg * c, F32))

    @pl.when(_is_any((g + 1) * c, seq_ends))
    def _():
        r_ref[...] = jnp.zeros(r_ref.shape, F32)

    cos, sin = cos_ref[...], sin_ref[...]
    scale = HEAD_DIM ** -0.5
    for h in range(n_heads):
        sl = slice(h * HEAD_DIM, (h + 1) * HEAD_DIM)
        q = _rope_half(q_ref[:, sl], cos, sin) * scale
        k = _rope_half(k_ref[:, sl], cos, sin)
        v = v_ref[:, sl]
        r = r_ref[h]
        qw = (q * tab_ref[0, h]).astype(BF16)
        o_ref[:, sl] = jnp.dot(qw, r.astype(BF16), preferred_element_type=F32)
        kw_t = (k * tab_ref[1, h]).T.astype(BF16)
        r_ref[h] = tab_ref[2, h] * r + jnp.dot(kw_t, v.astype(BF16), preferred_element_type=F32)


def _ret_fwd_body(lg_ref, q_ref, k_ref, v_ref, gate_ref, xb_ref, cos_ref, sin_ref, o_ref, r_ref, tab_ref, *,
                  n_heads, seq_starts):
    c = RET_CHUNK
    g = pl.program_id(0)

    @pl.when(g == 0)
    def _():
        n = lax.broadcasted_iota(jnp.int32, (c, HEAD_DIM), 0).astype(F32)
        rel = (lax.broadcasted_iota(jnp.int32, (c, c), 0) - lax.broadcasted_iota(jnp.int32, (c, c), 1)).astype(F32)
        for h in range(n_heads):
            lg_f, lg_b = lg_ref[0, h], lg_ref[1, h]
            tab_ref[0, h] = jnp.where(rel >= 0, jnp.exp(lg_f * jnp.maximum(rel, 0.0)),
                                      jnp.exp(lg_b * jnp.maximum(-rel, 0.0)))
            tab_ref[1, h] = jnp.exp(lg_f * (n + 1.0))
            tab_ref[2, h] = jnp.exp(lg_f * (c - 1.0 - n))
            tab_ref[3, h] = jnp.exp(jnp.full((HEAD_DIM, HEAD_DIM), lg_f * c, F32))

    @pl.when(_is_any(g * c, seq_starts))
    def _():
        r_ref[...] = jnp.zeros(r_ref.shape, F32)

    cos, sin = cos_ref[...], sin_ref[...]
    scale = HEAD_DIM ** -0.5
    for h in range(n_heads):
        sl = slice(h * HEAD_DIM, (h + 1) * HEAD_DIM)
        q = _rope_half(q_ref[:, sl], cos, sin) * scale
        k = _rope_half(k_ref[:, sl], cos, sin)
        v = v_ref[:, sl].astype(BF16)
        r = r_ref[h]
        qk = lax.dot_general(q.astype(BF16), k.astype(BF16), (((1,), (1,)), ((), ())), preferred_element_type=F32)
        o = jnp.dot((qk * tab_ref[0, h]).astype(BF16), v, preferred_element_type=F32)
        qw = (q * tab_ref[1, h]).astype(BF16)
        o = o + jnp.dot(qw, r.astype(BF16), preferred_element_type=F32) + xb_ref[:, sl]
        mu = jnp.mean(o, axis=-1, keepdims=True)
        d = o - mu
        var = jnp.mean(d * d, axis=-1, keepdims=True)
        o = d * lax.rsqrt(var + EPS)
        gate = gate_ref[:, sl]
        o_ref[:, sl] = (gate * jax.nn.sigmoid(gate) * o).astype(o_ref.dtype)
        kw_t = (k * tab_ref[2, h]).T.astype(BF16)
        r_ref[h] = tab_ref[3, h] * r + jnp.dot(kw_t, v, preferred_element_type=F32)


def retention(proj, col0, log_gamma, cos, sin, n_heads, seq_starts, seq_ends):
    n, _ = proj.shape
    w = n_heads * HEAD_DIM
    cb0 = col0 // w
    c = RET_CHUNK
    assert c == HEAD_DIM
    n_chunks = n // c
    smem = pl.BlockSpec(memory_space=pltpu.SMEM)
    rev = lambda off: pl.BlockSpec((c, w), lambda s: (n_chunks - 1 - s, cb0 + off))
    rev_tab = pl.BlockSpec((c, HEAD_DIM), lambda s: (n_chunks - 1 - s, 0))
    cross_b = pl.pallas_call(
        functools.partial(_ret_bwd_body, n_heads=n_heads, n_chunks=n_chunks, seq_ends=seq_ends),
        out_shape=jax.ShapeDtypeStruct((n, w), F32),
        grid=(n_chunks,),
        in_specs=[smem, rev(0), rev(1), rev(2), rev_tab, rev_tab],
        out_specs=pl.BlockSpec((c, w), lambda s: (n_chunks - 1 - s, 0)),
        scratch_shapes=[pltpu.VMEM((n_heads, HEAD_DIM, HEAD_DIM), F32),
                        pltpu.VMEM((3, n_heads, c, HEAD_DIM), F32)],
        compiler_params=_cparams(("arbitrary",)),
        name="ret_bwd",
    )(log_gamma, proj, proj, proj, cos, sin)
    fwd = lambda off: pl.BlockSpec((c, w), lambda s: (s, cb0 + off))
    tab = pl.BlockSpec((c, HEAD_DIM), lambda s: (s, 0))
    return pl.pallas_call(
        functools.partial(_ret_fwd_body, n_heads=n_heads, seq_starts=seq_starts),
        out_shape=jax.ShapeDtypeStruct((n, w), BF16),
        grid=(n_chunks,),
        in_specs=[smem, fwd(0), fwd(1), fwd(2), fwd(3), pl.BlockSpec((c, w), lambda s: (s, 0)), tab, tab],
        out_specs=pl.BlockSpec((c, w), lambda s: (s, 0)),
        scratch_shapes=[pltpu.VMEM((n_heads, HEAD_DIM, HEAD_DIM), F32),
                        pltpu.VMEM((4, n_heads, c, HEAD_DIM), F32)],
        compiler_params=_cparams(("arbitrary",)),
        name="ret_fwd",
    )(log_gamma, proj, proj, proj, proj, cross_b, cos, sin)


def _rope_quarter(x, cos, sin_signed, low):
    partner = jnp.where(low, pltpu.roll(x, HEAD_DIM - 32, axis=1), pltpu.roll(x, 32, axis=1))
    return x * cos + partner * sin_signed


def _gqa_prep_body(q_ref, k_ref, v_ref, nw_ref, cos_ref, sin_ref, qo_ref, ko_ref, vto_ref, *, hq, hk):
    cos, sin = cos_ref[...], sin_ref[...]
    low = (lax.broadcasted_iota(jnp.int32, cos.shape, 1) % 64) < 32
    scale = HEAD_DIM ** -0.5 * LOG2E

    def norm_rope(x, w):
        y = x * lax.rsqrt(jnp.mean(x * x, axis=-1, keepdims=True) + EPS) * w
        return _rope_quarter(y, cos, sin, low)

    for h in range(hq):
        sl = slice(h * HEAD_DIM, (h + 1) * HEAD_DIM)
        qo_ref[:, sl] = (norm_rope(q_ref[:, sl], nw_ref[0:1, :]) * scale).astype(qo_ref.dtype)
    for h in range(hk):
        sl = slice(h * HEAD_DIM, (h + 1) * HEAD_DIM)
        ko_ref[:, sl] = norm_rope(k_ref[:, sl], nw_ref[1:2, :]).astype(ko_ref.dtype)
        vto_ref[sl, :] = v_ref[:, sl].T.astype(vto_ref.dtype)


def gqa_prep(proj, qk_norm, cos, sin, hq, hk, tm=256):
    n = proj.shape[0]
    tm = min(tm, n)
    wq, wk = hq * HEAD_DIM, hk * HEAD_DIM
    assert wq % wk == 0
    row = lambda width, blk: pl.BlockSpec((tm, width), lambda i: (i, blk))
    return pl.pallas_call(
        functools.partial(_gqa_prep_body, hq=hq, hk=hk),
        out_shape=[jax.ShapeDtypeStruct((n, wq), BF16), jax.ShapeDtypeStruct((n, wk), BF16),
                   jax.ShapeDtypeStruct((wk, n), BF16)],
        grid=(n // tm,),
        in_specs=[row(wq, 0), row(wk, wq // wk), row(wk, wq // wk + 1),
                  pl.BlockSpec((2, HEAD_DIM), lambda i: (0, 0)), row(HEAD_DIM, 0), row(HEAD_DIM, 0)],
        out_specs=[row(wq, 0), row(wk, 0), pl.BlockSpec((wk, tm), lambda i: (0, i))],
        compiler_params=_cparams(("parallel",)),
        name="gqa_prep",
    )(proj, proj, proj, qk_norm, cos, sin)


def _latent_norm_body(cq_ref, ckv_ref, wq_ref, wkv_ref, qo_ref, kvo_ref):
    def rms(x, w):
        return x * lax.rsqrt(jnp.mean(x * x, axis=-1, keepdims=True) + EPS) * w
    qo_ref[...] = rms(cq_ref[...], wq_ref[...]).astype(qo_ref.dtype)
    kvo_ref[...] = rms(ckv_ref[...], wkv_ref[...]).astype(kvo_ref.dtype)


def latent_norm(proj, col_cq, cq_norm, ckv_norm, tm=512):
    n = proj.shape[0]
    tm = min(tm, n)
    assert col_cq % Q_LORA == 0 and (col_cq + Q_LORA) % KV_LORA == 0
    return pl.pallas_call(
        _latent_norm_body,
        out_shape=[jax.ShapeDtypeStruct((n, Q_LORA), BF16), jax.ShapeDtypeStruct((n, KV_LORA), BF16)],
        grid=(n // tm,),
        in_specs=[pl.BlockSpec((tm, Q_LORA), lambda i: (i, col_cq // Q_LORA)),
                  pl.BlockSpec((tm, KV_LORA), lambda i: (i, (col_cq + Q_LORA) // KV_LORA)),
                  pl.BlockSpec((1, Q_LORA), lambda i: (0, 0)), pl.BlockSpec((1, KV_LORA), lambda i: (0, 0))],
        out_specs=[pl.BlockSpec((tm, Q_LORA), lambda i: (i, 0)), pl.BlockSpec((tm, KV_LORA), lambda i: (i, 0))],
        compiler_params=_cparams(("parallel",)),
        name="latent_norm",
    )(proj, proj, cq_norm.reshape(1, Q_LORA), ckv_norm.reshape(1, KV_LORA))


def _mla_q_body(a_ref, b_ref, cos_ref, sin_ref, qo_ref, *, heads):
    cos, sin = cos_ref[...], sin_ref[...]
    low = (lax.broadcasted_iota(jnp.int32, cos.shape, 1) % 64) < 32
    scale = (QK_NOPE + QK_ROPE) ** -0.5 * LOG2E
    r = jnp.dot(a_ref[...], b_ref[...], preferred_element_type=F32)
    for h in range(heads):
        a = slice(h * MLA_QK_PAD, h * MLA_QK_PAD + QK_NOPE)
        b = slice(h * MLA_QK_PAD + QK_NOPE, (h + 1) * MLA_QK_PAD)
        qo_ref[:, a] = (r[:, a] * scale).astype(qo_ref.dtype)
        qo_ref[:, b] = (_rope_quarter(r[:, b], cos, sin, low) * scale).astype(qo_ref.dtype)


def _mla_kv_body(a_ref, b_ref, kpe_ref, cos_ref, sin_ref, ko_ref, vto_ref, *, heads):
    cos, sin = cos_ref[...], sin_ref[...]
    low = (lax.broadcasted_iota(jnp.int32, cos.shape, 1) % 64) < 32
    k_rope = _rope_quarter(kpe_ref[...], cos, sin, low).astype(ko_ref.dtype)
    r = jnp.dot(a_ref[...], b_ref[...], preferred_element_type=F32)
    for h in range(heads):
        a = slice(h * MLA_QK_PAD, h * MLA_QK_PAD + QK_NOPE)
        b = slice(h * MLA_QK_PAD + QK_NOPE, (h + 1) * MLA_QK_PAD)
        ko_ref[:, a] = r[:, a].astype(ko_ref.dtype)
        ko_ref[:, b] = k_rope
        vto_ref[h * V_D:(h + 1) * V_D, :] = r[:, b].T.astype(vto_ref.dtype)


def mla_up_proj(cqn, ckvn, w_uq, w_ukv, layer, proj, col_kpe, cos, sin, n_heads, tm=512, heads_per_step=4):
    n = cqn.shape[0]
    tm = min(tm, n)
    hs = min(heads_per_step, n_heads)
    assert n_heads % hs == 0 and col_kpe % 128 == 0
    tn = hs * MLA_QK_PAD
    wd = n_heads * MLA_QK_PAD
    grid = (n // tm, n_heads // hs)
    tab = pl.BlockSpec((tm, 128), lambda i, j: (i, 0))
    q = pl.pallas_call(
        functools.partial(_mla_q_body, heads=hs),
        out_shape=jax.ShapeDtypeStruct((n, wd), BF16),
        grid=grid,
        in_specs=[pl.BlockSpec((tm, Q_LORA), lambda i, j: (i, 0)),
                  _layer_spec(w_uq, layer, (Q_LORA, tn), lambda i, j: (0, j)),
                  tab, tab],
        out_specs=pl.BlockSpec((tm, tn), lambda i, j: (i, j)),
        compiler_params=_cparams(("parallel", "parallel")),
        name="mla_q_proj",
    )(cqn, w_uq, cos, sin)
    k, vt = pl.pallas_call(
        functools.partial(_mla_kv_body, heads=hs),
        out_shape=[jax.ShapeDtypeStruct((n, wd), BF16), jax.ShapeDtypeStruct((n_heads * V_D, n), BF16)],
        grid=grid,
        in_specs=[pl.BlockSpec((tm, KV_LORA), lambda i, j: (i, 0)),
                  _layer_spec(w_ukv, layer, (KV_LORA, tn), lambda i, j: (0, j)),
                  pl.BlockSpec((tm, 128), lambda i, j: (i, col_kpe // 128)), tab, tab],
        out_specs=[pl.BlockSpec((tm, tn), lambda i, j: (i, j)),
                   pl.BlockSpec((hs * V_D, tm), lambda i, j: (j, i))],
        compiler_params=_cparams(("parallel", "parallel")),
        name="mla_kv_proj",
    )(ckvn, w_ukv, proj, cos, sin)
    return q, k, vt


def _inv_freq(dim):
    return (np.float32(ROPE_THETA) ** (-np.arange(0, dim, 2, dtype=np.float32) / np.float32(dim))).astype(np.float32)


def _angles(pos, dim):
    ang = pos.astype(F32)[:, None] * jnp.asarray(_inv_freq(dim))[None, :]
    return jnp.cos(ang), jnp.sin(ang)


def _rope_tables(seq_lens):
    tmax = max(seq_lens)
    pos = jnp.arange(tmax)
    per_token = lambda t: jnp.concatenate([t[:n] for n in seq_lens], axis=0)
    c, s = _angles(pos, HEAD_DIM)
    ret = (per_token(jnp.concatenate([c, c], -1)), per_token(jnp.concatenate([-s, s], -1)))
    rows = -(-tmax // GRID_W)
    cr, sr = (jnp.repeat(t, GRID_W, axis=0)[:tmax] for t in _angles(jnp.arange(rows), HEAD_DIM // 2))
    cc, sc = (jnp.tile(t, (rows, 1))[:tmax] for t in _angles(jnp.arange(GRID_W), HEAD_DIM // 2))
    axial = (per_token(jnp.concatenate([cr, cr, cc, cc], -1)), per_token(jnp.concatenate([-sr, sr, -sc, sc], -1)))
    c1, s1 = _angles(pos, QK_ROPE)
    one, zero = jnp.ones_like(c1), jnp.zeros_like(s1)
    mla = (per_token(jnp.concatenate([c1, c1, one, one], -1)), per_token(jnp.concatenate([-s1, s1, zero, zero], -1)))
    return ret, axial, mla


def _pad_last(w, width):
    return jnp.pad(w, ((0, 0),) * (w.ndim - 1) + ((0, width - w.shape[-1]),))


def _mix_even(h, w_in, i, log_1m_gamma, rel_bias, ret_tab, seqs):
    seq_starts, seq_ends = seqs
    n_heads = w_in.shape[-1] // (7 * HEAD_DIM)
    w = n_heads * HEAD_DIM
    proj = matmul(h, w_in, layer=i, out_dtype=F32, name="mm_in_even")
    assert all(window // (2 * dil) == WIN_HALF for window, dil in DIL_PATTERNS)
    dils = tuple(dil for _, dil in DIL_PATTERNS)
    oa = window_attention(proj, rel_bias, dils, n_heads, seq_starts, seq_ends)
    log_gamma = jnp.log1p(-jnp.exp(log_1m_gamma.astype(F32)))
    ob = retention(proj, 3 * w, log_gamma, ret_tab[0], ret_tab[1], n_heads, seq_starts, seq_ends)
    return [oa], [ob]


def _mix_odd(h, w_in, in_odd, i, qk_norm, cq_norm, ckv_norm, w_uq, w_ukv, ax_tab, mla_tab, groups, tiles):
    wkv = KV_C * HEAD_DIM
    wq = in_odd - 2 * wkv - Q_LORA - KV_LORA - QK_ROPE
    hq = wq // HEAD_DIM
    proj = matmul(h, w_in, layer=i, out_dtype=F32, name="mm_in_odd")
    qc, kc, vtc = gqa_prep(proj, qk_norm, ax_tab[0], ax_tab[1], hq, KV_C)
    col_cq = wq + 2 * wkv
    cqn, ckvn = latent_norm(proj, col_cq, cq_norm, ckv_norm)
    n_hd = w_uq.shape[-1] // MLA_QK_PAD
    assert V_D == MLA_QK_PAD - QK_NOPE and w_ukv.shape[-1] == n_hd * MLA_QK_PAD
    q_d, k_d, vt_d = mla_up_proj(cqn, ckvn, w_uq, w_ukv, i, proj, col_cq + Q_LORA + KV_LORA,
                                 mla_tab[0], mla_tab[1], n_hd)
    ocs, ods = [], []
    for row0, batch, seq_len in groups:
        ocs.append(flash_attention(qc, kc, vtc, n_kv_heads=KV_C, g_heads=hq // KV_C, dqk=HEAD_DIM, dv=HEAD_DIM,
                                   row0=row0, batch=batch, seq_len=seq_len, tq=tiles[0], tk=tiles[1],
                                   nsplit=tiles[2], name="attn_gqa"))
        ods.append(flash_attention(q_d, k_d, vt_d, n_kv_heads=n_hd, g_heads=1, dqk=MLA_QK_PAD, dv=V_D,
                                   row0=row0, batch=batch, seq_len=seq_len, tq=tiles[3], tk=tiles[4],
                                   nsplit=tiles[5], name="attn_mla"))
    return ocs, ods


def _conv_ffn(h, w_up, conv_w, w_down, layer, x, gate, seqs):
    seq_starts, seq_ends = seqs
    dffp = w_down.shape[-2]
    act = ff_up_conv_gate(h, w_up, conv_w, layer, dffp, seq_starts, seq_ends)
    return matmul(act, w_down, layer=layer, out_dtype=F32, tn=1024, tk=2816, resid=x, gate=gate,
                  seq_starts=seq_starts, name="mm_ff_down")


def _trunk(x_prompt, x_sample, c_prompt, c_sample, rel_bias, norm_w, w_mod, b_mod, w_in_even,
           ret_log_1m_gamma, w_out_even, w_in_odd, c_qk_norm, mla_cq_norm, mla_ckv_norm, w_uq,
           w_ukv, w_out_odd, w_ff_up, conv_ff, w_ff_down, final_norm_w, *, attn_tiles, ff_mult):
    bp, tp, d = x_prompt.shape
    bs, ts, _ = x_sample.shape
    depth = norm_w.shape[0]
    seq_lens = (tp,) * bp + (ts,) * bs
    seq_starts = tuple(int(v) for v in np.cumsum((0,) + seq_lens[:-1]))
    seq_ends = tuple(int(v) for v in np.cumsum(seq_lens))
    seqs = (seq_starts, seq_ends)
    groups = ((0, bp, tp), (bp * tp, bs, ts))
    n_seq = len(seq_lens)

    x = jnp.concatenate([x_prompt.reshape(bp * tp, d), x_sample.reshape(bs * ts, d)], axis=0)
    c = jnp.concatenate([c_prompt, c_sample], axis=0)
    c_act = jnp.pad(jax.nn.silu(c), ((0, 8 - n_seq), (0, 0))).astype(BF16)
    ret_tab, ax_tab, mla_tab = _rope_tables(seq_lens)

    in_odd = w_in_odd.shape[-1]
    w_in_even_b = w_in_even.astype(BF16)
    w_in_odd_b = _pad_last(w_in_odd, -(-in_odd // 512) * 512).astype(BF16)
    w_out_b = (w_out_even.astype(BF16), w_out_odd.astype(BF16))
    n_hd = w_uq.shape[-1] // (QK_NOPE + QK_ROPE)
    w_uq_b = _pad_last(w_uq.reshape(-1, Q_LORA, n_hd, QK_NOPE + QK_ROPE), MLA_QK_PAD)
    w_uq_b = w_uq_b.reshape(-1, Q_LORA, n_hd * MLA_QK_PAD).astype(BF16)
    w_ukv_b = w_ukv.astype(BF16)
    dff = w_ff_down.shape[-2]
    dffp = -(-dff // ff_mult) * ff_mult
    split_pad = lambda w: jnp.concatenate([_pad_last(w[..., :dff], dffp), _pad_last(w[..., dff:], dffp)], axis=-1)
    w_up_b = split_pad(w_ff_up).astype(BF16)
    conv_p = split_pad(conv_ff)
    w_down_b = jnp.pad(w_ff_down, ((0, 0), (0, dffp - dff), (0, 0))).astype(BF16)
    b_mod3 = b_mod.reshape(depth, 1, -1)

    for layer in range(depth):
        mod = matmul(c_act, w_mod, layer=layer, out_dtype=F32, tm=8, tn=2048, tk=1024, bias=b_mod3, name="mm_mod")
        mod6 = mod[:n_seq].reshape(n_seq * 6, 1, d)
        gate_of = lambda ci: mod6.reshape(n_seq, 6, d)[:, ci].reshape(n_seq, 1, d)
        h = norm_mod(x, norm_w[layer, 0], mod6, 1, 0, seq_starts)
        i = layer // 2
        if layer % 2 == 0:
            mix = _mix_even(h, w_in_even_b, i, ret_log_1m_gamma[i], rel_bias, ret_tab, seqs)
        else:
            mix = _mix_odd(h, w_in_odd_b, in_odd, i, c_qk_norm[i], mla_cq_norm[i], mla_ckv_norm[i], w_uq_b, w_ukv_b,
                           ax_tab, mla_tab, groups, attn_tiles)
        x = matmul_cat_resid(mix[0], mix[1], w_out_b[layer % 2], i, x, gate_of(2), seq_starts)
        h = norm_mod(x, norm_w[layer, 1], mod6, 4, 3, seq_starts)
        x = _conv_ffn(h, w_up_b, conv_p, w_down_b, layer, x, gate_of(5), seqs)
    y_p, y_s = final_norm(x, final_norm_w, bp * tp)
    return y_p.reshape(bp, tp, d), y_s.reshape(bs, ts, d)


def kernel(x_prompt, x_sample, c_prompt, c_sample, rel_bias, norm_w, w_mod, b_mod, w_in_even, ret_log_1m_gamma, w_out_even, w_in_odd, c_qk_norm, mla_cq_norm, mla_ckv_norm, w_uq, w_ukv, w_out_odd, w_ff_up, conv_ff, w_ff_down, final_norm_w):
    return _trunk(x_prompt, x_sample, c_prompt, c_sample, rel_bias, norm_w, w_mod, b_mod, w_in_even,
                  ret_log_1m_gamma, w_out_even, w_in_odd, c_qk_norm, mla_cq_norm, mla_ckv_norm, w_uq,
                  w_ukv, w_out_odd, w_ff_up, conv_ff, w_ff_down, final_norm_w,
                  attn_tiles=(512, 4096, 8, 2048, 4096, 8), ff_mult=1024)
```
